```python
import jax, jax.numpy as jnp
from jax import lax
import numpy as np

D_MODEL = 1024
BATCH = 16
SEQ = 4096
DEPTH = 1
DEC_BATCH = 2
DEC_SEQ = 8192
PAST_LEN = 128

GLA_HEADS = 4
GLA_DK = D_MODEL // 2
GLA_DV = D_MODEL
HEAD_K = GLA_DK // GLA_HEADS
HEAD_V = GLA_DV // GLA_HEADS
GATE_RANK = 16
GATE_TAU = 16.0
CHUNK = 64
CONV_DIM = D_MODEL
CONV_WIDTH = 3
N_EXPERTS = 16
EC_FACTOR = 2
D_EXPERT = 2816
EPS = 1e-6
IN_SIZES = (GLA_DK, GLA_DK, GLA_DV, GLA_DV, GATE_RANK, GATE_RANK, CONV_DIM, CONV_DIM, CONV_DIM, D_MODEL, D_MODEL)
D_IN = 2 * GLA_DK + 2 * GLA_DV + 2 * GATE_RANK + 3 * CONV_DIM + 2 * D_MODEL

kernel_name = "hybrid_gla_shortconv_ec_moe_encoder"


def _split_points():
    pts, acc = [], 0
    for s in IN_SIZES[:-1]:
        acc += s
        pts.append(acc)
    return pts


def rmsnorm(x, w):
    xf = x.astype(jnp.float32)
    xf = xf * lax.rsqrt(jnp.mean(xf * xf, axis=-1, keepdims=True) + EPS)
    return (xf * w.astype(jnp.float32)).astype(x.dtype)


def gla_one_direction(q, k, v, log_a):
    B, L, H, dk = q.shape
    dv = v.shape[-1]
    n = L // CHUNK
    q = q.reshape(B, n, CHUNK, H, dk)
    k = k.reshape(B, n, CHUNK, H, dk)
    v = v.reshape(B, n, CHUNK, H, dv)
    b = jnp.cumsum(log_a.reshape(B, n, CHUNK, H, dk), axis=2)
    b_last = b[:, :, -1:]
    q_t = q * jnp.exp(b)
    k_t = k * jnp.exp(-b)
    k_end = k * jnp.exp(b_last - b)
    mask = jnp.tril(jnp.ones((CHUNK, CHUNK), dtype=bool))
    att = jnp.einsum('bnihd,bnjhd->bnhij', q_t, k_t)
    att = jnp.where(mask, att, 0.0)
    o_intra = jnp.einsum('bnhij,bnjhv->bnihv', att, v)
    decay = jnp.exp(b_last[:, :, 0])

    def step(S, xs):
        qc, kc, vc, dc = xs
        o = jnp.einsum('bihd,bhdv->bihv', qc, S)
        S = dc[..., None] * S + jnp.einsum('bjhd,bjhv->bhdv', kc, vc)
        return S, o

    S0 = jnp.zeros((B, H, dk, dv), jnp.float32)
    xs = (jnp.moveaxis(q_t, 1, 0), jnp.moveaxis(k_end, 1, 0), jnp.moveaxis(v, 1, 0), jnp.moveaxis(decay, 1, 0))
    _, o_inter = lax.scan(step, S0, xs)
    o = o_intra + jnp.moveaxis(o_inter, 0, 1)
    return o.reshape(B, L, H, dv)


def token_mixer(u, w_in, w_alpha_f, b_alpha_f, w_alpha_b, b_alpha_b, gla_norm_w,
                conv_w, conv_b, w_out_a, w_out_b, w_o):
    B, L, _ = u.shape
    f32 = jnp.float32
    z = u @ w_in
    q, k, v, r, lr_f, lr_b, cb, cc, cx, ga, gb = jnp.split(z, _split_points(), axis=-1)

    qh = (q.astype(f32) * (HEAD_K ** -0.5)).reshape(B, L, GLA_HEADS, HEAD_K)
    kh = k.astype(f32).reshape(B, L, GLA_HEADS, HEAD_K)
    vh = v.astype(f32).reshape(B, L, GLA_HEADS, HEAD_V)
    la_f = (jax.nn.log_sigmoid((lr_f @ w_alpha_f + b_alpha_f).astype(f32)) / GATE_TAU).reshape(B, L, GLA_HEADS, HEAD_K)
    la_b = (jax.nn.log_sigmoid((lr_b @ w_alpha_b + b_alpha_b).astype(f32)) / GATE_TAU).reshape(B, L, GLA_HEADS, HEAD_K)
    o_f = gla_one_direction(qh, kh, vh, la_f)
    o_b = jnp.flip(gla_one_direction(jnp.flip(qh, 1), jnp.flip(kh, 1), jnp.flip(vh, 1), jnp.flip(la_b, 1)), 1)
    o = o_f + o_b
    o = o * lax.rsqrt(jnp.mean(o * o, axis=-1, keepdims=True) + EPS) * gla_norm_w.astype(f32)
    o = o.reshape(B, L, GLA_DV).astype(u.dtype) * jax.nn.silu(r)
    branch_a = o @ w_out_a

    xc = cc * cx
    xp = jnp.pad(xc, ((0, 0), (1, 1), (0, 0)))
    conv = xp[:, :-2] * conv_w[0] + xp[:, 1:-1] * conv_w[1] + xp[:, 2:] * conv_w[2] + conv_b
    branch_b = (cb * conv) @ w_out_b

    merged = jax.nn.sigmoid(ga) * branch_a + jax.nn.sigmoid(gb) * branch_b
    return merged @ w_o


def expert_choice_ffn(u, w_router, w_gate, w_up, w_down):
    B, L, D = u.shape
    T = B * L
    cap = EC_FACTOR * T // N_EXPERTS
    xt = u.reshape(T, D)
    aff = jax.nn.softmax((xt @ w_router).astype(jnp.float32), axis=-1)
    g, idx = lax.top_k(aff.T, cap)
    xe = xt[idx]
    h = jax.nn.silu(jnp.einsum('ecd,edf->ecf', xe, w_gate)) * jnp.einsum('ecd,edf->ecf', xe, w_up)
    ye = jnp.einsum('ecf,efd->ecd', h, w_down) * g[..., None].astype(u.dtype)
    y = jnp.zeros_like(xt).at[idx.reshape(-1)].add(ye.reshape(-1, D))
    return y.reshape(B, L, D)


def trunk(x, c, w_ada, b_ada, norm1_w, w_in, w_alpha_f, b_alpha_f, w_alpha_b, b_alpha_b,
          gla_norm_w, conv_w, conv_b, w_out_a, w_out_b, w_o, norm2_w, w_router,
          w_gate, w_up, w_down, norm_f_w):
    for l in range(DEPTH):
        mod = jax.nn.silu(c) @ w_ada[l] + b_ada[l]
        sh1, sc1, g1, sh2, sc2, g2 = jnp.split(mod[:, None, :], 6, axis=-1)
        u = rmsnorm(x, norm1_w[l]) * (1.0 + sc1) + sh1
        x = x + g1 * token_mixer(u, w_in[l], w_alpha_f[l], b_alpha_f[l], w_alpha_b[l], b_alpha_b[l],
                                 gla_norm_w[l], conv_w[l], conv_b[l], w_out_a[l], w_out_b[l], w_o[l])
        u = rmsnorm(x, norm2_w[l]) * (1.0 + sc2) + sh2
        x = x + g2 * expert_choice_ffn(u, w_router[l], w_gate[l], w_up[l], w_down[l])
    return rmsnorm(x, norm_f_w)


def setup_inputs(seed: int = 0) -> dict:
    key = jax.random.key(seed)
    ks = jax.random.split(key, 26)
    nrm = jax.random.normal
    D = D_MODEL
    return {
        "x_prompt": nrm(ks[0], (BATCH, SEQ, D), jnp.float32),
        "x_sample": nrm(ks[1], (DEC_BATCH, DEC_SEQ, D), jnp.float32),
        "c_prompt": nrm(ks[2], (BATCH, D), jnp.float32),
        "c_sample": nrm(ks[3], (DEC_BATCH, D), jnp.float32),
        "w_ada": nrm(ks[4], (DEPTH, D, 6 * D), jnp.float32) * (0.3 * D ** -0.5),
        "b_ada": nrm(ks[5], (DEPTH, 6 * D), jnp.float32) * 0.02,
        "norm1_w": 1.0 + 0.02 * nrm(ks[6], (DEPTH, D), jnp.float32),
        "w_in": nrm(ks[7], (DEPTH, D, D_IN), jnp.float32) * D ** -0.5,
        "w_alpha_f": nrm(ks[8], (DEPTH, GATE_RANK, GLA_DK), jnp.float32) * GATE_RANK ** -0.5,
        "b_alpha_f": 0.1 * nrm(ks[9], (DEPTH, GLA_DK), jnp.float32),
        "w_alpha_b": nrm(ks[10], (DEPTH, GATE_RANK, GLA_DK), jnp.float32) * GATE_RANK ** -0.5,
        "b_alpha_b": 0.1 * nrm(ks[11], (DEPTH, GLA_DK), jnp.float32),
        "gla_norm_w": 1.0 + 0.02 * nrm(ks[12], (DEPTH, GLA_HEADS, HEAD_V), jnp.float32),
        "conv_w": nrm(ks[13], (DEPTH, CONV_WIDTH, CONV_DIM), jnp.float32) * CONV_WIDTH ** -0.5,
        "conv_b": 0.02 * nrm(ks[14], (DEPTH, CONV_DIM), jnp.float32),
        "w_out_a": nrm(ks[15], (DEPTH, GLA_DV, D), jnp.float32) * GLA_DV ** -0.5,
        "w_out_b": nrm(ks[16], (DEPTH, CONV_DIM, D), jnp.float32) * CONV_DIM ** -0.5,
        "w_o": nrm(ks[17], (DEPTH, D, D), jnp.float32) * D ** -0.5,
        "norm2_w": 1.0 + 0.02 * nrm(ks[18], (DEPTH, D), jnp.float32),
        "w_router": nrm(ks[19], (DEPTH, D, N_EXPERTS), jnp.float32) * D ** -0.5,
        "w_gate": nrm(ks[20], (DEPTH, N_EXPERTS, D, D_EXPERT), jnp.float32) * D ** -0.5,
        "w_up": nrm(ks[21], (DEPTH, N_EXPERTS, D, D_EXPERT), jnp.float32) * D ** -0.5,
        "w_down": nrm(ks[22], (DEPTH, N_EXPERTS, D_EXPERT, D), jnp.float32) * D_EXPERT ** -0.5,
        "norm_f_w": 1.0 + 0.02 * nrm(ks[23], (D,), jnp.float32),
    }


def reference(x_prompt, x_sample, c_prompt, c_sample, w_ada, b_ada, norm1_w, w_in,
              w_alpha_f, b_alpha_f, w_alpha_b, b_alpha_b, gla_norm_w, conv_w, conv_b,
              w_out_a, w_out_b, w_o, norm2_w, w_router, w_gate, w_up, w_down, norm_f_w):
    y_prompt = trunk(x_prompt, c_prompt, w_ada, b_ada, norm1_w, w_in, w_alpha_f, b_alpha_f,
                     w_alpha_b, b_alpha_b, gla_norm_w, conv_w, conv_b, w_out_a, w_out_b, w_o,
                     norm2_w, w_router, w_gate, w_up, w_down, norm_f_w)
    y_sample = trunk(x_sample, c_sample, w_ada, b_ada, norm1_w, w_in, w_alpha_f, b_alpha_f,
                     w_alpha_b, b_alpha_b, gla_norm_w, conv_w, conv_b, w_out_a, w_out_b, w_o,
                     norm2_w, w_router, w_gate, w_up, w_down, norm_f_w)
    return (y_prompt, y_sample)
```

```python
import functools

import numpy as np
import jax
import jax.numpy as jnp
from jax import lax
from jax.experimental import pallas as pl
from jax.experimental.pallas import tpu as pltpu

F32 = jnp.float32
BF16 = jnp.bfloat16

D_MODEL = 1024
GLA_HEADS = 4
HEAD_K = 128
HEAD_V = 256
GATE_RANK = 16
GATE_TAU = 16.0
CHUNK = 64
N_EXPERTS = 16
EC_FACTOR = 2
D_EXPERT = 2816
EPS = 1e-6

Z_Q, Z_K, Z_V, Z_R, Z_CB, Z_CC, Z_CX, Z_GA, Z_GB, Z_LR = 0, 512, 1024, 2048, 3072, 4096, 5120, 6144, 7168, 8192
Z_WIDTH = 8448

UNIT = 4 * CHUNK
SUB = 128
WIN = SUB + 16
VMEM_LIMIT = 56 * 1024 * 1024


def _dot(a, b):
    return jnp.dot(a, b, preferred_element_type=F32)


def _dot_nt(a, b):
    return lax.dot_general(a, b, (((1,), (1,)), ((), ())), preferred_element_type=F32)


def _dot_tn(a, b):
    return lax.dot_general(a, b, (((0,), (0,)), ((), ())), preferred_element_type=F32)


def _params(sem, limit=VMEM_LIMIT):
    return pltpu.CompilerParams(dimension_semantics=sem, vmem_limit_bytes=limit)


def _adaln_kernel(c_ref, w_ref, b_ref, o_ref):
    c = c_ref[...]
    s = c * jax.nn.sigmoid(c)
    o_ref[...] = jnp.dot(s, w_ref[...], preferred_element_type=F32,
                         precision=lax.Precision.HIGHEST) + b_ref[...]


def _adaln(c_all, w_ada, b_ada):
    nb, d = c_all.shape
    n = w_ada.shape[1]
    tn = 1536
    return pl.pallas_call(
        _adaln_kernel,
        out_shape=jax.ShapeDtypeStruct((nb, n), F32),
        grid=(n // tn,),
        in_specs=[pl.BlockSpec((nb, d), lambda j: (0, 0)),
                  pl.BlockSpec((d, tn), lambda j: (0, j)),
                  pl.BlockSpec((1, tn), lambda j: (0, j))],
        out_specs=pl.BlockSpec((nb, tn), lambda j: (0, j)),
        compiler_params=_params(("arbitrary",)),
        name="adaln",
    )(c_all, w_ada, b_ada.reshape(1, n))


def _inproj_kernel(x_ref, mod_ref, n1_ref, w_ref, z_ref, u_scr):
    @pl.when(pl.program_id(2) == 0)
    def _():
        x = x_ref[0]
        ms = jnp.mean(x * x, axis=-1, keepdims=True)
        xn = x * lax.rsqrt(ms + EPS) * n1_ref[...]
        m = mod_ref[0]
        u_scr[...] = (xn * (1.0 + m[1:2]) + m[0:1]).astype(BF16)

    z_ref[0] = _dot(u_scr[...], w_ref[...]).astype(BF16)


def _in_proj(x, mod3, boff, norm1_w, w_in_p):
    b, l, d = x.shape
    tm = min(1024, l)
    tn = 2816
    return pl.pallas_call(
        _inproj_kernel,
        out_shape=jax.ShapeDtypeStruct((b, l, Z_WIDTH), BF16),
        grid=(b, l // tm, Z_WIDTH // tn),
        in_specs=[pl.BlockSpec((1, tm, d), lambda bi, i, n: (bi, i, 0)),
                  pl.BlockSpec((1, 6, d), lambda bi, i, n: (bi + boff, 0, 0)),
                  pl.BlockSpec((1, d), lambda bi, i, n: (0, 0)),
                  pl.BlockSpec((d, tn), lambda bi, i, n: (0, n))],
        out_specs=pl.BlockSpec((1, tm, tn), lambda bi, i, n: (bi, i, n)),
        scratch_shapes=[pltpu.VMEM((tm, d), BF16)],
        compiler_params=_params(("arbitrary", "arbitrary", "arbitrary")),
        name="in_proj",
    )(x, mod3, norm1_w.reshape(1, d), w_in_p)


def _gla_constants(reverse):
    r = np.arange(UNIT)[:, None]
    s = np.arange(UNIT)[None, :]
    cr, cs = r // CHUNK, s // CHUNK
    if reverse:
        tri = (cr == cs) & (s >= r)
        tr, ts = 3 - cr, 3 - cs
    else:
        tri = (cr == cs) & (s <= r)
        tr, ts = cr, cs
    code = np.zeros((UNIT, UNIT), np.int32)
    code[tri] = 1
    code[(tr == ts + 1) & (tr % 2 == 1)] = 2
    code[(tr >= 2) & (ts <= 1)] = 3
    return jnp.asarray(tri, BF16), jnp.asarray(code)


def _gla_kernel(reverse, nunits, has_prev, *refs):
    if has_prev:
        q_ref, k_ref, v_ref, lr_ref, wp_ref, bp_ref, tri_ref, code_ref, prev_ref, o_ref, s_scr = refs
    else:
        q_ref, k_ref, v_ref, lr_ref, wp_ref, bp_ref, tri_ref, code_ref, o_ref, s_scr = refs
        prev_ref = None

    @pl.when(pl.program_id(2) == 0)
    def _():
        s_scr[...] = jnp.zeros_like(s_scr)

    order = (3, 2, 1, 0) if reverse else (0, 1, 2, 3)

    def to_mem(by_step):
        out = [None] * 4
        for t in range(4):
            out[order[t]] = by_step[t]
        return out

    def scale_chunks(arr, logf):
        parts = []
        for c in range(4):
            part = arr[c * CHUNK:(c + 1) * CHUNK]
            if logf[c] is not None:
                part = part * jnp.exp(logf[c])
            parts.append(part)
        return jnp.concatenate(parts, axis=0).astype(BF16)

    def unit(ui, carry):
        um = (nunits - 1 - ui) if reverse else ui
        rows = pl.ds(pl.multiple_of(um * UNIT, UNIT), UNIT)
        pre = _dot(lr_ref[0, rows, :], wp_ref[0]) + bp_ref[0]
        la = (jnp.minimum(pre, 0.0) - jnp.log(1.0 + jnp.exp(-jnp.abs(pre)))) * (1.0 / GATE_TAU)
        hi = la.astype(BF16)
        lo = (la - hi.astype(F32)).astype(BF16)
        bb = _dot(tri_ref[...], jnp.concatenate([hi, lo], axis=1))
        b = bb[:, :HEAD_K] + bb[:, HEAD_K:]
        edge = 0 if reverse else CHUNK - 1
        tot = [b[c * CHUNK + edge:c * CHUNK + edge + 1] for c in range(4)]
        tot_full = jnp.concatenate([jnp.broadcast_to(t, (CHUNK, HEAD_K)) for t in tot], axis=0)

        q = q_ref[0, rows, :].astype(F32) * (HEAD_K ** -0.5)
        k = k_ref[0, rows, :].astype(F32)
        q_t = q * jnp.exp(b)
        k_t = (k * jnp.exp(-b)).astype(BF16)
        k_e = k * jnp.exp(tot_full - b)

        bt = [tot[o] for o in order]
        fq = to_mem([None, None, None, bt[2]])
        fk = to_mem([bt[1], None, None, None])
        gq = to_mem([None, bt[0], bt[0] + bt[1], bt[0] + bt[1] + bt[2]])
        gk = to_mem([bt[1] + bt[2] + bt[3], bt[2] + bt[3], bt[3], None])
        dec = jnp.exp(bt[0] + bt[1] + bt[2] + bt[3])

        q_tb = q_t.astype(BF16)
        k_eb = k_e.astype(BF16)
        m1 = _dot_nt(q_tb, k_t)
        m2 = _dot_nt(q_tb, k_eb)
        m3 = _dot_nt(scale_chunks(q_t, fq), scale_chunks(k_e, fk))
        code = code_ref[...]
        att = jnp.where(code == 1, m1, jnp.where(code == 2, m2, jnp.where(code == 3, m3, 0.0)))

        vb = v_ref[0, rows, :]
        st = s_scr[...]
        o = _dot(att.astype(BF16), vb) + _dot_nt(scale_chunks(q_t, gq), st.astype(BF16))
        s_scr[...] = st * dec + _dot_tn(vb, scale_chunks(k_e, gk))
        if prev_ref is not None:
            o = o + prev_ref[0, rows, :]
        o_ref[0, rows, :] = o
        return carry

    lax.fori_loop(0, nunits, unit, 0)


def _gla(z, wpre, bpre, reverse, prev):
    b, l, _ = z.shape
    lb = min(1024, l)
    nb = l // lb
    nunits = lb // UNIT
    tri, code = _gla_constants(reverse)

    def seq(i):
        return (nb - 1 - i) if reverse else i

    in_specs = [
        pl.BlockSpec((1, lb, HEAD_K), lambda bi, h, i: (bi, seq(i), Z_Q // HEAD_K + h)),
        pl.BlockSpec((1, lb, HEAD_K), lambda bi, h, i: (bi, seq(i), Z_K // HEAD_K + h)),
        pl.BlockSpec((1, lb, HEAD_V), lambda bi, h, i: (bi, seq(i), Z_V // HEAD_V + h)),
        pl.BlockSpec((1, lb, 128), lambda bi, h, i: (bi, seq(i), Z_LR // 128)),
        pl.BlockSpec((1, 128, HEAD_K), lambda bi, h, i: (h, 0, 0)),
        pl.BlockSpec((1, 1, HEAD_K), lambda bi, h, i: (h, 0, 0)),
        pl.BlockSpec((UNIT, UNIT), lambda bi, h, i: (0, 0)),
        pl.BlockSpec((UNIT, UNIT), lambda bi, h, i: (0, 0)),
    ]
    args = [z, z, z, z, wpre, bpre, tri, code]
    if prev is not None:
        in_specs.append(pl.BlockSpec((1, lb, HEAD_V), lambda bi, h, i: (bi, seq(i), h)))
        args.append(prev)
    return pl.pallas_call(
        functools.partial(_gla_kernel, reverse, nunits, prev is not None),
        out_shape=jax.ShapeDtypeStruct((b, l, GLA_HEADS * HEAD_V), F32),
        grid=(b, GLA_HEADS, nb),
        in_specs=in_specs,
        out_specs=pl.BlockSpec((1, lb, HEAD_V), lambda bi, h, i: (bi, seq(i), h)),
        scratch_shapes=[pltpu.VMEM((HEAD_V, HEAD_K), F32)],
        compiler_params=_params(("arbitrary", "arbitrary", "arbitrary")),
        name="gla_bwd" if reverse else "gla_fwd",
    )(*args)


def _mix_kernel(tm, x_ref, o_ref, r_ref, cb_ref, cc_ref, cx_ref, ccp_ref, cxp_ref, ccn_ref, cxn_ref,
                ga_ref, gb_ref, mod_ref, gnw_ref, cw_ref, cbias_ref, woa_ref, wob_ref, wo_ref,
                n2_ref, wrt_ref, x1_ref, u2_ref, aff_ref):
    i = pl.program_id(1)
    last = pl.num_programs(1) - 1

    o = o_ref[0]
    parts = []
    for h in range(GLA_HEADS):
        oh = o[:, h * HEAD_V:(h + 1) * HEAD_V]
        ms = jnp.mean(oh * oh, axis=-1, keepdims=True)
        parts.append(oh * lax.rsqrt(ms + EPS))
    on = jnp.concatenate(parts, axis=1) * gnw_ref[...]
    r = r_ref[0].astype(F32)
    branch_a = _dot((on * (r * jax.nn.sigmoid(r))).astype(BF16), woa_ref[...])

    xc = cc_ref[0].astype(F32) * cx_ref[0].astype(F32)
    prev_row = ccp_ref[0, 15:16, :].astype(F32) * cxp_ref[0, 15:16, :].astype(F32)
    next_row = ccn_ref[0, 0:1, :].astype(F32) * cxn_ref[0, 0:1, :].astype(F32)
    prev_row = jnp.where(i > 0, prev_row, 0.0)
    next_row = jnp.where(i < last, next_row, 0.0)
    rowid = lax.broadcasted_iota(jnp.int32, (tm, 1), 0)
    xm1 = jnp.where(rowid == 0, prev_row, pltpu.roll(xc, 1, axis=0))
    xp1 = jnp.where(rowid == tm - 1, next_row, pltpu.roll(xc, tm - 1, axis=0))
    cw = cw_ref[...]
    conv = xm1 * cw[0:1] + xc * cw[1:2] + xp1 * cw[2:3] + cbias_ref[...]
    branch_b = _dot((cb_ref[0].astype(F32) * conv).astype(BF16), wob_ref[...])

    merged = (jax.nn.sigmoid(ga_ref[0].astype(F32)) * branch_a
              + jax.nn.sigmoid(gb_ref[0].astype(F32)) * branch_b)
    mix = _dot(merged.astype(BF16), wo_ref[...])

    m = mod_ref[0]
    x1 = x_ref[0] + m[2:3] * mix
    x1_ref[0] = x1
    ms = jnp.mean(x1 * x1, axis=-1, keepdims=True)
    u2 = (x1 * lax.rsqrt(ms + EPS) * n2_ref[...]) * (1.0 + m[4:5]) + m[3:4]
    u2b = u2.astype(BF16)
    u2_ref[0] = u2b

    logits = _dot_nt(wrt_ref[...], u2b)
    ex = jnp.exp(logits - jnp.max(logits, axis=0, keepdims=True))
    aff = ex / jnp.sum(ex, axis=0, keepdims=True)
    for j in range(tm // SUB):
        aff_ref[j] = aff[:, j * SUB:(j + 1) * SUB]


def _mix_out(x, o, z, mod3, boff, gnw, conv_w, conv_b, woa, wob, wo, n2, wrt):
    b, l, d = x.shape
    tm = min(512, l)
    nt = l // tm
    hb = tm // 16
    nhalo = l // 16

    def col(c):
        return lambda bi, i: (bi, i, c // d)

    in_specs = [
        pl.BlockSpec((1, tm, d), lambda bi, i: (bi, i, 0)),
        pl.BlockSpec((1, tm, d), lambda bi, i: (bi, i, 0)),
        pl.BlockSpec((1, tm, d), col(Z_R)),
        pl.BlockSpec((1, tm, d), col(Z_CB)),
        pl.BlockSpec((1, tm, d), col(Z_CC)),
        pl.BlockSpec((1, tm, d), col(Z_CX)),
        pl.BlockSpec((1, 16, d), lambda bi, i: (bi, jnp.maximum(i * hb - 1, 0), Z_CC // d)),
        pl.BlockSpec((1, 16, d), lambda bi, i: (bi, jnp.maximum(i * hb - 1, 0), Z_CX // d)),
        pl.BlockSpec((1, 16, d), lambda bi, i: (bi, jnp.minimum((i + 1) * hb, nhalo - 1), Z_CC // d)),
        pl.BlockSpec((1, 16, d), lambda bi, i: (bi, jnp.minimum((i + 1) * hb, nhalo - 1), Z_CX // d)),
        pl.BlockSpec((1, tm, d), col(Z_GA)),
        pl.BlockSpec((1, tm, d), col(Z_GB)),
        pl.BlockSpec((1, 6, d), lambda bi, i: (bi + boff, 0, 0)),
        pl.BlockSpec((1, d), lambda bi, i: (0, 0)),
        pl.BlockSpec((3, d), lambda bi, i: (0, 0)),
        pl.BlockSpec((1, d), lambda bi, i: (0, 0)),
        pl.BlockSpec((d, d), lambda bi, i: (0, 0)),
        pl.BlockSpec((d, d), lambda bi, i: (0, 0)),
        pl.BlockSpec((d, d), lambda bi, i: (0, 0)),
        pl.BlockSpec((1, d), lambda bi, i: (0, 0)),
        pl.BlockSpec((N_EXPERTS, d), lambda bi, i: (0, 0)),
    ]
    out_shape = (jax.ShapeDtypeStruct((b, l, d), F32),
                 jax.ShapeDtypeStruct((b, l, d), BF16),
                 jax.ShapeDtypeStruct((b * l // SUB, N_EXPERTS, SUB), F32))
    out_specs = (pl.BlockSpec((1, tm, d), lambda bi, i: (bi, i, 0)),
                 pl.BlockSpec((1, tm, d), lambda bi, i: (bi, i, 0)),
                 pl.BlockSpec((tm // SUB, N_EXPERTS, SUB), lambda bi, i: (bi * nt + i, 0, 0)))
    return pl.pallas_call(
        functools.partial(_mix_kernel, tm),
        out_shape=out_shape,
        grid=(b, nt),
        in_specs=in_specs,
        out_specs=out_specs,
        compiler_params=_params(("arbitrary", "arbitrary")),
        name="mix_out",
    )(x, o, z, z, z, z, z, z, z, z, z, z, mod3, gnw, conv_w, conv_b, woa, wob, wo, n2, wrt)


def _select_kernel(cap, nsb, aff_ref, u_ref, pos_ref, base_ref):
    capf = float(cap)

    def bits(j):
        return pltpu.bitcast(aff_ref[j], jnp.int32)

    def count(pred):
        def body(j, acc):
            return acc + jnp.where(pred(bits(j)), 1.0, 0.0)
        acc = lax.fori_loop(0, nsb, body, jnp.zeros((N_EXPERTS, SUB), F32))
        return jnp.sum(acc, axis=1, keepdims=True)

    def bit_body(t, thr):
        cand = thr | jnp.left_shift(jnp.int32(1), 30 - t)
        return jnp.where(count(lambda v: v >= cand) >= capf, cand, thr)

    thr = lax.fori_loop(0, 31, bit_body, jnp.zeros((N_EXPERTS, SUB), jnp.int32))
    need = capf - count(lambda v: v > thr)

    def blk(j, carry):
        ceq, csel = carry
        v = bits(j)
        gtf = jnp.where(v > thr, 1.0, 0.0)
        eqf = jnp.where(v == thr, 1.0, 0.0)
        r1 = _dot(eqf.astype(BF16), u_ref[...])
        eq_rank = r1[:, :SUB] - eqf + ceq
        sel = gtf + jnp.where(eq_rank < need, eqf, 0.0)
        r2 = _dot(sel.astype(BF16), u_ref[...])
        pos_ref[j] = jnp.where(sel > 0.5, r2[:, :SUB] + csel - 1.0, -1.0).astype(jnp.int32)
        base_ref[j] = csel.astype(jnp.int32)
        return ceq + r1[:, SUB:], csel + r2[:, SUB:]

    zero = jnp.zeros((N_EXPERTS, SUB), F32)
    lax.fori_loop(0, nsb, blk, (zero, zero))


def _select(aff3, cap):
    nsb = aff3.shape[0]
    tri = np.triu(np.ones((SUB, SUB), np.float32))
    umat = jnp.asarray(np.concatenate([tri, np.ones((SUB, SUB), np.float32)], axis=1), BF16)
    shp = jax.ShapeDtypeStruct((nsb, N_EXPERTS, SUB), jnp.int32)
    return pl.pallas_call(
        functools.partial(_select_kernel, cap, nsb),
        out_shape=(shp, shp),
        compiler_params=pltpu.CompilerParams(vmem_limit_bytes=VMEM_LIMIT),
        name="select",
    )(aff3, umat)


def _compact_kernel(nsb, start_ref, pos_ref, x_ref, xe_ref):
    e = pl.program_id(0)

    @pl.when(pl.program_id(1) == 0)
    def _():
        xe_ref[...] = jnp.zeros_like(xe_ref)

    rowi = lax.broadcasted_iota(jnp.int32, (WIN, SUB), 0)

    def body(s, carry):
        st = pl.multiple_of(start_ref[0, 0, s], 16)
        local = pos_ref[s, pl.ds(e, 1), :] - st
        onehot = jnp.where(rowi == local, 1.0, 0.0).astype(BF16)
        xs = x_ref[pl.ds(pl.multiple_of(s * SUB, SUB), SUB), :]
        rows = _dot(onehot, xs).astype(BF16)
        xe_ref[0, pl.ds(st, WIN), :] = xe_ref[0, pl.ds(st, WIN), :] + rows
        return carry

    lax.fori_loop(0, nsb, body, 0)


def _compact(u2, pos3, start, cap, tbig):
    t, d = u2.shape
    nbig = t // tbig
    nsb = tbig // SUB
    return pl.pallas_call(
        functools.partial(_compact_kernel, nsb),
        out_shape=jax.ShapeDtypeStruct((N_EXPERTS, cap, d), BF16),
        grid=(N_EXPERTS, nbig),
        in_specs=[pl.BlockSpec((1, 1, nsb), lambda e, b: (e * nbig + b, 0, 0), memory_space=pltpu.SMEM),
                  pl.BlockSpec((nsb, N_EXPERTS, SUB), lambda e, b: (b, 0, 0)),
                  pl.BlockSpec((tbig, d), lambda e, b: (b, 0))],
        out_specs=pl.BlockSpec((1, cap, d), lambda e, b: (e, 0, 0)),
        compiler_params=_params(("arbitrary", "arbitrary")),
        name="compact",
    )(start.reshape(N_EXPERTS * nbig, 1, nsb), pos3, u2)


def _ffn_kernel(tm, x_ref, wg_ref, wu_ref, wd_ref, o_ref, acc_ref):
    f = pl.program_id(2)

    @pl.when(f == 0)
    def _():
        acc_ref[...] = jnp.zeros_like(acc_ref)

    half = min(512, tm)
    for r in range(tm // half):
        rows = slice(r * half, (r + 1) * half)
        xs = x_ref[0, rows, :]
        g = _dot(xs, wg_ref[0])
        u = _dot(xs, wu_ref[0])
        h = (g * jax.nn.sigmoid(g)) * u
        acc_ref[rows, :] += _dot(h.astype(BF16), wd_ref[0])

    @pl.when(f == pl.num_programs(2) - 1)
    def _():
        o_ref[0] = acc_ref[...].astype(BF16)


def _ffn(xe, wg, wu, wd):
    e, cap, d = xe.shape
    f = wg.shape[2]
    tm = min(1024, cap)
    tf = 1408
    return pl.pallas_call(
        functools.partial(_ffn_kernel, tm),
        out_shape=jax.ShapeDtypeStruct((e, cap, d), BF16),
        grid=(e, cap // tm, f // tf),
        in_specs=[pl.BlockSpec((1, tm, d), lambda ei, m, fi: (ei, m, 0)),
                  pl.BlockSpec((1, d, tf), lambda ei, m, fi: (ei, 0, fi)),
                  pl.BlockSpec((1, d, tf), lambda ei, m, fi: (ei, 0, fi)),
                  pl.BlockSpec((1, tf, d), lambda ei, m, fi: (ei, fi, 0))],
        out_specs=pl.BlockSpec((1, tm, d), lambda ei, m, fi: (ei, m, 0)),
        scratch_shapes=[pltpu.VMEM((tm, d), F32)],
        compiler_params=_params(("arbitrary", "arbitrary", "arbitrary")),
        name="expert_ffn",
    )(xe, wg, wu, wd)


def _uncompact_kernel(nsb, nbig, abig_ref, rel_ref, pos_ref, aff_ref, ye_ref, x1_ref, mod_ref, nf_ref,
                      out_ref, acc_ref):
    b = pl.program_id(0)
    e = pl.program_id(1)

    @pl.when(e == 0)
    def _():
        acc_ref[...] = jnp.zeros_like(acc_ref)

    rowi = lax.broadcasted_iota(jnp.int32, (WIN, SUB), 0)
    big = abig_ref[e * nbig + b]

    def body(s, carry):
        rel = pl.multiple_of(rel_ref[0, 0, s], 16)
        local = pos_ref[s, pl.ds(e, 1), :] - (big + rel)
        gate = aff_ref[s, pl.ds(e, 1), :]
        weights = jnp.where(rowi == local, gate, 0.0).astype(BF16)
        rows = pl.ds(pl.multiple_of(s * SUB, SUB), SUB)
        acc_ref[rows, :] += _dot_tn(weights, ye_ref[pl.ds(rel, WIN), :])
        return carry

    lax.fori_loop(0, nsb, body, 0)

    @pl.when(e == pl.num_programs(1) - 1)
    def _():
        xo = x1_ref[...] + mod_ref[0][5:6] * acc_ref[...]
        ms = jnp.mean(xo * xo, axis=-1, keepdims=True)
        out_ref[...] = xo * lax.rsqrt(ms + EPS) * nf_ref[...]


def _uncompact(ye, pos3, aff3, abig, rel, x1, mod3, boff, seq_len, nf, tbig):
    t, d = x1.shape
    cap = ye.shape[1]
    nbig = t // tbig
    nsb = tbig // SUB
    wbig = tbig + 32
    grid_spec = pltpu.PrefetchScalarGridSpec(
        num_scalar_prefetch=1,
        grid=(nbig, N_EXPERTS),
        in_specs=[
            pl.BlockSpec((1, 1, nsb), lambda b, e, ab: (e * nbig + b, 0, 0), memory_space=pltpu.SMEM),
            pl.BlockSpec((nsb, N_EXPERTS, SUB), lambda b, e, ab: (b, 0, 0)),
            pl.BlockSpec((nsb, N_EXPERTS, SUB), lambda b, e, ab: (b, 0, 0)),
            pl.BlockSpec((pl.Element(wbig), pl.Element(d)),
                         lambda b, e, ab: (pl.multiple_of(e * cap + ab[e * nbig + b], 16), 0)),
            pl.BlockSpec((tbig, d), lambda b, e, ab: (b, 0)),
            pl.BlockSpec((1, 6, d), lambda b, e, ab: (boff + (b * tbig) // seq_len, 0, 0)),
            pl.BlockSpec((1, d), lambda b, e, ab: (0, 0)),
        ],
        out_specs=pl.BlockSpec((tbig, d), lambda b, e, ab: (b, 0)),
        scratch_shapes=[pltpu.VMEM((tbig, d), F32)],
    )
    return pl.pallas_call(
        functools.partial(_uncompact_kernel, nsb, nbig),
        out_shape=jax.ShapeDtypeStruct((t, d), F32),
        grid_spec=grid_spec,
        compiler_params=_params(("arbitrary", "arbitrary")),
        name="uncompact",
    )(abig, rel.reshape(N_EXPERTS * nbig, 1, nsb), pos3, aff3, ye.reshape(N_EXPERTS * cap, d), x1, mod3, nf)


def _floor16(v):
    return (v // 16) * 16


def _trunk(x, mod3, boff, w):
    b, l, d = x.shape
    t = b * l
    cap = EC_FACTOR * t // N_EXPERTS

    z = _in_proj(x, mod3, boff, w["norm1_w"], w["w_in"])
    o = _gla(z, w["wpre_f"], w["bpre_f"], False, None)
    o = _gla(z, w["wpre_b"], w["bpre_b"], True, o)
    x1, u2, aff3 = _mix_out(x, o, z, mod3, boff, w["gla_norm_w"], w["conv_w"], w["conv_b"],
                            w["w_out_a"], w["w_out_b"], w["w_o"], w["norm2_w"], w["w_router_t"])
    pos3, base3 = _select(aff3, cap)

    base = base3[:, :, 0].T
    start = jnp.minimum(_floor16(base), cap - WIN)
    tbig_c = min(2048, t)
    tbig_u = min(1024, t)
    while tbig_u + 32 > cap:
        tbig_u //= 2
    nsb_u = tbig_u // SUB
    abig = jnp.minimum(_floor16(base[:, ::nsb_u]), cap - (tbig_u + 32))
    rel = start - jnp.repeat(abig, nsb_u, axis=1)

    u2f = u2.reshape(t, d)
    xe = _compact(u2f, pos3, start, cap, tbig_c)
    ye = _ffn(xe, w["w_gate"], w["w_up"], w["w_down"])
    out = _uncompact(ye, pos3, aff3, abig.reshape(-1), rel, x1.reshape(t, d), mod3, boff, l,
                     w["norm_f_w"], tbig_u)
    return out.reshape(b, l, d)


def _prepare_weights(w_in, w_alpha_f, b_alpha_f, w_alpha_b, b_alpha_b, gla_norm_w, conv_w, conv_b,
                     w_out_a, w_out_b, w_o, norm1_w, norm2_w, w_router, w_gate, w_up, w_down, norm_f_w):
    d = D_MODEL
    wq, wk, wv, wr, wlf, wlb, wcb, wcc, wcx, wga, wgb = jnp.split(
        w_in, np.cumsum([512, 512, 1024, 1024, 16, 16, 1024, 1024, 1024, 1024])[:].tolist(), axis=1)
    pad = jnp.zeros((d, Z_WIDTH - Z_LR - 2 * GATE_RANK), w_in.dtype)
    w_in_p = jnp.concatenate([wq, wk, wv, wr, wcb, wcc, wcx, wga, wgb, wlf, wlb, pad], axis=1).astype(BF16)

    def pre(wa, ba, row0):
        wh = wa.reshape(GATE_RANK, GLA_HEADS, HEAD_K).transpose(1, 0, 2)
        full = jnp.zeros((GLA_HEADS, 128, HEAD_K), F32).at[:, row0:row0 + GATE_RANK].set(wh)
        return full.astype(BF16), ba.reshape(GLA_HEADS, 1, HEAD_K)

    wpre_f, bpre_f = pre(w_alpha_f, b_alpha_f, 0)
    wpre_b, bpre_b = pre(w_alpha_b, b_alpha_b, GATE_RANK)
    return dict(
        w_in=w_in_p, wpre_f=wpre_f, bpre_f=bpre_f, wpre_b=wpre_b, bpre_b=bpre_b,
        gla_norm_w=gla_norm_w.reshape(1, d), conv_w=conv_w, conv_b=conv_b.reshape(1, d),
        w_out_a=w_out_a.astype(BF16), w_out_b=w_out_b.astype(BF16), w_o=w_o.astype(BF16),
        norm1_w=norm1_w, norm2_w=norm2_w.reshape(1, d), w_router_t=w_router.T.astype(BF16),
        w_gate=w_gate.astype(BF16), w_up=w_up.astype(BF16), w_down=w_down.astype(BF16),
        norm_f_w=norm_f_w.reshape(1, d))


def kernel(x_prompt, x_sample, c_prompt, c_sample, w_ada, b_ada, norm1_w, w_in, w_alpha_f, b_alpha_f, w_alpha_b, b_alpha_b, gla_norm_w, conv_w, conv_b, w_out_a, w_out_b, w_o, norm2_w, w_router, w_gate, w_up, w_down, norm_f_w):
    d = D_MODEL
    bp, bs = c_prompt.shape[0], c_sample.shape[0]
    nrows = -(-(bp + bs) // 8) * 8
    c_all = jnp.concatenate([c_prompt, c_sample, jnp.zeros((nrows - bp - bs, d), F32)], axis=0)
    mod3 = _adaln(c_all, w_ada[0], b_ada[0]).reshape(nrows, 6, d)
    w = _prepare_weights(w_in[0], w_alpha_f[0], b_alpha_f[0], w_alpha_b[0], b_alpha_b[0], gla_norm_w[0],
                         conv_w[0], conv_b[0], w_out_a[0], w_out_b[0], w_o[0], norm1_w[0], norm2_w[0],
                         w_router[0], w_gate[0], w_up[0], w_down[0], norm_f_w)
    y_prompt = _trunk(x_prompt, mod3, 0, w)
    y_sample = _trunk(x_sample, mod3, bp, w)
    return (y_prompt, y_sample)
```

```python
import functools

import numpy as np
import jax
import jax.numpy as jnp
from jax import lax
from jax.experimental import pallas as pl
from jax.experimental.pallas import tpu as pltpu

F32 = jnp.float32
BF16 = jnp.bfloat16

D_MODEL = 1024
GLA_HEADS = 4
HEAD_K = 128
HEAD_V = 256
GATE_RANK = 16
GATE_TAU = 16.0
CHUNK = 64
N_EXPERTS = 16
EC_FACTOR = 2
D_EXPERT = 2816
EPS = 1e-6
LOG2E = 1.4426950408889634

Z_Q, Z_K, Z_V, Z_R, Z_CB, Z_CC, Z_CX, Z_GA, Z_GB, Z_LR = 0, 512, 1024, 2048, 3072, 4096, 5120, 6144, 7168, 8192
Z_WIDTH = 8448

UNIT = 4 * CHUNK
SUB = 128
WIN = SUB + 16
VMEM_LIMIT = 56 * 1024 * 1024


def _dot(a, b):
    return jnp.dot(a, b, preferred_element_type=F32)


def _dot_nt(a, b):
    return lax.dot_general(a, b, (((1,), (1,)), ((), ())), preferred_element_type=F32)


def _dot_tn(a, b):
    return lax.dot_general(a, b, (((0,), (0,)), ((), ())), preferred_element_type=F32)


def _params(sem, limit=VMEM_LIMIT):
    return pltpu.CompilerParams(dimension_semantics=sem, vmem_limit_bytes=limit)


def _adaln_kernel(c_ref, w_ref, b_ref, o_ref):
    c = c_ref[...]
    s = c * jax.nn.sigmoid(c)
    o_ref[...] = jnp.dot(s, w_ref[...], preferred_element_type=F32,
                         precision=lax.Precision.HIGHEST) + b_ref[...]


def _adaln(c_all, w_ada, b_ada):
    nb, d = c_all.shape
    n = w_ada.shape[1]
    tn = 1536
    return pl.pallas_call(
        _adaln_kernel,
        out_shape=jax.ShapeDtypeStruct((nb, n), F32),
        grid=(n // tn,),
        in_specs=[pl.BlockSpec((nb, d), lambda j: (0, 0)),
                  pl.BlockSpec((d, tn), lambda j: (0, j)),
                  pl.BlockSpec((1, tn), lambda j: (0, j))],
        out_specs=pl.BlockSpec((nb, tn), lambda j: (0, j)),
        compiler_params=_params(("arbitrary",)),
        name="adaln",
    )(c_all, w_ada, b_ada.reshape(1, n))


def _inproj_kernel(x_ref, mod_ref, n1_ref, w_ref, z_ref, u_scr):
    @pl.when(pl.program_id(2) == 0)
    def _():
        x = x_ref[0]
        ms = jnp.mean(x * x, axis=-1, keepdims=True)
        xn = x * lax.rsqrt(ms + EPS) * n1_ref[...]
        m = mod_ref[0]
        u_scr[...] = (xn * (1.0 + m[1:2]) + m[0:1]).astype(BF16)

    z_ref[0] = _dot(u_scr[...], w_ref[...]).astype(BF16)


def _in_proj(x, mod3, boff, norm1_w, w_in_p):
    b, l, d = x.shape
    tm = min(1024, l)
    tn = 2816
    return pl.pallas_call(
        _inproj_kernel,
        out_shape=jax.ShapeDtypeStruct((b, l, Z_WIDTH), BF16),
        grid=(b, l // tm, Z_WIDTH // tn),
        in_specs=[pl.BlockSpec((1, tm, d), lambda bi, i, n: (bi, i, 0)),
                  pl.BlockSpec((1, 6, d), lambda bi, i, n: (bi + boff, 0, 0)),
                  pl.BlockSpec((1, d), lambda bi, i, n: (0, 0)),
                  pl.BlockSpec((d, tn), lambda bi, i, n: (0, n))],
        out_specs=pl.BlockSpec((1, tm, tn), lambda bi, i, n: (bi, i, n)),
        scratch_shapes=[pltpu.VMEM((tm, d), BF16)],
        compiler_params=_params(("arbitrary", "arbitrary", "arbitrary")),
        name="in_proj",
    )(x, mod3, norm1_w.reshape(1, d), w_in_p)


def _gla_constants(reverse):
    r = np.arange(UNIT)[:, None]
    s = np.arange(UNIT)[None, :]
    cr, cs = r // CHUNK, s // CHUNK
    if reverse:
        tri = (cr == cs) & (s >= r)
        tr, ts = 3 - cr, 3 - cs
    else:
        tri = (cr == cs) & (s <= r)
        tr, ts = cr, cs
    code = np.zeros((UNIT, UNIT), np.int32)
    code[tri] = 1
    return jnp.asarray(tri, BF16), jnp.asarray(code)


def _gla_kernel(reverse, nunits, has_prev, *refs):
    if has_prev:
        q_ref, k_ref, v_ref, lr_ref, wp_ref, bp_ref, tri_ref, code_ref, prev_ref, o_ref, s_scr = refs
    else:
        q_ref, k_ref, v_ref, lr_ref, wp_ref, bp_ref, tri_ref, code_ref, o_ref, s_scr = refs
        prev_ref = None

    @pl.when(pl.program_id(2) == 0)
    def _():
        s_scr[...] = jnp.zeros_like(s_scr)

    order = (3, 2, 1, 0) if reverse else (0, 1, 2, 3)

    def to_mem(by_step):
        out = [None] * 4
        for t in range(4):
            out[order[t]] = by_step[t]
        return out

    def scale_chunks(arr, logf):
        parts = []
        for c in range(4):
            part = arr[c * CHUNK:(c + 1) * CHUNK]
            if logf[c] is not None:
                part = part * jnp.exp2(logf[c])
            parts.append(part)
        return jnp.concatenate(parts, axis=0).astype(BF16)

    def rows_of(ui):
        um = (nunits - 1 - ui) if reverse else ui
        return pl.ds(um * UNIT, UNIT)

    def decays(ui):
        pre = _dot(lr_ref[0, rows_of(ui), :], wp_ref[0]) + bp_ref[0]
        soft = jnp.log2(1.0 + jnp.exp2(jnp.abs(pre) * (-LOG2E)))
        la = (jnp.minimum(pre, 0.0) * LOG2E - soft) * (1.0 / GATE_TAU)
        hi = la.astype(BF16)
        lo = (la - hi.astype(F32)).astype(BF16)
        bb = _dot(tri_ref[...], jnp.concatenate([hi, lo], axis=1))
        return bb[:, :HEAD_K] + bb[:, HEAD_K:]

    def operands(ui, b):
        rows = rows_of(ui)
        edge = 0 if reverse else CHUNK - 1
        tot = [b[c * CHUNK + edge:c * CHUNK + edge + 1] for c in range(4)]
        tot_full = jnp.concatenate([jnp.broadcast_to(t, (CHUNK, HEAD_K)) for t in tot], axis=0)
        q = q_ref[0, rows, :].astype(F32)
        k = k_ref[0, rows, :].astype(F32)
        q_t = q * jnp.exp2(b)
        k_t = (k * jnp.exp2(-b)).astype(BF16)
        k_e = k * jnp.exp2(tot_full - b)
        q_tb = q_t.astype(BF16)
        k_eb = k_e.astype(BF16)
        bt = [tot[o] for o in order]

        zero = jnp.zeros((CHUNK, HEAD_K), BF16)
        e1 = jnp.exp2(bt[1])
        e2 = jnp.exp2(bt[2])
        qrows, krows = [None] * 4, [None] * 4
        for t in range(4):
            c = order[t]
            sl = slice(c * CHUNK, (c + 1) * CHUNK)
            qc, kc = q_tb[sl], k_eb[sl]
            q3 = qc if t == 2 else ((q_t[sl] * e2).astype(BF16) if t == 3 else zero)
            k3 = (k_e[sl] * e1).astype(BF16) if t == 0 else (kc if t == 1 else zero)
            qrows[c] = jnp.concatenate([qc if t == 1 else zero, qc if t == 3 else zero, q3], axis=1)
            krows[c] = jnp.concatenate([kc if t == 0 else zero, kc if t == 2 else zero, k3], axis=1)

        gq = to_mem([None, bt[0], bt[0] + bt[1], bt[0] + bt[1] + bt[2]])
        gk = to_mem([bt[1] + bt[2] + bt[3], bt[2] + bt[3], bt[3], None])
        dec = jnp.exp2(bt[0] + bt[1] + bt[2] + bt[3])
        return dict(q_t=q_tb, k_t=k_t, q_cat=jnp.concatenate(qrows, axis=0), k_cat=jnp.concatenate(krows, axis=0),
                    qs=scale_chunks(q_t, gq), ks=scale_chunks(k_e, gk), dec=dec)

    def scores(p):
        same = _dot_nt(p["q_t"], p["k_t"])
        cross = _dot_nt(p["q_cat"], p["k_cat"])
        return jnp.where(code_ref[...] == 1, same, cross).astype(BF16)

    units = range(nunits)
    bs = [decays(ui) for ui in units]
    ps = [operands(ui, bs[ui]) for ui in units]
    atts = [scores(ps[ui]) for ui in units]
    intra = [_dot(atts[ui], v_ref[0, rows_of(ui), :]) for ui in units]
    for ui in units:
        rows = rows_of(ui)
        p = ps[ui]
        st = s_scr[...]
        o = intra[ui] + _dot_nt(p["qs"], st.astype(BF16))
        s_scr[...] = st * p["dec"] + _dot_tn(v_ref[0, rows, :], p["ks"])
        if prev_ref is not None:
            o = o + prev_ref[0, rows, :]
        o_ref[0, rows, :] = o


def _gla(z, wpre, bpre, reverse, prev):
    b, l, _ = z.shape
    lb = min(1024, l)
    nb = l // lb
    nunits = lb // UNIT
    tri, code = _gla_constants(reverse)

    def seq(i):
        return (nb - 1 - i) if reverse else i

    in_specs = [
        pl.BlockSpec((1, lb, HEAD_K), lambda bi, h, i: (bi, seq(i), Z_Q // HEAD_K + h)),
        pl.BlockSpec((1, lb, HEAD_K), lambda bi, h, i: (bi, seq(i), Z_K // HEAD_K + h)),
        pl.BlockSpec((1, lb, HEAD_V), lambda bi, h, i: (bi, seq(i), Z_V // HEAD_V + h)),
        pl.BlockSpec((1, lb, 128), lambda bi, h, i: (bi, seq(i), Z_LR // 128)),
        pl.BlockSpec((1, 128, HEAD_K), lambda bi, h, i: (h, 0, 0)),
        pl.BlockSpec((1, 1, HEAD_K), lambda bi, h, i: (h, 0, 0)),
        pl.BlockSpec((UNIT, UNIT), lambda bi, h, i: (0, 0)),
        pl.BlockSpec((UNIT, UNIT), lambda bi, h, i: (0, 0)),
    ]
    args = [z, z, z, z, wpre, bpre, tri, code]
    if prev is not None:
        in_specs.append(pl.BlockSpec((1, lb, HEAD_V), lambda bi, h, i: (bi, seq(i), h)))
        args.append(prev)
    return pl.pallas_call(
        functools.partial(_gla_kernel, reverse, nunits, prev is not None),
        out_shape=jax.ShapeDtypeStruct((b, l, GLA_HEADS * HEAD_V), F32),
        grid=(b, GLA_HEADS, nb),
        in_specs=in_specs,
        out_specs=pl.BlockSpec((1, lb, HEAD_V), lambda bi, h, i: (bi, seq(i), h)),
        scratch_shapes=[pltpu.VMEM((HEAD_V, HEAD_K), F32)],
        compiler_params=_params(("arbitrary", "arbitrary", "arbitrary")),
        name="gla_bwd" if reverse else "gla_fwd",
    )(*args)


def _mix_kernel(tm, x_ref, o_ref, r_ref, cb_ref, cc_ref, cx_ref, ccp_ref, cxp_ref, ccn_ref, cxn_ref,
                ga_ref, gb_ref, mod_ref, gnw_ref, cw_ref, cbias_ref, woa_ref, wob_ref, wo_ref,
                n2_ref, wrt_ref, x1_ref, u2_ref, aff_ref):
    i = pl.program_id(1)
    last = pl.num_programs(1) - 1

    o = o_ref[0]
    parts = []
    for h in range(GLA_HEADS):
        oh = o[:, h * HEAD_V:(h + 1) * HEAD_V]
        ms = jnp.mean(oh * oh, axis=-1, keepdims=True)
        parts.append(oh * lax.rsqrt(ms + EPS))
    on = jnp.concatenate(parts, axis=1) * gnw_ref[...]
    r = r_ref[0].astype(F32)
    branch_a = _dot((on * (r * jax.nn.sigmoid(r))).astype(BF16), woa_ref[...])

    xc = cc_ref[0].astype(F32) * cx_ref[0].astype(F32)
    prev_row = ccp_ref[0, 15:16, :].astype(F32) * cxp_ref[0, 15:16, :].astype(F32)
    next_row = ccn_ref[0, 0:1, :].astype(F32) * cxn_ref[0, 0:1, :].astype(F32)
    prev_row = jnp.where(i > 0, prev_row, 0.0)
    next_row = jnp.where(i < last, next_row, 0.0)
    rowid = lax.broadcasted_iota(jnp.int32, (tm, 1), 0)
    xm1 = jnp.where(rowid == 0, prev_row, pltpu.roll(xc, 1, axis=0))
    xp1 = jnp.where(rowid == tm - 1, next_row, pltpu.roll(xc, tm - 1, axis=0))
    cw = cw_ref[...]
    conv = xm1 * cw[0:1] + xc * cw[1:2] + xp1 * cw[2:3] + cbias_ref[...]
    branch_b = _dot((cb_ref[0].astype(F32) * conv).astype(BF16), wob_ref[...])

    merged = (jax.nn.sigmoid(ga_ref[0].astype(F32)) * branch_a
              + jax.nn.sigmoid(gb_ref[0].astype(F32)) * branch_b)
    mix = _dot(merged.astype(BF16), wo_ref[...])

    m = mod_ref[0]
    x1 = x_ref[0] + m[2:3] * mix
    x1_ref[0] = x1
    ms = jnp.mean(x1 * x1, axis=-1, keepdims=True)
    u2 = (x1 * lax.rsqrt(ms + EPS) * n2_ref[...]) * (1.0 + m[4:5]) + m[3:4]
    u2b = u2.astype(BF16)
    u2_ref[0] = u2b

    logits = _dot_nt(wrt_ref[...], u2b)
    ex = jnp.exp(logits - jnp.max(logits, axis=0, keepdims=True))
    aff = ex / jnp.sum(ex, axis=0, keepdims=True)
    for j in range(tm // SUB):
        aff_ref[j] = aff[:, j * SUB:(j + 1) * SUB]


def _mix_out(x, o, z, mod3, boff, gnw, conv_w, conv_b, woa, wob, wo, n2, wrt):
    b, l, d = x.shape
    tm = min(512, l)
    nt = l // tm
    hb = tm // 16
    nhalo = l // 16

    def col(c):
        return lambda bi, i: (bi, i, c // d)

    in_specs = [
        pl.BlockSpec((1, tm, d), lambda bi, i: (bi, i, 0)),
        pl.BlockSpec((1, tm, d), lambda bi, i: (bi, i, 0)),
        pl.BlockSpec((1, tm, d), col(Z_R)),
        pl.BlockSpec((1, tm, d), col(Z_CB)),
        pl.BlockSpec((1, tm, d), col(Z_CC)),
        pl.BlockSpec((1, tm, d), col(Z_CX)),
        pl.BlockSpec((1, 16, d), lambda bi, i: (bi, jnp.maximum(i * hb - 1, 0), Z_CC // d)),
        pl.BlockSpec((1, 16, d), lambda bi, i: (bi, jnp.maximum(i * hb - 1, 0), Z_CX // d)),
        pl.BlockSpec((1, 16, d), lambda bi, i: (bi, jnp.minimum((i + 1) * hb, nhalo - 1), Z_CC // d)),
        pl.BlockSpec((1, 16, d), lambda bi, i: (bi, jnp.minimum((i + 1) * hb, nhalo - 1), Z_CX // d)),
        pl.BlockSpec((1, tm, d), col(Z_GA)),
        pl.BlockSpec((1, tm, d), col(Z_GB)),
        pl.BlockSpec((1, 6, d), lambda bi, i: (bi + boff, 0, 0)),
        pl.BlockSpec((1, d), lambda bi, i: (0, 0)),
        pl.BlockSpec((3, d), lambda bi, i: (0, 0)),
        pl.BlockSpec((1, d), lambda bi, i: (0, 0)),
        pl.BlockSpec((d, d), lambda bi, i: (0, 0)),
        pl.BlockSpec((d, d), lambda bi, i: (0, 0)),
        pl.BlockSpec((d, d), lambda bi, i: (0, 0)),
        pl.BlockSpec((1, d), lambda bi, i: (0, 0)),
        pl.BlockSpec((N_EXPERTS, d), lambda bi, i: (0, 0)),
    ]
    out_shape = (jax.ShapeDtypeStruct((b, l, d), F32),
                 jax.ShapeDtypeStruct((b, l, d), BF16),
                 jax.ShapeDtypeStruct((b * l // SUB, N_EXPERTS, SUB), F32))
    out_specs = (pl.BlockSpec((1, tm, d), lambda bi, i: (bi, i, 0)),
                 pl.BlockSpec((1, tm, d), lambda bi, i: (bi, i, 0)),
                 pl.BlockSpec((tm // SUB, N_EXPERTS, SUB), lambda bi, i: (bi * nt + i, 0, 0)))
    return pl.pallas_call(
        functools.partial(_mix_kernel, tm),
        out_shape=out_shape,
        grid=(b, nt),
        in_specs=in_specs,
        out_specs=out_specs,
        compiler_params=_params(("arbitrary", "arbitrary")),
        name="mix_out",
    )(x, o, z, z, z, z, z, z, z, z, z, z, mod3, gnw, conv_w, conv_b, woa, wob, wo, n2, wrt)


def _select_kernel(cap, nsb, aff_ref, u_ref, pos_ref, base_ref):
    capf = float(cap)

    def bits(j):
        return pltpu.bitcast(aff_ref[j], jnp.int32)

    def count(pred):
        def body(j, acc):
            return acc + jnp.where(pred(bits(j)), 1.0, 0.0)
        acc = lax.fori_loop(0, nsb, body, jnp.zeros((N_EXPERTS, SUB), F32))
        return jnp.sum(acc, axis=1, keepdims=True)

    def bit_body(t, thr):
        cand = thr | jnp.left_shift(jnp.int32(1), 30 - t)
        return jnp.where(count(lambda v: v >= cand) >= capf, cand, thr)

    thr = lax.fori_loop(0, 31, bit_body, jnp.zeros((N_EXPERTS, SUB), jnp.int32))
    need = capf - count(lambda v: v > thr)

    def blk(j, carry):
        ceq, csel = carry
        v = bits(j)
        gtf = jnp.where(v > thr, 1.0, 0.0)
        eqf = jnp.where(v == thr, 1.0, 0.0)
        r1 = _dot(eqf.astype(BF16), u_ref[...])
        eq_rank = r1[:, :SUB] - eqf + ceq
        sel = gtf + jnp.where(eq_rank < need, eqf, 0.0)
        r2 = _dot(sel.astype(BF16), u_ref[...])
        pos_ref[j] = jnp.where(sel > 0.5, r2[:, :SUB] + csel - 1.0, -1.0).astype(jnp.int32)
        base_ref[j] = csel.astype(jnp.int32)
        return ceq + r1[:, SUB:], csel + r2[:, SUB:]

    zero = jnp.zeros((N_EXPERTS, SUB), F32)
    lax.fori_loop(0, nsb, blk, (zero, zero))


def _select(aff3, cap):
    nsb = aff3.shape[0]
    tri = np.triu(np.ones((SUB, SUB), np.float32))
    umat = jnp.asarray(np.concatenate([tri, np.ones((SUB, SUB), np.float32)], axis=1), BF16)
    shp = jax.ShapeDtypeStruct((nsb, N_EXPERTS, SUB), jnp.int32)
    return pl.pallas_call(
        functools.partial(_select_kernel, cap, nsb),
        out_shape=(shp, shp),
        compiler_params=pltpu.CompilerParams(vmem_limit_bytes=VMEM_LIMIT),
        name="select",
    )(aff3, umat)


def _compact_kernel(nsb, start_ref, pos_ref, x_ref, xe_ref):
    e = pl.program_id(0)

    @pl.when(pl.program_id(1) == 0)
    def _():
        xe_ref[...] = jnp.zeros_like(xe_ref)

    rowi = lax.broadcasted_iota(jnp.int32, (WIN, SUB), 0)

    def body(s, carry):
        st = pl.multiple_of(start_ref[0, 0, s], 16)
        local = pos_ref[s, pl.ds(e, 1), :] - st
        onehot = jnp.where(rowi == local, 1.0, 0.0).astype(BF16)
        xs = x_ref[pl.ds(s * SUB, SUB), :]
        rows = _dot(onehot, xs).astype(BF16)
        xe_ref[0, pl.ds(st, WIN), :] = xe_ref[0, pl.ds(st, WIN), :] + rows
        return carry

    for s in range(nsb):
        body(s, 0)


def _compact(u2, pos3, start, cap, tbig):
    t, d = u2.shape
    nbig = t // tbig
    nsb = tbig // SUB
    return pl.pallas_call(
        functools.partial(_compact_kernel, nsb),
        out_shape=jax.ShapeDtypeStruct((N_EXPERTS, cap, d), BF16),
        grid=(N_EXPERTS, nbig),
        in_specs=[pl.BlockSpec((1, 1, nsb), lambda e, b: (e * nbig + b, 0, 0), memory_space=pltpu.SMEM),
                  pl.BlockSpec((nsb, N_EXPERTS, SUB), lambda e, b: (b, 0, 0)),
                  pl.BlockSpec((tbig, d), lambda e, b: (b, 0))],
        out_specs=pl.BlockSpec((1, cap, d), lambda e, b: (e, 0, 0)),
        compiler_params=_params(("arbitrary", "arbitrary")),
        name="compact",
    )(start.reshape(N_EXPERTS * nbig, 1, nsb), pos3, u2)


def _ffn_kernel(tm, x_ref, wg_ref, wu_ref, wd_ref, o_ref, acc_ref):
    f = pl.program_id(2)

    @pl.when(f == 0)
    def _():
        acc_ref[...] = jnp.zeros_like(acc_ref)

    half = min(512, tm)
    for r in range(tm // half):
        rows = slice(r * half, (r + 1) * half)
        xs = x_ref[0, rows, :]
        g = _dot(xs, wg_ref[0])
        u = _dot(xs, wu_ref[0])
        h = (g * jax.nn.sigmoid(g)) * u
        acc_ref[rows, :] += _dot(h.astype(BF16), wd_ref[0])

    @pl.when(f == pl.num_programs(2) - 1)
    def _():
        o_ref[0] = acc_ref[...].astype(BF16)


def _ffn(xe, wg, wu, wd):
    e, cap, d = xe.shape
    f = wg.shape[2]
    tm = min(1024, cap)
    tf = 1408
    return pl.pallas_call(
        functools.partial(_ffn_kernel, tm),
        out_shape=jax.ShapeDtypeStruct((e, cap, d), BF16),
        grid=(e, cap // tm, f // tf),
        in_specs=[pl.BlockSpec((1, tm, d), lambda ei, m, fi: (ei, m, 0)),
                  pl.BlockSpec((1, d, tf), lambda ei, m, fi: (ei, 0, fi)),
                  pl.BlockSpec((1, d, tf), lambda ei, m, fi: (ei, 0, fi)),
                  pl.BlockSpec((1, tf, d), lambda ei, m, fi: (ei, fi, 0))],
        out_specs=pl.BlockSpec((1, tm, d), lambda ei, m, fi: (ei, m, 0)),
        scratch_shapes=[pltpu.VMEM((tm, d), F32)],
        compiler_params=_params(("arbitrary", "arbitrary", "arbitrary")),
        name="expert_ffn",
    )(xe, wg, wu, wd)


def _uncompact_kernel(nsb, nbig, abig_ref, rel_ref, pos_ref, aff_ref, ye_ref, x1_ref, mod_ref, nf_ref,
                      out_ref, acc_ref):
    b = pl.program_id(0)
    e = pl.program_id(1)

    @pl.when(e == 0)
    def _():
        acc_ref[...] = jnp.zeros_like(acc_ref)

    rowi = lax.broadcasted_iota(jnp.int32, (WIN, SUB), 0)
    big = abig_ref[e * nbig + b]

    def body(s, carry):
        rel = pl.multiple_of(rel_ref[0, 0, s], 16)
        local = pos_ref[s, pl.ds(e, 1), :] - (big + rel)
        gate = aff_ref[s, pl.ds(e, 1), :]
        weights = jnp.where(rowi == local, gate, 0.0).astype(BF16)
        rows = pl.ds(s * SUB, SUB)
        acc_ref[rows, :] += _dot_tn(weights, ye_ref[pl.ds(rel, WIN), :])
        return carry

    for s in range(nsb):
        body(s, 0)

    @pl.when(e == pl.num_programs(1) - 1)
    def _():
        xo = x1_ref[...] + mod_ref[0][5:6] * acc_ref[...]
        ms = jnp.mean(xo * xo, axis=-1, keepdims=True)
        out_ref[...] = xo * lax.rsqrt(ms + EPS) * nf_ref[...]


def _uncompact(ye, pos3, aff3, abig, rel, x1, mod3, boff, seq_len, nf, tbig):
    t, d = x1.shape
    cap = ye.shape[1]
    nbig = t // tbig
    nsb = tbig // SUB
    wbig = tbig + 32
    grid_spec = pltpu.PrefetchScalarGridSpec(
        num_scalar_prefetch=1,
        grid=(nbig, N_EXPERTS),
        in_specs=[
            pl.BlockSpec((1, 1, nsb), lambda b, e, ab: (e * nbig + b, 0, 0), memory_space=pltpu.SMEM),
            pl.BlockSpec((nsb, N_EXPERTS, SUB), lambda b, e, ab: (b, 0, 0)),
            pl.BlockSpec((nsb, N_EXPERTS, SUB), lambda b, e, ab: (b, 0, 0)),
            pl.BlockSpec((pl.Element(wbig), pl.Element(d)),
                         lambda b, e, ab: (pl.multiple_of(e * cap + ab[e * nbig + b], 16), 0)),
            pl.BlockSpec((tbig, d), lambda b, e, ab: (b, 0)),
            pl.BlockSpec((1, 6, d), lambda b, e, ab: (boff + (b * tbig) // seq_len, 0, 0)),
            pl.BlockSpec((1, d), lambda b, e, ab: (0, 0)),
        ],
        out_specs=pl.BlockSpec((tbig, d), lambda b, e, ab: (b, 0)),
        scratch_shapes=[pltpu.VMEM((tbig, d), F32)],
    )
    return pl.pallas_call(
        functools.partial(_uncompact_kernel, nsb, nbig),
        out_shape=jax.ShapeDtypeStruct((t, d), F32),
        grid_spec=grid_spec,
        compiler_params=_params(("arbitrary", "arbitrary")),
        name="uncompact",
    )(abig, rel.reshape(N_EXPERTS * nbig, 1, nsb), pos3, aff3, ye.reshape(N_EXPERTS * cap, d), x1, mod3, nf)


def _floor16(v):
    return (v // 16) * 16


def _trunk(x, mod3, boff, w):
    b, l, d = x.shape
    t = b * l
    cap = EC_FACTOR * t // N_EXPERTS

    z = _in_proj(x, mod3, boff, w["norm1_w"], w["w_in"])
    o = _gla(z, w["wpre_f"], w["bpre_f"], False, None)
    o = _gla(z, w["wpre_b"], w["bpre_b"], True, o)
    x1, u2, aff3 = _mix_out(x, o, z, mod3, boff, w["gla_norm_w"], w["conv_w"], w["conv_b"],
                            w["w_out_a"], w["w_out_b"], w["w_o"], w["norm2_w"], w["w_router_t"])
    pos3, base3 = _select(aff3, cap)

    base = base3[:, :, 0].T
    start = jnp.minimum(_floor16(base), cap - WIN)
    tbig_c = min(2048, t)
    tbig_u = min(1024, t)
    while tbig_u + 32 > cap:
        tbig_u //= 2
    nsb_u = tbig_u // SUB
    abig = jnp.minimum(_floor16(base[:, ::nsb_u]), cap - (tbig_u + 32))
    rel = start - jnp.repeat(abig, nsb_u, axis=1)

    u2f = u2.reshape(t, d)
    xe = _compact(u2f, pos3, start, cap, tbig_c)
    ye = _ffn(xe, w["w_gate"], w["w_up"], w["w_down"])
    out = _uncompact(ye, pos3, aff3, abig.reshape(-1), rel, x1.reshape(t, d), mod3, boff, l,
                     w["norm_f_w"], tbig_u)
    return out.reshape(b, l, d)


def _prepare_weights(w_in, w_alpha_f, b_alpha_f, w_alpha_b, b_alpha_b, gla_norm_w, conv_w, conv_b,
                     w_out_a, w_out_b, w_o, norm1_w, norm2_w, w_router, w_gate, w_up, w_down, norm_f_w):
    d = D_MODEL
    wq, wk, wv, wr, wlf, wlb, wcb, wcc, wcx, wga, wgb = jnp.split(
        w_in, np.cumsum([512, 512, 1024, 1024, 16, 16, 1024, 1024, 1024, 1024])[:].tolist(), axis=1)
    pad = jnp.zeros((d, Z_WIDTH - Z_LR - 2 * GATE_RANK), w_in.dtype)
    wq = wq * (HEAD_K ** -0.5)
    w_in_p = jnp.concatenate([wq, wk, wv, wr, wcb, wcc, wcx, wga, wgb, wlf, wlb, pad], axis=1).astype(BF16)

    def pre(wa, ba, row0):
        wh = wa.reshape(GATE_RANK, GLA_HEADS, HEAD_K).transpose(1, 0, 2)
        full = jnp.zeros((GLA_HEADS, 128, HEAD_K), F32).at[:, row0:row0 + GATE_RANK].set(wh)
        return full.astype(BF16), ba.reshape(GLA_HEADS, 1, HEAD_K)

    wpre_f, bpre_f = pre(w_alpha_f, b_alpha_f, 0)
    wpre_b, bpre_b = pre(w_alpha_b, b_alpha_b, GATE_RANK)
    return dict(
        w_in=w_in_p, wpre_f=wpre_f, bpre_f=bpre_f, wpre_b=wpre_b, bpre_b=bpre_b,
        gla_norm_w=gla_norm_w.reshape(1, d), conv_w=conv_w, conv_b=conv_b.reshape(1, d),
        w_out_a=w_out_a.astype(BF16), w_out_b=w_out_b.astype(BF16), w_o=w_o.astype(BF16),
        norm1_w=norm1_w, norm2_w=norm2_w.reshape(1, d), w_router_t=w_router.T.astype(BF16),
        w_gate=w_gate.astype(BF16), w_up=w_up.astype(BF16), w_down=w_down.astype(BF16),
        norm_f_w=norm_f_w.reshape(1, d))


def kernel(x_prompt, x_sample, c_prompt, c_sample, w_ada, b_ada, norm1_w, w_in, w_alpha_f, b_alpha_f, w_alpha_b, b_alpha_b, gla_norm_w, conv_w, conv_b, w_out_a, w_out_b, w_o, norm2_w, w_router, w_gate, w_up, w_down, norm_f_w):
    d = D_MODEL
    bp, bs = c_prompt.shape[0], c_sample.shape[0]
    nrows = -(-(bp + bs) // 8) * 8
    c_all = jnp.concatenate([c_prompt, c_sample, jnp.zeros((nrows - bp - bs, d), F32)], axis=0)
    mod3 = _adaln(c_all, w_ada[0], b_ada[0]).reshape(nrows, 6, d)
    w = _prepare_weights(w_in[0], w_alpha_f[0], b_alpha_f[0], w_alpha_b[0], b_alpha_b[0], gla_norm_w[0],
                         conv_w[0], conv_b[0], w_out_a[0], w_out_b[0], w_o[0], norm1_w[0], norm2_w[0],
                         w_router[0], w_gate[0], w_up[0], w_down[0], norm_f_w)
    y_prompt = _trunk(x_prompt, mod3, 0, w)
    y_sample = _trunk(x_sample, mod3, bp, w)
    return (y_prompt, y_sample)
```

```python
import functools

import numpy as np
import jax
import jax.numpy as jnp
from jax import lax
from jax.experimental import pallas as pl
from jax.experimental.pallas import tpu as pltpu

F32 = jnp.float32
BF16 = jnp.bfloat16

D_MODEL = 1024
GLA_HEADS = 4
HEAD_K = 128
HEAD_V = 256
GATE_RANK = 16
GATE_TAU = 16.0
CHUNK = 64
N_EXPERTS = 16
EC_FACTOR = 2
D_EXPERT = 2816
EPS = 1e-6
LOG2E = 1.4426950408889634

Z_Q, Z_K, Z_V, Z_R, Z_CB, Z_CC, Z_CX, Z_GA, Z_GB, Z_LR = 0, 512, 1024, 2048, 3072, 4096, 5120, 6144, 7168, 8192
Z_WIDTH = 8448

UNIT = 4 * CHUNK
SUB = 128
WIN = SUB + 16
SMALL_WIN = 64
PAIR_WIN = 112
VMEM_LIMIT = 56 * 1024 * 1024


def _dot(a, b):
    return jnp.dot(a, b, preferred_element_type=F32)


def _dot_nt(a, b):
    return lax.dot_general(a, b, (((1,), (1,)), ((), ())), preferred_element_type=F32)


def _sigmoid(x):
    return 0.5 * jnp.tanh(0.5 * x) + 0.5


def _dot_tn(a, b):
    return lax.dot_general(a, b, (((0,), (0,)), ((), ())), preferred_element_type=F32)


def _params(sem, limit=VMEM_LIMIT):
    return pltpu.CompilerParams(dimension_semantics=sem, vmem_limit_bytes=limit)


def _adaln_kernel(c_ref, w_ref, b_ref, o_ref):
    c = c_ref[...]
    s = c * jax.nn.sigmoid(c)
    o_ref[...] = jnp.dot(s, w_ref[...], preferred_element_type=F32,
                         precision=lax.Precision.HIGHEST) + b_ref[...]


def _adaln(c_all, w_ada, b_ada):
    nb, d = c_all.shape
    n = w_ada.shape[1]
    tn = 1536
    return pl.pallas_call(
        _adaln_kernel,
        out_shape=jax.ShapeDtypeStruct((nb, n), F32),
        grid=(n // tn,),
        in_specs=[pl.BlockSpec((nb, d), lambda j: (0, 0)),
                  pl.BlockSpec((d, tn), lambda j: (0, j)),
                  pl.BlockSpec((1, tn), lambda j: (0, j))],
        out_specs=pl.BlockSpec((nb, tn), lambda j: (0, j)),
        compiler_params=_params(("arbitrary",)),
        name="adaln",
    )(c_all, w_ada, b_ada.reshape(1, n))


def _inproj_kernel(x_ref, mod_ref, n1_ref, w_ref, z_ref, u_scr):
    @pl.when(pl.program_id(2) == 0)
    def _():
        x = x_ref[0]
        ms = jnp.mean(x * x, axis=-1, keepdims=True)
        xn = x * lax.rsqrt(ms + EPS) * n1_ref[...]
        m = mod_ref[0]
        u_scr[...] = (xn * (1.0 + m[1:2]) + m[0:1]).astype(BF16)

    z_ref[0] = _dot(u_scr[...], w_ref[...]).astype(BF16)


def _in_proj(x, mod3, boff, norm1_w, w_in_p):
    b, l, d = x.shape
    tm = min(1024, l)
    tn = 2816
    return pl.pallas_call(
        _inproj_kernel,
        out_shape=jax.ShapeDtypeStruct((b, l, Z_WIDTH), BF16),
        grid=(b, l // tm, Z_WIDTH // tn),
        in_specs=[pl.BlockSpec((1, tm, d), lambda bi, i, n: (bi, i, 0)),
                  pl.BlockSpec((1, 6, d), lambda bi, i, n: (bi + boff, 0, 0)),
                  pl.BlockSpec((1, d), lambda bi, i, n: (0, 0)),
                  pl.BlockSpec((d, tn), lambda bi, i, n: (0, n))],
        out_specs=pl.BlockSpec((1, tm, tn), lambda bi, i, n: (bi, i, n)),
        scratch_shapes=[pltpu.VMEM((tm, d), BF16)],
        compiler_params=_params(("arbitrary", "arbitrary", "arbitrary")),
        name="in_proj",
    )(x, mod3, norm1_w.reshape(1, d), w_in_p)


def _gla_constants(reverse):
    r = np.arange(UNIT)[:, None]
    s = np.arange(UNIT)[None, :]
    cr, cs = r // CHUNK, s // CHUNK
    if reverse:
        tri = (cr == cs) & (s >= r)
        tr, ts = 3 - cr, 3 - cs
    else:
        tri = (cr == cs) & (s <= r)
        tr, ts = cr, cs
    code = np.zeros((UNIT, UNIT), np.int32)
    code[tri] = 1
    return jnp.asarray(tri, BF16), jnp.asarray(code)


def _gla_kernel(reverse, nunits, has_prev, *refs):
    if has_prev:
        q_ref, k_ref, v_ref, lr_ref, wp_ref, bp_ref, tri_ref, code_ref, prev_ref, o_ref, s_scr = refs
    else:
        q_ref, k_ref, v_ref, lr_ref, wp_ref, bp_ref, tri_ref, code_ref, o_ref, s_scr = refs
        prev_ref = None

    @pl.when(pl.program_id(2) == 0)
    def _():
        s_scr[...] = jnp.zeros_like(s_scr)

    order = (3, 2, 1, 0) if reverse else (0, 1, 2, 3)

    def to_mem(by_step):
        out = [None] * 4
        for t in range(4):
            out[order[t]] = by_step[t]
        return out

    def scale_chunks(arr, logf):
        parts = []
        for c in range(4):
            part = arr[c * CHUNK:(c + 1) * CHUNK]
            if logf[c] is not None:
                part = part * jnp.exp2(logf[c])
            parts.append(part)
        return jnp.concatenate(parts, axis=0).astype(BF16)

    def rows_of(ui):
        um = (nunits - 1 - ui) if reverse else ui
        return pl.ds(um * UNIT, UNIT)

    def decays(ui):
        pre = _dot(lr_ref[0, rows_of(ui), :], wp_ref[0]) + bp_ref[0]
        soft = jnp.log2(1.0 + jnp.exp2(jnp.abs(pre) * (-LOG2E)))
        la = (jnp.minimum(pre, 0.0) * LOG2E - soft) * (1.0 / GATE_TAU)
        hi = la.astype(BF16)
        lo = (la - hi.astype(F32)).astype(BF16)
        bb = _dot(tri_ref[...], jnp.concatenate([hi, lo], axis=1))
        return bb[:, :HEAD_K] + bb[:, HEAD_K:]

    def operands(ui, b):
        rows = rows_of(ui)
        edge = 0 if reverse else CHUNK - 1
        tot = [b[c * CHUNK + edge:c * CHUNK + edge + 1] for c in range(4)]
        tot_full = jnp.concatenate([jnp.broadcast_to(t, (CHUNK, HEAD_K)) for t in tot], axis=0)
        q = q_ref[0, rows, :].astype(F32)
        k = k_ref[0, rows, :].astype(F32)
        q_t = q * jnp.exp2(b)
        k_t = (k * jnp.exp2(-b)).astype(BF16)
        k_e = k * jnp.exp2(tot_full - b)
        q_tb = q_t.astype(BF16)
        k_eb = k_e.astype(BF16)
        bt = [tot[o] for o in order]

        zero = jnp.zeros((CHUNK, HEAD_K), BF16)
        e1 = jnp.exp2(bt[1])
        e2 = jnp.exp2(bt[2])
        qrows, krows = [None] * 4, [None] * 4
        for t in range(4):
            c = order[t]
            sl = slice(c * CHUNK, (c + 1) * CHUNK)
            qc, kc = q_tb[sl], k_eb[sl]
            q3 = qc if t == 2 else ((q_t[sl] * e2).astype(BF16) if t == 3 else zero)
            k3 = (k_e[sl] * e1).astype(BF16) if t == 0 else (kc if t == 1 else zero)
            qrows[c] = jnp.concatenate([qc if t == 1 else zero, qc if t == 3 else zero, q3], axis=1)
            krows[c] = jnp.concatenate([kc if t == 0 else zero, kc if t == 2 else zero, k3], axis=1)

        gq = to_mem([None, bt[0], bt[0] + bt[1], bt[0] + bt[1] + bt[2]])
        gk = to_mem([bt[1] + bt[2] + bt[3], bt[2] + bt[3], bt[3], None])
        dec = jnp.exp2(bt[0] + bt[1] + bt[2] + bt[3])
        return dict(q_t=q_tb, k_t=k_t, q_cat=jnp.concatenate(qrows, axis=0), k_cat=jnp.concatenate(krows, axis=0),
                    qs=scale_chunks(q_t, gq), ks=scale_chunks(k_e, gk), dec=dec)

    def scores(p):
        same = _dot_nt(p["q_t"], p["k_t"])
        cross = _dot_nt(p["q_cat"], p["k_cat"])
        return jnp.where(code_ref[...] == 1, same, cross).astype(BF16)

    units = range(nunits)
    bs = [decays(ui) for ui in units]
    ps = [operands(ui, bs[ui]) for ui in units]
    atts = [scores(ps[ui]) for ui in units]
    intra = [_dot(atts[ui], v_ref[0, rows_of(ui), :]) for ui in units]
    for ui in units:
        rows = rows_of(ui)
        p = ps[ui]
        st = s_scr[...]
        o = intra[ui] + _dot_nt(p["qs"], st.astype(BF16))
        s_scr[...] = st * p["dec"] + _dot_tn(v_ref[0, rows, :], p["ks"])
        if prev_ref is not None:
            o = o + prev_ref[0, rows, :]
        o_ref[0, rows, :] = o


def _gla(z, wpre, bpre, reverse, prev):
    b, l, _ = z.shape
    lb = min(1024, l)
    nb = l // lb
    nunits = lb // UNIT
    tri, code = _gla_constants(reverse)

    def seq(i):
        return (nb - 1 - i) if reverse else i

    in_specs = [
        pl.BlockSpec((1, lb, HEAD_K), lambda bi, h, i: (bi, seq(i), Z_Q // HEAD_K + h)),
        pl.BlockSpec((1, lb, HEAD_K), lambda bi, h, i: (bi, seq(i), Z_K // HEAD_K + h)),
        pl.BlockSpec((1, lb, HEAD_V), lambda bi, h, i: (bi, seq(i), Z_V // HEAD_V + h)),
        pl.BlockSpec((1, lb, 128), lambda bi, h, i: (bi, seq(i), Z_LR // 128)),
        pl.BlockSpec((1, 128, HEAD_K), lambda bi, h, i: (h, 0, 0)),
        pl.BlockSpec((1, 1, HEAD_K), lambda bi, h, i: (h, 0, 0)),
        pl.BlockSpec((UNIT, UNIT), lambda bi, h, i: (0, 0)),
        pl.BlockSpec((UNIT, UNIT), lambda bi, h, i: (0, 0)),
    ]
    args = [z, z, z, z, wpre, bpre, tri, code]
    if prev is not None:
        in_specs.append(pl.BlockSpec((1, lb, HEAD_V), lambda bi, h, i: (bi, seq(i), h)))
        args.append(prev)
    return pl.pallas_call(
        functools.partial(_gla_kernel, reverse, nunits, prev is not None),
        out_shape=jax.ShapeDtypeStruct((b, l, GLA_HEADS * HEAD_V), F32),
        grid=(b, GLA_HEADS, nb),
        in_specs=in_specs,
        out_specs=pl.BlockSpec((1, lb, HEAD_V), lambda bi, h, i: (bi, seq(i), h)),
        scratch_shapes=[pltpu.VMEM((HEAD_V, HEAD_K), F32)],
        compiler_params=_params(("arbitrary", "arbitrary", "arbitrary")),
        name="gla_bwd" if reverse else "gla_fwd",
    )(*args)


def _mix_kernel(tm, x_ref, o_ref, r_ref, cb_ref, cc_ref, cx_ref, ccp_ref, cxp_ref, ccn_ref, cxn_ref,
                ga_ref, gb_ref, mod_ref, cw_ref, cbias_ref, woa_ref, wob_ref, wo_ref,
                n2_ref, wrt_ref, x1_ref, u2_ref, aff_ref):
    i = pl.program_id(1)
    last = pl.num_programs(1) - 1

    o = o_ref[0]
    parts = []
    for h in range(GLA_HEADS):
        oh = o[:, h * HEAD_V:(h + 1) * HEAD_V]
        ms = jnp.mean(oh * oh, axis=-1, keepdims=True)
        parts.append(oh * lax.rsqrt(ms + EPS))
    on = jnp.concatenate(parts, axis=1)
    r = r_ref[0]
    branch_a = _dot(on.astype(BF16) * (r * _sigmoid(r)), woa_ref[...])

    xc = cc_ref[0].astype(F32) * cx_ref[0].astype(F32)
    prev_row = ccp_ref[0, 15:16, :].astype(F32) * cxp_ref[0, 15:16, :].astype(F32)
    next_row = ccn_ref[0, 0:1, :].astype(F32) * cxn_ref[0, 0:1, :].astype(F32)
    prev_row = jnp.where(i > 0, prev_row, 0.0)
    next_row = jnp.where(i < last, next_row, 0.0)
    rowid = lax.broadcasted_iota(jnp.int32, (tm, 1), 0)
    xm1 = jnp.where(rowid == 0, prev_row, pltpu.roll(xc, 1, axis=0))
    xp1 = jnp.where(rowid == tm - 1, next_row, pltpu.roll(xc, tm - 1, axis=0))
    cw = cw_ref[...]
    conv = xm1 * cw[0:1] + xc * cw[1:2] + xp1 * cw[2:3] + cbias_ref[...]
    branch_b = _dot((cb_ref[0].astype(F32) * conv).astype(BF16), wob_ref[...])

    merged = (_sigmoid(ga_ref[0]) * branch_a.astype(BF16)
              + _sigmoid(gb_ref[0]) * branch_b.astype(BF16))
    mix = _dot(merged, wo_ref[...])

    m = mod_ref[0]
    x1 = x_ref[0] + m[2:3] * mix
    x1_ref[0] = x1
    ms = jnp.mean(x1 * x1, axis=-1, keepdims=True)
    u2 = (x1 * lax.rsqrt(ms + EPS)) * (n2_ref[...] * (1.0 + m[4:5])) + m[3:4]
    u2b = u2.astype(BF16)
    u2_ref[0] = u2b

    logits = _dot_nt(wrt_ref[...], u2b)
    ex = jnp.exp(logits - jnp.max(logits, axis=0, keepdims=True))
    aff = ex / jnp.sum(ex, axis=0, keepdims=True)
    for j in range(tm // SUB):
        aff_ref[j] = aff[:, j * SUB:(j + 1) * SUB]


def _mix_out(x, o, z, mod3, boff, conv_w, conv_b, woa, wob, wo, n2, wrt):
    b, l, d = x.shape
    tm = min(512, l)
    nt = l // tm
    hb = tm // 16
    nhalo = l // 16

    def col(c):
        return lambda bi, i: (bi, i, c // d)

    in_specs = [
        pl.BlockSpec((1, tm, d), lambda bi, i: (bi, i, 0)),
        pl.BlockSpec((1, tm, d), lambda bi, i: (bi, i, 0)),
        pl.BlockSpec((1, tm, d), col(Z_R)),
        pl.BlockSpec((1, tm, d), col(Z_CB)),
        pl.BlockSpec((1, tm, d), col(Z_CC)),
        pl.BlockSpec((1, tm, d), col(Z_CX)),
        pl.BlockSpec((1, 16, d), lambda bi, i: (bi, jnp.maximum(i * hb - 1, 0), Z_CC // d)),
        pl.BlockSpec((1, 16, d), lambda bi, i: (bi, jnp.maximum(i * hb - 1, 0), Z_CX // d)),
        pl.BlockSpec((1, 16, d), lambda bi, i: (bi, jnp.minimum((i + 1) * hb, nhalo - 1), Z_CC // d)),
        pl.BlockSpec((1, 16, d), lambda bi, i: (bi, jnp.minimum((i + 1) * hb, nhalo - 1), Z_CX // d)),
        pl.BlockSpec((1, tm, d), col(Z_GA)),
        pl.BlockSpec((1, tm, d), col(Z_GB)),
        pl.BlockSpec((1, 6, d), lambda bi, i: (bi + boff, 0, 0)),
        pl.BlockSpec((3, d), lambda bi, i: (0, 0)),
        pl.BlockSpec((1, d), lambda bi, i: (0, 0)),
        pl.BlockSpec((d, d), lambda bi, i: (0, 0)),
        pl.BlockSpec((d, d), lambda bi, i: (0, 0)),
        pl.BlockSpec((d, d), lambda bi, i: (0, 0)),
        pl.BlockSpec((1, d), lambda bi, i: (0, 0)),
        pl.BlockSpec((N_EXPERTS, d), lambda bi, i: (0, 0)),
    ]
    out_shape = (jax.ShapeDtypeStruct((b, l, d), F32),
                 jax.ShapeDtypeStruct((b, l, d), BF16),
                 jax.ShapeDtypeStruct((b * l // SUB, N_EXPERTS, SUB), F32))
    out_specs = (pl.BlockSpec((1, tm, d), lambda bi, i: (bi, i, 0)),
                 pl.BlockSpec((1, tm, d), lambda bi, i: (bi, i, 0)),
                 pl.BlockSpec((tm // SUB, N_EXPERTS, SUB), lambda bi, i: (bi * nt + i, 0, 0)))
    return pl.pallas_call(
        functools.partial(_mix_kernel, tm),
        out_shape=out_shape,
        grid=(b, nt),
        in_specs=in_specs,
        out_specs=out_specs,
        compiler_params=_params(("arbitrary", "arbitrary")),
        name="mix_out",
    )(x, o, z, z, z, z, z, z, z, z, z, z, mod3, conv_w, conv_b, woa, wob, wo, n2, wrt)


def _select_kernel(cap, nsb, aff_ref, u_ref, pos_ref, base_ref):
    capf = float(cap)

    def bits(j):
        return pltpu.bitcast(aff_ref[j], jnp.int32)

    def count(pred):
        def body(j, acc):
            return acc + jnp.where(pred(bits(j)), 1.0, 0.0)
        acc = lax.fori_loop(0, nsb, body, jnp.zeros((N_EXPERTS, SUB), F32))
        return jnp.sum(acc, axis=1, keepdims=True)

    def bit_body(t, thr):
        cand = thr | jnp.left_shift(jnp.int32(1), 30 - t)
        return jnp.where(count(lambda v: v >= cand) >= capf, cand, thr)

    thr = lax.fori_loop(0, 31, bit_body, jnp.zeros((N_EXPERTS, SUB), jnp.int32))
    need = capf - count(lambda v: v > thr)

    def blk(j, carry):
        ceq, csel = carry
        v = bits(j)
        gtf = jnp.where(v > thr, 1.0, 0.0)
        eqf = jnp.where(v == thr, 1.0, 0.0)
        r1 = _dot(eqf.astype(BF16), u_ref[...])
        eq_rank = r1[:, :SUB] - eqf + ceq
        sel = gtf + jnp.where(eq_rank < need, eqf, 0.0)
        r2 = _dot(sel.astype(BF16), u_ref[...])
        pos_ref[j] = jnp.where(sel > 0.5, r2[:, :SUB] + csel - 1.0, -1.0).astype(jnp.int32)
        base_ref[j] = csel.astype(jnp.int32)
        return ceq + r1[:, SUB:], csel + r2[:, SUB:]

    zero = jnp.zeros((N_EXPERTS, SUB), F32)
    lax.fori_loop(0, nsb, blk, (zero, zero))


def _select(aff3, cap):
    nsb = aff3.shape[0]
    tri = np.triu(np.ones((SUB, SUB), np.float32))
    umat = jnp.asarray(np.concatenate([tri, np.ones((SUB, SUB), np.float32)], axis=1), BF16)
    shp = jax.ShapeDtypeStruct((nsb, N_EXPERTS, SUB), jnp.int32)
    return pl.pallas_call(
        functools.partial(_select_kernel, cap, nsb),
        out_shape=(shp, shp),
        compiler_params=pltpu.CompilerParams(vmem_limit_bytes=VMEM_LIMIT),
        name="select",
    )(aff3, umat)


def _compact_kernel(nsb, start_ref, pos_ref, x_ref, xe_ref):
    e = pl.program_id(0)

    @pl.when(pl.program_id(1) == 0)
    def _():
        xe_ref[...] = jnp.zeros_like(xe_ref)

    rowi = lax.broadcasted_iota(jnp.int32, (WIN, SUB), 0)

    def body(s, carry):
        st = pl.multiple_of(start_ref[0, 0, s], 16)
        local = pos_ref[s, pl.ds(e, 1), :] - st
        onehot = jnp.where(rowi == local, 1.0, 0.0).astype(BF16)
        xs = x_ref[pl.ds(s * SUB, SUB), :]
        rows = _dot(onehot, xs).astype(BF16)
        xe_ref[0, pl.ds(st, WIN), :] = xe_ref[0, pl.ds(st, WIN), :] + rows
        return carry

    for s in range(nsb):
        body(s, 0)


def _compact(u2, pos3, start, cap, tbig):
    t, d = u2.shape
    nbig = t // tbig
    nsb = tbig // SUB
    return pl.pallas_call(
        functools.partial(_compact_kernel, nsb),
        out_shape=jax.ShapeDtypeStruct((N_EXPERTS, cap, d), BF16),
        grid=(N_EXPERTS, nbig),
        in_specs=[pl.BlockSpec((1, 1, nsb), lambda e, b: (e * nbig + b, 0, 0), memory_space=pltpu.SMEM),
                  pl.BlockSpec((nsb, N_EXPERTS, SUB), lambda e, b: (b, 0, 0)),
                  pl.BlockSpec((tbig, d), lambda e, b: (b, 0))],
        out_specs=pl.BlockSpec((1, cap, d), lambda e, b: (e, 0, 0)),
        compiler_params=_params(("arbitrary", "arbitrary")),
        name="compact",
    )(start.reshape(N_EXPERTS * nbig, 1, nsb), pos3, u2)


def _ffn_kernel(tm, x_ref, wg_ref, wu_ref, wd_ref, o_ref, acc_ref):
    f = pl.program_id(2)

    @pl.when(f == 0)
    def _():
        acc_ref[...] = jnp.zeros_like(acc_ref)

    half = min(512, tm)
    for r in range(tm // half):
        rows = slice(r * half, (r + 1) * half)
        xs = x_ref[0, rows, :]
        g = _dot(xs, wg_ref[0])
        u = _dot(xs, wu_ref[0])
        h = (g * jax.nn.sigmoid(g)) * u
        acc_ref[rows, :] += _dot(h.astype(BF16), wd_ref[0])

    @pl.when(f == pl.num_programs(2) - 1)
    def _():
        o_ref[0] = acc_ref[...].astype(BF16)


def _ffn(xe, wg, wu, wd):
    e, cap, d = xe.shape
    f = wg.shape[2]
    tm = min(1024, cap)
    tf = 1408
    return pl.pallas_call(
        functools.partial(_ffn_kernel, tm),
        out_shape=jax.ShapeDtypeStruct((e, cap, d), BF16),
        grid=(e, cap // tm, f // tf),
        in_specs=[pl.BlockSpec((1, tm, d), lambda ei, m, fi: (ei, m, 0)),
                  pl.BlockSpec((1, d, tf), lambda ei, m, fi: (ei, 0, fi)),
                  pl.BlockSpec((1, d, tf), lambda ei, m, fi: (ei, 0, fi)),
                  pl.BlockSpec((1, tf, d), lambda ei, m, fi: (ei, fi, 0))],
        out_specs=pl.BlockSpec((1, tm, d), lambda ei, m, fi: (ei, m, 0)),
        scratch_shapes=[pltpu.VMEM((tm, d), F32)],
        compiler_params=_params(("arbitrary", "arbitrary", "arbitrary")),
        name="expert_ffn",
    )(xe, wg, wu, wd)


def _uncompact_kernel(nsb, nbig, abig_ref, rel_ref, pos_ref, aff_ref, ye_ref, x1_ref, mod_ref, nf_ref,
                      out_ref, acc_ref):
    b = pl.program_id(0)
    e = pl.program_id(1)

    @pl.when(e == 0)
    def _():
        acc_ref[...] = jnp.zeros_like(acc_ref)

    rowi = lax.broadcasted_iota(jnp.int32, (WIN, SUB), 0)
    big = abig_ref[e * nbig + b]

    def body(s, carry):
        rel = pl.multiple_of(rel_ref[0, 0, s], 16)
        local = pos_ref[s, pl.ds(e, 1), :] - (big + rel)
        gate = aff_ref[s, pl.ds(e, 1), :]
        weights = jnp.where(rowi == local, gate, 0.0).astype(BF16)
        rows = pl.ds(s * SUB, SUB)
        acc_ref[rows, :] += _dot_tn(weights, ye_ref[pl.ds(rel, WIN), :])
        return carry

    for s in range(nsb):
        body(s, 0)

    @pl.when(e == pl.num_programs(1) - 1)
    def _():
        xo = x1_ref[...] + mod_ref[0][5:6] * acc_ref[...]
        ms = jnp.mean(xo * xo, axis=-1, keepdims=True)
        out_ref[...] = xo * lax.rsqrt(ms + EPS) * nf_ref[...]


def _uncompact(ye, pos3, aff3, abig, rel, x1, mod3, boff, seq_len, nf, tbig):
    t, d = x1.shape
    cap = ye.shape[1]
    nbig = t // tbig
    nsb = tbig // SUB
    wbig = tbig + 32
    grid_spec = pltpu.PrefetchScalarGridSpec(
        num_scalar_prefetch=1,
        grid=(nbig, N_EXPERTS),
        in_specs=[
            pl.BlockSpec((1, 1, nsb), lambda b, e, ab: (e * nbig + b, 0, 0), memory_space=pltpu.SMEM),
            pl.BlockSpec((nsb, N_EXPERTS, SUB), lambda b, e, ab: (b, 0, 0)),
            pl.BlockSpec((nsb, N_EXPERTS, SUB), lambda b, e, ab: (b, 0, 0)),
            pl.BlockSpec((pl.Element(wbig), pl.Element(d)),
                         lambda b, e, ab: (pl.multiple_of(e * cap + ab[e * nbig + b], 16), 0)),
            pl.BlockSpec((tbig, d), lambda b, e, ab: (b, 0)),
            pl.BlockSpec((1, 6, d), lambda b, e, ab: (boff + (b * tbig) // seq_len, 0, 0)),
            pl.BlockSpec((1, d), lambda b, e, ab: (0, 0)),
        ],
        out_specs=pl.BlockSpec((tbig, d), lambda b, e, ab: (b, 0)),
        scratch_shapes=[pltpu.VMEM((tbig, d), F32)],
    )
    return pl.pallas_call(
        functools.partial(_uncompact_kernel, nsb, nbig),
        out_shape=jax.ShapeDtypeStruct((t, d), F32),
        grid_spec=grid_spec,
        compiler_params=_params(("arbitrary", "arbitrary")),
        name="uncompact",
    )(abig, rel.reshape(N_EXPERTS * nbig, 1, nsb), pos3, aff3, ye.reshape(N_EXPERTS * cap, d), x1, mod3, nf)


def _uncompact_small_kernel(nsub, ab_ref, rel_ref, pos_ref, aff_ref, *refs):
    ye_refs = refs[:N_EXPERTS]
    x1_ref, mod_ref, nf_ref, out_ref = refs[N_EXPERTS:]
    p = pl.program_id(0)
    npair = nsub // 2
    rowi = lax.broadcasted_iota(jnp.int32, (SMALL_WIN, SUB), 0)
    ys = []
    for j in range(2):
        pos = pos_ref[j]
        aff = aff_ref[j]
        weights, windows = [], []
        for e in range(N_EXPERTS):
            rel = pl.multiple_of(rel_ref[e * nsub + 2 * p + j], 16)
            local = pos[e:e + 1] - (ab_ref[e * npair + p] + rel)
            weights.append(jnp.where(rowi == local, aff[e:e + 1], 0.0).astype(BF16))
            windows.append(ye_refs[e][pl.ds(rel, SMALL_WIN), :])
        ys.append(_dot_tn(jnp.concatenate(weights, axis=0), jnp.concatenate(windows, axis=0)))
    xo = x1_ref[...] + mod_ref[0][5:6] * jnp.concatenate(ys, axis=0)
    ms = jnp.mean(xo * xo, axis=-1, keepdims=True)
    out_ref[...] = xo * lax.rsqrt(ms + EPS) * nf_ref[...]


def _uncompact_small(ye, pos3, aff3, ab, rel, x1, mod3, boff, seq_len, nf):
    t, d = x1.shape
    cap = ye.shape[1]
    nsub = t // SUB
    npair = nsub // 2

    def window(e):
        return pl.BlockSpec((pl.Element(PAIR_WIN), pl.Element(d)),
                            lambda p, ab_ref, rel_ref: (pl.multiple_of(e * cap + ab_ref[e * npair + p], 16), 0))

    grid_spec = pltpu.PrefetchScalarGridSpec(
        num_scalar_prefetch=2,
        grid=(npair,),
        in_specs=[pl.BlockSpec((2, N_EXPERTS, SUB), lambda p, ab_ref, rel_ref: (p, 0, 0)),
                  pl.BlockSpec((2, N_EXPERTS, SUB), lambda p, ab_ref, rel_ref: (p, 0, 0))]
        + [window(e) for e in range(N_EXPERTS)]
        + [pl.BlockSpec((2 * SUB, d), lambda p, ab_ref, rel_ref: (p, 0)),
           pl.BlockSpec((1, 6, d), lambda p, ab_ref, rel_ref: (boff + (p * 2 * SUB) // seq_len, 0, 0)),
           pl.BlockSpec((1, d), lambda p, ab_ref, rel_ref: (0, 0))],
        out_specs=pl.BlockSpec((2 * SUB, d), lambda p, ab_ref, rel_ref: (p, 0)),
    )
    ye2 = ye.reshape(N_EXPERTS * cap, d)
    return pl.pallas_call(
        functools.partial(_uncompact_small_kernel, nsub),
        out_shape=jax.ShapeDtypeStruct((t, d), F32),
        grid_spec=grid_spec,
        compiler_params=_params(("arbitrary",)),
        name="uncompact_small",
    )(ab, rel, pos3, aff3, *([ye2] * N_EXPERTS), x1, mod3, nf)


def _floor16(v):
    return (v // 16) * 16


def _trunk(x, mod3, boff, w):
    b, l, d = x.shape
    t = b * l
    cap = EC_FACTOR * t // N_EXPERTS

    z = _in_proj(x, mod3, boff, w["norm1_w"], w["w_in"])
    o = _gla(z, w["wpre_f"], w["bpre_f"], False, None)
    o = _gla(z, w["wpre_b"], w["bpre_b"], True, o)
    x1, u2, aff3 = _mix_out(x, o, z, mod3, boff, w["conv_w"], w["conv_b"],
                            w["w_out_a"], w["w_out_b"], w["w_o"], w["norm2_w"], w["w_router_t"])
    pos3, base3 = _select(aff3, cap)

    base = base3[:, :, 0].T
    start = jnp.minimum(_floor16(base), cap - WIN)
    tbig_c = min(2048, t)
    tbig_u = min(1024, t)
    while tbig_u + 32 > cap:
        tbig_u //= 2
    nsb_u = tbig_u // SUB
    abig = jnp.minimum(_floor16(base[:, ::nsb_u]), cap - (tbig_u + 32))
    rel = start - jnp.repeat(abig, nsb_u, axis=1)

    u2f = u2.reshape(t, d)
    xe = _compact(u2f, pos3, start, cap, tbig_c)
    ye = _ffn(xe, w["w_gate"], w["w_up"], w["w_down"])
    x1f = x1.reshape(t, d)
    base_next = jnp.concatenate([base[:, 1:], jnp.full((N_EXPERTS, 1), cap, jnp.int32)], axis=1)
    start_small = jnp.minimum(_floor16(base), cap - SMALL_WIN)
    ab_pair = jnp.minimum(_floor16(base[:, ::2]), cap - PAIR_WIN)
    rel_small = start_small - jnp.repeat(ab_pair, 2, axis=1)
    fits_small = (jnp.all(base_next - _floor16(base) <= SMALL_WIN)
                  & jnp.all(rel_small <= PAIR_WIN - SMALL_WIN))
    out = lax.cond(
        fits_small,
        lambda: _uncompact_small(ye, pos3, aff3, ab_pair.reshape(-1), rel_small.reshape(-1), x1f, mod3, boff,
                                 l, w["norm_f_w"]),
        lambda: _uncompact(ye, pos3, aff3, abig.reshape(-1), rel, x1f, mod3, boff, l, w["norm_f_w"], tbig_u))
    return out.reshape(b, l, d)


def _prepare_weights(w_in, w_alpha_f, b_alpha_f, w_alpha_b, b_alpha_b, gla_norm_w, conv_w, conv_b,
                     w_out_a, w_out_b, w_o, norm1_w, norm2_w, w_router, w_gate, w_up, w_down, norm_f_w):
    d = D_MODEL
    wq, wk, wv, wr, wlf, wlb, wcb, wcc, wcx, wga, wgb = jnp.split(
        w_in, np.cumsum([512, 512, 1024, 1024, 16, 16, 1024, 1024, 1024, 1024])[:].tolist(), axis=1)
    pad = jnp.zeros((d, Z_WIDTH - Z_LR - 2 * GATE_RANK), w_in.dtype)
    wq = wq * (HEAD_K ** -0.5)
    w_in_p = jnp.concatenate([wq, wk, wv, wr, wcb, wcc, wcx, wga, wgb, wlf, wlb, pad], axis=1).astype(BF16)

    def pre(wa, ba, row0):
        wh = wa.reshape(GATE_RANK, GLA_HEADS, HEAD_K).transpose(1, 0, 2)
        full = jnp.zeros((GLA_HEADS, 128, HEAD_K), F32).at[:, row0:row0 + GATE_RANK].set(wh)
        return full.astype(BF16), ba.reshape(GLA_HEADS, 1, HEAD_K)

    wpre_f, bpre_f = pre(w_alpha_f, b_alpha_f, 0)
    wpre_b, bpre_b = pre(w_alpha_b, b_alpha_b, GATE_RANK)
    return dict(
        w_in=w_in_p, wpre_f=wpre_f, bpre_f=bpre_f, wpre_b=wpre_b, bpre_b=bpre_b,
        conv_w=conv_w, conv_b=conv_b.reshape(1, d),
        w_out_a=(gla_norm_w.reshape(d, 1) * w_out_a).astype(BF16),
        w_out_b=w_out_b.astype(BF16), w_o=w_o.astype(BF16),
        norm1_w=norm1_w, norm2_w=norm2_w.reshape(1, d), w_router_t=w_router.T.astype(BF16),
        w_gate=w_gate.astype(BF16), w_up=w_up.astype(BF16), w_down=w_down.astype(BF16),
        norm_f_w=norm_f_w.reshape(1, d))


def kernel(x_prompt, x_sample, c_prompt, c_sample, w_ada, b_ada, norm1_w, w_in, w_alpha_f, b_alpha_f, w_alpha_b, b_alpha_b, gla_norm_w, conv_w, conv_b, w_out_a, w_out_b, w_o, norm2_w, w_router, w_gate, w_up, w_down, norm_f_w):
    d = D_MODEL
    bp, bs = c_prompt.shape[0], c_sample.shape[0]
    nrows = -(-(bp + bs) // 8) * 8
    c_all = jnp.concatenate([c_prompt, c_sample, jnp.zeros((nrows - bp - bs, d), F32)], axis=0)
    mod3 = _adaln(c_all, w_ada[0], b_ada[0]).reshape(nrows, 6, d)
    w = _prepare_weights(w_in[0], w_alpha_f[0], b_alpha_f[0], w_alpha_b[0], b_alpha_b[0], gla_norm_w[0],
                         conv_w[0], conv_b[0], w_out_a[0], w_out_b[0], w_o[0], norm1_w[0], norm2_w[0],
                         w_router[0], w_gate[0], w_up[0], w_down[0], norm_f_w)
    y_prompt = _trunk(x_prompt, mod3, 0, w)
    y_sample = _trunk(x_sample, mod3, bp, w)
    return (y_prompt, y_sample)
```

```python
import functools

import numpy as np
import jax
import jax.numpy as jnp
from jax import lax
from jax.experimental import pallas as pl
from jax.experimental.pallas import tpu as pltpu

F32 = jnp.float32
BF16 = jnp.bfloat16

D_MODEL = 1024
GLA_HEADS = 4
HEAD_K = 128
HEAD_V = 256
GATE_RANK = 16
GATE_TAU = 16.0
CHUNK = 64
N_EXPERTS = 16
EC_FACTOR = 2
D_EXPERT = 2816
EPS = 1e-6
LOG2E = 1.4426950408889634

Z_Q, Z_K, Z_V, Z_R, Z_CB, Z_CC, Z_CX, Z_GA, Z_GB, Z_LR = 0, 512, 1024, 2048, 3072, 4096, 5120, 6144, 7168, 8192
Z_WIDTH = 8448

UNIT = 4 * CHUNK
SUB = 128
WIN = SUB + 16
SMALL_WIN = 64
PAIR_WIN = 112
CWIN = 96
VMEM_LIMIT = 56 * 1024 * 1024


def _dot(a, b):
    return jnp.dot(a, b, preferred_element_type=F32)


def _dot_nt(a, b):
    return lax.dot_general(a, b, (((1,), (1,)), ((), ())), preferred_element_type=F32)


def _sigmoid(x):
    return 0.5 * jnp.tanh(0.5 * x) + 0.5


def _dot_tn(a, b):
    return lax.dot_general(a, b, (((0,), (0,)), ((), ())), preferred_element_type=F32)


def _params(sem, limit=VMEM_LIMIT):
    return pltpu.CompilerParams(dimension_semantics=sem, vmem_limit_bytes=limit)


def _adaln_kernel(c_ref, w_ref, b_ref, o_ref):
    c = c_ref[...]
    s = c * jax.nn.sigmoid(c)
    o_ref[...] = jnp.dot(s, w_ref[...], preferred_element_type=F32,
                         precision=lax.Precision.HIGHEST) + b_ref[...]


def _adaln(c_all, w_ada, b_ada):
    nb, d = c_all.shape
    n = w_ada.shape[1]
    tn = 1536
    return pl.pallas_call(
        _adaln_kernel,
        out_shape=jax.ShapeDtypeStruct((nb, n), F32),
        grid=(n // tn,),
        in_specs=[pl.BlockSpec((nb, d), lambda j: (0, 0)),
                  pl.BlockSpec((d, tn), lambda j: (0, j)),
                  pl.BlockSpec((1, tn), lambda j: (0, j))],
        out_specs=pl.BlockSpec((nb, tn), lambda j: (0, j)),
        compiler_params=_params(("arbitrary",)),
        name="adaln",
    )(c_all, w_ada, b_ada.reshape(1, n))


def _inproj_kernel(x_ref, mod_ref, n1_ref, w_ref, z_ref, u_scr):
    @pl.when(pl.program_id(2) == 0)
    def _():
        x = x_ref[0]
        ms = jnp.mean(x * x, axis=-1, keepdims=True)
        xn = x * lax.rsqrt(ms + EPS) * n1_ref[...]
        m = mod_ref[0]
        u_scr[...] = (xn * (1.0 + m[1:2]) + m[0:1]).astype(BF16)

    z_ref[0] = _dot(u_scr[...], w_ref[...]).astype(BF16)


def _in_proj(x, mod3, boff, norm1_w, w_in_p):
    b, l, d = x.shape
    tm = min(1024, l)
    tn = 2816
    return pl.pallas_call(
        _inproj_kernel,
        out_shape=jax.ShapeDtypeStruct((b, l, Z_WIDTH), BF16),
        grid=(b, l // tm, Z_WIDTH // tn),
        in_specs=[pl.BlockSpec((1, tm, d), lambda bi, i, n: (bi, i, 0)),
                  pl.BlockSpec((1, 6, d), lambda bi, i, n: (bi + boff, 0, 0)),
                  pl.BlockSpec((1, d), lambda bi, i, n: (0, 0)),
                  pl.BlockSpec((d, tn), lambda bi, i, n: (0, n))],
        out_specs=pl.BlockSpec((1, tm, tn), lambda bi, i, n: (bi, i, n)),
        scratch_shapes=[pltpu.VMEM((tm, d), BF16)],
        compiler_params=_params(("arbitrary", "arbitrary", "arbitrary")),
        name="in_proj",
    )(x, mod3, norm1_w.reshape(1, d), w_in_p)


def _gla_constants(reverse):
    r = np.arange(UNIT)[:, None]
    s = np.arange(UNIT)[None, :]
    cr, cs = r // CHUNK, s // CHUNK
    if reverse:
        tri = (cr == cs) & (s >= r)
        tr, ts = 3 - cr, 3 - cs
    else:
        tri = (cr == cs) & (s <= r)
        tr, ts = cr, cs
    code = np.zeros((UNIT, UNIT), np.int32)
    code[tri] = 1
    return jnp.asarray(tri, BF16), jnp.asarray(code)


def _gla_kernel(reverse, nunits, has_prev, *refs):
    if has_prev:
        q_ref, k_ref, v_ref, lr_ref, wp_ref, bp_ref, tri_ref, code_ref, prev_ref, o_ref, s_scr = refs
    else:
        q_ref, k_ref, v_ref, lr_ref, wp_ref, bp_ref, tri_ref, code_ref, o_ref, s_scr = refs
        prev_ref = None

    @pl.when(pl.program_id(2) == 0)
    def _():
        s_scr[...] = jnp.zeros_like(s_scr)

    order = (3, 2, 1, 0) if reverse else (0, 1, 2, 3)

    def to_mem(by_step):
        out = [None] * 4
        for t in range(4):
            out[order[t]] = by_step[t]
        return out

    def scale_chunks(arr, logf):
        parts = []
        for c in range(4):
            part = arr[c * CHUNK:(c + 1) * CHUNK]
            if logf[c] is not None:
                part = part * jnp.exp2(logf[c])
            parts.append(part)
        return jnp.concatenate(parts, axis=0).astype(BF16)

    def rows_of(ui):
        um = (nunits - 1 - ui) if reverse else ui
        return pl.ds(um * UNIT, UNIT)

    def decays(ui):
        pre = _dot(lr_ref[0, rows_of(ui), :], wp_ref[0]) + bp_ref[0]
        soft = jnp.log2(1.0 + jnp.exp2(jnp.abs(pre) * (-LOG2E)))
        la = (jnp.minimum(pre, 0.0) * LOG2E - soft) * (1.0 / GATE_TAU)
        hi = la.astype(BF16)
        lo = (la - hi.astype(F32)).astype(BF16)
        bb = _dot(tri_ref[...], jnp.concatenate([hi, lo], axis=1))
        return bb[:, :HEAD_K] + bb[:, HEAD_K:]

    def operands(ui, b):
        rows = rows_of(ui)
        edge = 0 if reverse else CHUNK - 1
        tot = [b[c * CHUNK + edge:c * CHUNK + edge + 1] for c in range(4)]
        tot_full = jnp.concatenate([jnp.broadcast_to(t, (CHUNK, HEAD_K)) for t in tot], axis=0)
        q = q_ref[0, rows, :].astype(F32)
        k = k_ref[0, rows, :].astype(F32)
        q_t = q * jnp.exp2(b)
        k_t = (k * jnp.exp2(-b)).astype(BF16)
        k_e = k * jnp.exp2(tot_full - b)
        q_tb = q_t.astype(BF16)
        k_eb = k_e.astype(BF16)
        bt = [tot[o] for o in order]

        zero = jnp.zeros((CHUNK, HEAD_K), BF16)
        e1 = jnp.exp2(bt[1])
        e2 = jnp.exp2(bt[2])
        qrows, krows = [None] * 4, [None] * 4
        for t in range(4):
            c = order[t]
            sl = slice(c * CHUNK, (c + 1) * CHUNK)
            qc, kc = q_tb[sl], k_eb[sl]
            q3 = qc if t == 2 else ((q_t[sl] * e2).astype(BF16) if t == 3 else zero)
            k3 = (k_e[sl] * e1).astype(BF16) if t == 0 else (kc if t == 1 else zero)
            qrows[c] = jnp.concatenate([qc if t == 1 else zero, qc if t == 3 else zero, q3], axis=1)
            krows[c] = jnp.concatenate([kc if t == 0 else zero, kc if t == 2 else zero, k3], axis=1)

        gq = to_mem([None, bt[0], bt[0] + bt[1], bt[0] + bt[1] + bt[2]])
        gk = to_mem([bt[1] + bt[2] + bt[3], bt[2] + bt[3], bt[3], None])
        dec = jnp.exp2(bt[0] + bt[1] + bt[2] + bt[3])
        return dict(q_t=q_tb, k_t=k_t, q_cat=jnp.concatenate(qrows, axis=0), k_cat=jnp.concatenate(krows, axis=0),
                    qs=scale_chunks(q_t, gq), ks=scale_chunks(k_e, gk), dec=dec)

    def scores(p):
        same = _dot_nt(p["q_t"], p["k_t"])
        cross = _dot_nt(p["q_cat"], p["k_cat"])
        return jnp.where(code_ref[...] == 1, same, cross).astype(BF16)

    units = range(nunits)
    bs = [decays(ui) for ui in units]
    ps = [operands(ui, bs[ui]) for ui in units]
    atts = [scores(ps[ui]) for ui in units]
    intra = [_dot(atts[ui], v_ref[0, rows_of(ui), :]) for ui in units]
    for ui in units:
        rows = rows_of(ui)
        p = ps[ui]
        st = s_scr[...]
        o = intra[ui] + _dot_nt(p["qs"], st.astype(BF16))
        s_scr[...] = st * p["dec"] + _dot_tn(v_ref[0, rows, :], p["ks"])
        if prev_ref is not None:
            o = o + prev_ref[0, rows, :]
        o_ref[0, rows, :] = o


def _gla(z, wpre, bpre, reverse, prev):
    b, l, _ = z.shape
    lb = min(1024, l)
    nb = l // lb
    nunits = lb // UNIT
    tri, code = _gla_constants(reverse)

    def seq(i):
        return (nb - 1 - i) if reverse else i

    in_specs = [
        pl.BlockSpec((1, lb, HEAD_K), lambda bi, h, i: (bi, seq(i), Z_Q // HEAD_K + h)),
        pl.BlockSpec((1, lb, HEAD_K), lambda bi, h, i: (bi, seq(i), Z_K // HEAD_K + h)),
        pl.BlockSpec((1, lb, HEAD_V), lambda bi, h, i: (bi, seq(i), Z_V // HEAD_V + h)),
        pl.BlockSpec((1, lb, 128), lambda bi, h, i: (bi, seq(i), Z_LR // 128)),
        pl.BlockSpec((1, 128, HEAD_K), lambda bi, h, i: (h, 0, 0)),
        pl.BlockSpec((1, 1, HEAD_K), lambda bi, h, i: (h, 0, 0)),
        pl.BlockSpec((UNIT, UNIT), lambda bi, h, i: (0, 0)),
        pl.BlockSpec((UNIT, UNIT), lambda bi, h, i: (0, 0)),
    ]
    args = [z, z, z, z, wpre, bpre, tri, code]
    if prev is not None:
        in_specs.append(pl.BlockSpec((1, lb, HEAD_V), lambda bi, h, i: (bi, seq(i), h)))
        args.append(prev)
    return pl.pallas_call(
        functools.partial(_gla_kernel, reverse, nunits, prev is not None),
        out_shape=jax.ShapeDtypeStruct((b, l, GLA_HEADS * HEAD_V), F32),
        grid=(b, GLA_HEADS, nb),
        in_specs=in_specs,
        out_specs=pl.BlockSpec((1, lb, HEAD_V), lambda bi, h, i: (bi, seq(i), h)),
        scratch_shapes=[pltpu.VMEM((HEAD_V, HEAD_K), F32)],
        compiler_params=_params(("arbitrary", "arbitrary", "arbitrary")),
        name="gla_bwd" if reverse else "gla_fwd",
    )(*args)


def _mix_kernel(tm, x_ref, o_ref, r_ref, cb_ref, cc_ref, cx_ref, ccp_ref, cxp_ref, ccn_ref, cxn_ref,
                ga_ref, gb_ref, mod_ref, cw_ref, cbias_ref, woa_ref, wob_ref, wo_ref,
                n2_ref, wrt_ref, x1_ref, u2_ref, aff_ref):
    i = pl.program_id(1)
    last = pl.num_programs(1) - 1

    o = o_ref[0]
    parts = []
    for h in range(GLA_HEADS):
        oh = o[:, h * HEAD_V:(h + 1) * HEAD_V]
        ms = jnp.mean(oh * oh, axis=-1, keepdims=True)
        parts.append(oh * lax.rsqrt(ms + EPS))
    on = jnp.concatenate(parts, axis=1)
    r = r_ref[0]
    branch_a = _dot(on.astype(BF16) * (r * _sigmoid(r)), woa_ref[...])

    xc = cc_ref[0].astype(F32) * cx_ref[0].astype(F32)
    prev_row = ccp_ref[0, 15:16, :].astype(F32) * cxp_ref[0, 15:16, :].astype(F32)
    next_row = ccn_ref[0, 0:1, :].astype(F32) * cxn_ref[0, 0:1, :].astype(F32)
    prev_row = jnp.where(i > 0, prev_row, 0.0)
    next_row = jnp.where(i < last, next_row, 0.0)
    rowid = lax.broadcasted_iota(jnp.int32, (tm, 1), 0)
    xm1 = jnp.where(rowid == 0, prev_row, pltpu.roll(xc, 1, axis=0))
    xp1 = jnp.where(rowid == tm - 1, next_row, pltpu.roll(xc, tm - 1, axis=0))
    cw = cw_ref[...]
    conv = xm1 * cw[0:1] + xc * cw[1:2] + xp1 * cw[2:3] + cbias_ref[...]
    branch_b = _dot((cb_ref[0].astype(F32) * conv).astype(BF16), wob_ref[...])

    merged = (_sigmoid(ga_ref[0]) * branch_a.astype(BF16)
              + _sigmoid(gb_ref[0]) * branch_b.astype(BF16))
    mix = _dot(merged, wo_ref[...])

    m = mod_ref[0]
    x1 = x_ref[0] + m[2:3] * mix
    x1_ref[0] = x1
    ms = jnp.mean(x1 * x1, axis=-1, keepdims=True)
    u2 = (x1 * lax.rsqrt(ms + EPS)) * (n2_ref[...] * (1.0 + m[4:5])) + m[3:4]
    u2b = u2.astype(BF16)
    u2_ref[0] = u2b

    logits = _dot_nt(wrt_ref[...], u2b)
    ex = jnp.exp(logits - jnp.max(logits, axis=0, keepdims=True))
    aff = ex / jnp.sum(ex, axis=0, keepdims=True)
    for j in range(tm // SUB):
        aff_ref[j] = aff[:, j * SUB:(j + 1) * SUB]


def _mix_out(x, o, z, mod3, boff, conv_w, conv_b, woa, wob, wo, n2, wrt):
    b, l, d = x.shape
    tm = min(512, l)
    nt = l // tm
    hb = tm // 16
    nhalo = l // 16

    def col(c):
        return lambda bi, i: (bi, i, c // d)

    in_specs = [
        pl.BlockSpec((1, tm, d), lambda bi, i: (bi, i, 0)),
        pl.BlockSpec((1, tm, d), lambda bi, i: (bi, i, 0)),
        pl.BlockSpec((1, tm, d), col(Z_R)),
        pl.BlockSpec((1, tm, d), col(Z_CB)),
        pl.BlockSpec((1, tm, d), col(Z_CC)),
        pl.BlockSpec((1, tm, d), col(Z_CX)),
        pl.BlockSpec((1, 16, d), lambda bi, i: (bi, jnp.maximum(i * hb - 1, 0), Z_CC // d)),
        pl.BlockSpec((1, 16, d), lambda bi, i: (bi, jnp.maximum(i * hb - 1, 0), Z_CX // d)),
        pl.BlockSpec((1, 16, d), lambda bi, i: (bi, jnp.minimum((i + 1) * hb, nhalo - 1), Z_CC // d)),
        pl.BlockSpec((1, 16, d), lambda bi, i: (bi, jnp.minimum((i + 1) * hb, nhalo - 1), Z_CX // d)),
        pl.BlockSpec((1, tm, d), col(Z_GA)),
        pl.BlockSpec((1, tm, d), col(Z_GB)),
        pl.BlockSpec((1, 6, d), lambda bi, i: (bi + boff, 0, 0)),
        pl.BlockSpec((3, d), lambda bi, i: (0, 0)),
        pl.BlockSpec((1, d), lambda bi, i: (0, 0)),
        pl.BlockSpec((d, d), lambda bi, i: (0, 0)),
        pl.BlockSpec((d, d), lambda bi, i: (0, 0)),
        pl.BlockSpec((d, d), lambda bi, i: (0, 0)),
        pl.BlockSpec((1, d), lambda bi, i: (0, 0)),
        pl.BlockSpec((N_EXPERTS, d), lambda bi, i: (0, 0)),
    ]
    out_shape = (jax.ShapeDtypeStruct((b, l, d), F32),
                 jax.ShapeDtypeStruct((b, l, d), BF16),
                 jax.ShapeDtypeStruct((b * l // SUB, N_EXPERTS, SUB), F32))
    out_specs = (pl.BlockSpec((1, tm, d), lambda bi, i: (bi, i, 0)),
                 pl.BlockSpec((1, tm, d), lambda bi, i: (bi, i, 0)),
                 pl.BlockSpec((tm // SUB, N_EXPERTS, SUB), lambda bi, i: (bi * nt + i, 0, 0)))
    return pl.pallas_call(
        functools.partial(_mix_kernel, tm),
        out_shape=out_shape,
        grid=(b, nt),
        in_specs=in_specs,
        out_specs=out_specs,
        compiler_params=_params(("arbitrary", "arbitrary")),
        name="mix_out",
    )(x, o, z, z, z, z, z, z, z, z, z, z, mod3, conv_w, conv_b, woa, wob, wo, n2, wrt)


def _select_kernel(cap, nsb, aff_ref, u_ref, pos_ref, base_ref):
    capf = float(cap)

    def bits(j):
        return pltpu.bitcast(aff_ref[j], jnp.int32)

    def count(pred):
        def body(j, acc):
            return acc + jnp.where(pred(bits(j)), 1.0, 0.0)
        acc = lax.fori_loop(0, nsb, body, jnp.zeros((N_EXPERTS, SUB), F32))
        return jnp.sum(acc, axis=1, keepdims=True)

    def bit_body(t, thr):
        cand = thr | jnp.left_shift(jnp.int32(1), 30 - t)
        return jnp.where(count(lambda v: v >= cand) >= capf, cand, thr)

    thr = lax.fori_loop(0, 31, bit_body, jnp.zeros((N_EXPERTS, SUB), jnp.int32))
    need = capf - count(lambda v: v > thr)

    def blk(j, carry):
        ceq, csel = carry
        v = bits(j)
        gtf = jnp.where(v > thr, 1.0, 0.0)
        eqf = jnp.where(v == thr, 1.0, 0.0)
        r1 = _dot(eqf.astype(BF16), u_ref[...])
        eq_rank = r1[:, :SUB] - eqf + ceq
        sel = gtf + jnp.where(eq_rank < need, eqf, 0.0)
        r2 = _dot(sel.astype(BF16), u_ref[...])
        pos_ref[j] = jnp.where(sel > 0.5, r2[:, :SUB] + csel - 1.0, -1.0).astype(jnp.int32)
        base_ref[j] = csel.astype(jnp.int32)
        return ceq + r1[:, SUB:], csel + r2[:, SUB:]

    zero = jnp.zeros((N_EXPERTS, SUB), F32)
    lax.fori_loop(0, nsb, blk, (zero, zero))


def _select(aff3, cap):
    nsb = aff3.shape[0]
    tri = np.triu(np.ones((SUB, SUB), np.float32))
    umat = jnp.asarray(np.concatenate([tri, np.ones((SUB, SUB), np.float32)], axis=1), BF16)
    shp = jax.ShapeDtypeStruct((nsb, N_EXPERTS, SUB), jnp.int32)
    return pl.pallas_call(
        functools.partial(_select_kernel, cap, nsb),
        out_shape=(shp, shp),
        compiler_params=pltpu.CompilerParams(vmem_limit_bytes=VMEM_LIMIT),
        name="select",
    )(aff3, umat)


def _compact_kernel(nsb, start_ref, pos_ref, x_ref, xe_ref):
    e = pl.program_id(0)

    @pl.when(pl.program_id(1) == 0)
    def _():
        xe_ref[...] = jnp.zeros_like(xe_ref)

    rowi = lax.broadcasted_iota(jnp.int32, (WIN, SUB), 0)

    def body(s, carry):
        st = pl.multiple_of(start_ref[0, 0, s], 16)
        local = pos_ref[s, pl.ds(e, 1), :] - st
        onehot = jnp.where(rowi == local, 1.0, 0.0).astype(BF16)
        xs = x_ref[pl.ds(s * SUB, SUB), :]
        rows = _dot(onehot, xs).astype(BF16)
        xe_ref[0, pl.ds(st, WIN), :] = xe_ref[0, pl.ds(st, WIN), :] + rows
        return carry

    for s in range(nsb):
        body(s, 0)


def _compact(u2, pos3, start, cap, tbig):
    t, d = u2.shape
    nbig = t // tbig
    nsb = tbig // SUB
    return pl.pallas_call(
        functools.partial(_compact_kernel, nsb),
        out_shape=jax.ShapeDtypeStruct((N_EXPERTS, cap + CWIN, d), BF16),
        grid=(N_EXPERTS, nbig),
        in_specs=[pl.BlockSpec((1, 1, nsb), lambda e, b: (e * nbig + b, 0, 0), memory_space=pltpu.SMEM),
                  pl.BlockSpec((nsb, N_EXPERTS, SUB), lambda e, b: (b, 0, 0)),
                  pl.BlockSpec((tbig, d), lambda e, b: (b, 0))],
        out_specs=pl.BlockSpec((1, cap + CWIN, d), lambda e, b: (e, 0, 0)),
        compiler_params=_params(("arbitrary", "arbitrary")),
        name="compact",
    )(start.reshape(N_EXPERTS * nbig, 1, nsb), pos3, u2)


def _compact_small_kernel(nblk, cap, start_ref, next_ref, pos_ref, x_ref, xe_ref, strip_buf, tail_buf, zero_buf,
                          sems, zsem):
    p = pl.program_id(0)
    last = pl.num_programs(0) - 1
    slot = lax.rem(p, 2)

    @pl.when(p == 0)
    def _():
        tail_buf[...] = jnp.zeros_like(tail_buf)
        zero_buf[...] = jnp.zeros_like(zero_buf)

    def strip_copy(sl, e, row0):
        return pltpu.make_async_copy(strip_buf.at[sl, e], xe_ref.at[e, pl.ds(row0, CWIN), :], sems.at[sl, e])

    pos = jnp.concatenate([pos_ref[0], pos_ref[1]], axis=1)
    rowi = lax.broadcasted_iota(jnp.int32, (CWIN, 2 * SUB), 0)
    starts = [pl.multiple_of(start_ref[e * nblk + p], 16) for e in range(N_EXPERTS)]
    onehot = jnp.concatenate(
        [jnp.where(rowi == pos[e:e + 1] - starts[e], 1.0, 0.0).astype(BF16) for e in range(N_EXPERTS)], axis=0)
    rows = _dot(onehot, x_ref[...]).astype(BF16)
    for e in range(N_EXPERTS):
        strip_buf[slot, e, 0:16, :] = rows[e * CWIN:e * CWIN + 16] + tail_buf[e]
        strip_buf[slot, e, 16:CWIN, :] = rows[e * CWIN + 16:(e + 1) * CWIN]
        t0 = next_ref[e * nblk + p] - starts[e]
        t0c = pl.multiple_of(jnp.minimum(t0, CWIN - 16), 16)
        tail_buf[e] = jnp.where(t0 < CWIN, strip_buf[slot, e, pl.ds(t0c, 16), :], jnp.zeros((16, 1), BF16))

    for e in range(N_EXPERTS):
        @pl.when(p > 0)
        def _():
            strip_copy(1 - slot, e, starts[e]).wait()
        strip_copy(slot, e, starts[e]).start()

    @pl.when(p == last)
    def _():
        for e in range(N_EXPERTS):
            strip_copy(slot, e, starts[e]).wait()
            pad = pltpu.make_async_copy(zero_buf, xe_ref.at[e, pl.ds(cap, CWIN), :], zsem)
            pad.start()
            pad.wait()


def _compact_small(u2, pos3, start, start_next, cap):
    t, d = u2.shape
    nblk = t // (2 * SUB)
    grid_spec = pltpu.PrefetchScalarGridSpec(
        num_scalar_prefetch=2,
        grid=(nblk,),
        in_specs=[pl.BlockSpec((2, N_EXPERTS, SUB), lambda p, a, b: (p, 0, 0)),
                  pl.BlockSpec((2 * SUB, d), lambda p, a, b: (p, 0))],
        out_specs=pl.BlockSpec(memory_space=pl.ANY),
        scratch_shapes=[pltpu.VMEM((2, N_EXPERTS, CWIN, d), BF16),
                        pltpu.VMEM((N_EXPERTS, 16, d), BF16),
                        pltpu.VMEM((CWIN, d), BF16),
                        pltpu.SemaphoreType.DMA((2, N_EXPERTS)),
                        pltpu.SemaphoreType.DMA(())],
    )
    return pl.pallas_call(
        functools.partial(_compact_small_kernel, nblk, cap),
        out_shape=jax.ShapeDtypeStruct((N_EXPERTS, cap + CWIN, d), BF16),
        grid_spec=grid_spec,
        compiler_params=_params(("arbitrary",)),
        name="compact_small",
    )(start, start_next, pos3, u2)


def _ffn_kernel(tm, x_ref, wg_ref, wu_ref, wd_ref, o_ref, acc_ref):
    f = pl.program_id(2)

    @pl.when(f == 0)
    def _():
        acc_ref[...] = jnp.zeros_like(acc_ref)

    half = min(512, tm)
    for r in range(tm // half):
        rows = slice(r * half, (r + 1) * half)
        xs = x_ref[0, rows, :]
        g = _dot(xs, wg_ref[0])
        u = _dot(xs, wu_ref[0])
        h = (g * jax.nn.sigmoid(g)) * u
        acc_ref[rows, :] += _dot(h.astype(BF16), wd_ref[0])

    @pl.when(f == pl.num_programs(2) - 1)
    def _():
        o_ref[0] = acc_ref[...].astype(BF16)


def _ffn(xe, cap, wg, wu, wd):
    e, _, d = xe.shape
    f = wg.shape[2]
    tm = min(1024, cap)
    tf = 1408
    return pl.pallas_call(
        functools.partial(_ffn_kernel, tm),
        out_shape=jax.ShapeDtypeStruct((e, cap, d), BF16),
        grid=(e, cap // tm, f // tf),
        in_specs=[pl.BlockSpec((1, tm, d), lambda ei, m, fi: (ei, m, 0)),
                  pl.BlockSpec((1, d, tf), lambda ei, m, fi: (ei, 0, fi)),
                  pl.BlockSpec((1, d, tf), lambda ei, m, fi: (ei, 0, fi)),
                  pl.BlockSpec((1, tf, d), lambda ei, m, fi: (ei, fi, 0))],
        out_specs=pl.BlockSpec((1, tm, d), lambda ei, m, fi: (ei, m, 0)),
        scratch_shapes=[pltpu.VMEM((tm, d), F32)],
        compiler_params=_params(("arbitrary", "arbitrary", "arbitrary")),
        name="expert_ffn",
    )(xe, wg, wu, wd)


def _uncompact_kernel(nsb, nbig, abig_ref, rel_ref, pos_ref, aff_ref, ye_ref, x1_ref, mod_ref, nf_ref,
                      out_ref, acc_ref):
    b = pl.program_id(0)
    e = pl.program_id(1)

    @pl.when(e == 0)
    def _():
        acc_ref[...] = jnp.zeros_like(acc_ref)

    rowi = lax.broadcasted_iota(jnp.int32, (WIN, SUB), 0)
    big = abig_ref[e * nbig + b]

    def body(s, carry):
        rel = pl.multiple_of(rel_ref[0, 0, s], 16)
        local = pos_ref[s, pl.ds(e, 1), :] - (big + rel)
        gate = aff_ref[s, pl.ds(e, 1), :]
        weights = jnp.where(rowi == local, gate, 0.0).astype(BF16)
        rows = pl.ds(s * SUB, SUB)
        acc_ref[rows, :] += _dot_tn(weights, ye_ref[pl.ds(rel, WIN), :])
        return carry

    for s in range(nsb):
        body(s, 0)

    @pl.when(e == pl.num_programs(1) - 1)
    def _():
        xo = x1_ref[...] + mod_ref[0][5:6] * acc_ref[...]
        ms = jnp.mean(xo * xo, axis=-1, keepdims=True)
        out_ref[...] = xo * lax.rsqrt(ms + EPS) * nf_ref[...]


def _uncompact(ye, pos3, aff3, abig, rel, x1, mod3, boff, seq_len, nf, tbig):
    t, d = x1.shape
    cap = ye.shape[1]
    nbig = t // tbig
    nsb = tbig // SUB
    wbig = tbig + 32
    grid_spec = pltpu.PrefetchScalarGridSpec(
        num_scalar_prefetch=1,
        grid=(nbig, N_EXPERTS),
        in_specs=[
            pl.BlockSpec((1, 1, nsb), lambda b, e, ab: (e * nbig + b, 0, 0), memory_space=pltpu.SMEM),
            pl.BlockSpec((nsb, N_EXPERTS, SUB), lambda b, e, ab: (b, 0, 0)),
            pl.BlockSpec((nsb, N_EXPERTS, SUB), lambda b, e, ab: (b, 0, 0)),
            pl.BlockSpec((pl.Element(wbig), pl.Element(d)),
                         lambda b, e, ab: (pl.multiple_of(e * cap + ab[e * nbig + b], 16), 0)),
            pl.BlockSpec((tbig, d), lambda b, e, ab: (b, 0)),
            pl.BlockSpec((1, 6, d), lambda b, e, ab: (boff + (b * tbig) // seq_len, 0, 0)),
            pl.BlockSpec((1, d), lambda b, e, ab: (0, 0)),
        ],
        out_specs=pl.BlockSpec((tbig, d), lambda b, e, ab: (b, 0)),
        scratch_shapes=[pltpu.VMEM((tbig, d), F32)],
    )
    return pl.pallas_call(
        functools.partial(_uncompact_kernel, nsb, nbig),
        out_shape=jax.ShapeDtypeStruct((t, d), F32),
        grid_spec=grid_spec,
        compiler_params=_params(("arbitrary", "arbitrary")),
        name="uncompact",
    )(abig, rel.reshape(N_EXPERTS * nbig, 1, nsb), pos3, aff3, ye.reshape(N_EXPERTS * cap, d), x1, mod3, nf)


def _uncompact_small_kernel(nsub, ab_ref, rel_ref, pos_ref, aff_ref, *refs):
    ye_refs = refs[:N_EXPERTS]
    x1_ref, mod_ref, nf_ref, out_ref = refs[N_EXPERTS:]
    p = pl.program_id(0)
    npair = nsub // 2
    rowi = lax.broadcasted_iota(jnp.int32, (SMALL_WIN, SUB), 0)
    ys = []
    for j in range(2):
        pos = pos_ref[j]
        aff = aff_ref[j]
        weights, windows = [], []
        for e in range(N_EXPERTS):
            rel = pl.multiple_of(rel_ref[e * nsub + 2 * p + j], 16)
            local = pos[e:e + 1] - (ab_ref[e * npair + p] + rel)
            weights.append(jnp.where(rowi == local, aff[e:e + 1], 0.0).astype(BF16))
            windows.append(ye_refs[e][pl.ds(rel, SMALL_WIN), :])
        ys.append(_dot_tn(jnp.concatenate(weights, axis=0), jnp.concatenate(windows, axis=0)))
    xo = x1_ref[...] + mod_ref[0][5:6] * jnp.concatenate(ys, axis=0)
    ms = jnp.mean(xo * xo, axis=-1, keepdims=True)
    out_ref[...] = xo * lax.rsqrt(ms + EPS) * nf_ref[...]


def _uncompact_small(ye, pos3, aff3, ab, rel, x1, mod3, boff, seq_len, nf):
    t, d = x1.shape
    cap = ye.shape[1]
    nsub = t // SUB
    npair = nsub // 2

    def window(e):
        return pl.BlockSpec((pl.Element(PAIR_WIN), pl.Element(d)),
                            lambda p, ab_ref, rel_ref: (pl.multiple_of(e * cap + ab_ref[e * npair + p], 16), 0))

    grid_spec = pltpu.PrefetchScalarGridSpec(
        num_scalar_prefetch=2,
        grid=(npair,),
        in_specs=[pl.BlockSpec((2, N_EXPERTS, SUB), lambda p, ab_ref, rel_ref: (p, 0, 0)),
                  pl.BlockSpec((2, N_EXPERTS, SUB), lambda p, ab_ref, rel_ref: (p, 0, 0))]
        + [window(e) for e in range(N_EXPERTS)]
        + [pl.BlockSpec((2 * SUB, d), lambda p, ab_ref, rel_ref: (p, 0)),
           pl.BlockSpec((1, 6, d), lambda p, ab_ref, rel_ref: (boff + (p * 2 * SUB) // seq_len, 0, 0)),
           pl.BlockSpec((1, d), lambda p, ab_ref, rel_ref: (0, 0))],
        out_specs=pl.BlockSpec((2 * SUB, d), lambda p, ab_ref, rel_ref: (p, 0)),
    )
    ye2 = ye.reshape(N_EXPERTS * cap, d)
    return pl.pallas_call(
        functools.partial(_uncompact_small_kernel, nsub),
        out_shape=jax.ShapeDtypeStruct((t, d), F32),
        grid_spec=grid_spec,
        compiler_params=_params(("arbitrary",)),
        name="uncompact_small",
    )(ab, rel, pos3, aff3, *([ye2] * N_EXPERTS), x1, mod3, nf)


def _floor16(v):
    return (v // 16) * 16


def _trunk(x, mod3, boff, w):
    b, l, d = x.shape
    t = b * l
    cap = EC_FACTOR * t // N_EXPERTS

    z = _in_proj(x, mod3, boff, w["norm1_w"], w["w_in"])
    o = _gla(z, w["wpre_f"], w["bpre_f"], False, None)
    o = _gla(z, w["wpre_b"], w["bpre_b"], True, o)
    x1, u2, aff3 = _mix_out(x, o, z, mod3, boff, w["conv_w"], w["conv_b"],
                            w["w_out_a"], w["w_out_b"], w["w_o"], w["norm2_w"], w["w_router_t"])
    pos3, base3 = _select(aff3, cap)

    base = base3[:, :, 0].T
    start = jnp.minimum(_floor16(base), cap - WIN)
    tbig_c = min(2048, t)
    tbig_u = min(1024, t)
    while tbig_u + 32 > cap:
        tbig_u //= 2
    nsb_u = tbig_u // SUB
    abig = jnp.minimum(_floor16(base[:, ::nsb_u]), cap - (tbig_u + 32))
    rel = start - jnp.repeat(abig, nsb_u, axis=1)

    base_next = jnp.concatenate([base[:, 1:], jnp.full((N_EXPERTS, 1), cap, jnp.int32)], axis=1)
    start_small = jnp.minimum(_floor16(base), cap - SMALL_WIN)
    ab_pair = jnp.minimum(_floor16(base[:, ::2]), cap - PAIR_WIN)
    rel_small = start_small - jnp.repeat(ab_pair, 2, axis=1)
    fits_small = (jnp.all(base_next - _floor16(base) <= SMALL_WIN)
                  & jnp.all(rel_small <= PAIR_WIN - SMALL_WIN))

    u2f = u2.reshape(t, d)
    start_blk = _floor16(base[:, ::2])
    next_blk = base_next[:, 1::2]
    xe = lax.cond(
        jnp.all(next_blk - start_blk <= CWIN),
        lambda: _compact_small(u2f, pos3, start_blk.reshape(-1), _floor16(next_blk).reshape(-1), cap),
        lambda: _compact(u2f, pos3, start, cap, tbig_c))
    ye = _ffn(xe, cap, w["w_gate"], w["w_up"], w["w_down"])
    x1f = x1.reshape(t, d)
    out = lax.cond(
        fits_small,
        lambda: _uncompact_small(ye, pos3, aff3, ab_pair.reshape(-1), rel_small.reshape(-1), x1f, mod3, boff,
                                 l, w["norm_f_w"]),
        lambda: _uncompact(ye, pos3, aff3, abig.reshape(-1), rel, x1f, mod3, boff, l, w["norm_f_w"], tbig_u))
    return out.reshape(b, l, d)


def _prepare_weights(w_in, w_alpha_f, b_alpha_f, w_alpha_b, b_alpha_b, gla_norm_w, conv_w, conv_b,
                     w_out_a, w_out_b, w_o, norm1_w, norm2_w, w_router, w_gate, w_up, w_down, norm_f_w):
    d = D_MODEL
    wq, wk, wv, wr, wlf, wlb, wcb, wcc, wcx, wga, wgb = jnp.split(
        w_in, np.cumsum([512, 512, 1024, 1024, 16, 16, 1024, 1024, 1024, 1024])[:].tolist(), axis=1)
    pad = jnp.zeros((d, Z_WIDTH - Z_LR - 2 * GATE_RANK), w_in.dtype)
    wq = wq * (HEAD_K ** -0.5)
    w_in_p = jnp.concatenate([wq, wk, wv, wr, wcb, wcc, wcx, wga, wgb, wlf, wlb, pad], axis=1).astype(BF16)

    def pre(wa, ba, row0):
        wh = wa.reshape(GATE_RANK, GLA_HEADS, HEAD_K).transpose(1, 0, 2)
        full = jnp.zeros((GLA_HEADS, 128, HEAD_K), F32).at[:, row0:row0 + GATE_RANK].set(wh)
        return full.astype(BF16), ba.reshape(GLA_HEADS, 1, HEAD_K)

    wpre_f, bpre_f = pre(w_alpha_f, b_alpha_f, 0)
    wpre_b, bpre_b = pre(w_alpha_b, b_alpha_b, GATE_RANK)
    return dict(
        w_in=w_in_p, wpre_f=wpre_f, bpre_f=bpre_f, wpre_b=wpre_b, bpre_b=bpre_b,
        conv_w=conv_w, conv_b=conv_b.reshape(1, d),
        w_out_a=(gla_norm_w.reshape(d, 1) * w_out_a).astype(BF16),
        w_out_b=w_out_b.astype(BF16), w_o=w_o.astype(BF16),
        norm1_w=norm1_w, norm2_w=norm2_w.reshape(1, d), w_router_t=w_router.T.astype(BF16),
        w_gate=w_gate.astype(BF16), w_up=w_up.astype(BF16), w_down=w_down.astype(BF16),
        norm_f_w=norm_f_w.reshape(1, d))


def kernel(x_prompt, x_sample, c_prompt, c_sample, w_ada, b_ada, norm1_w, w_in, w_alpha_f, b_alpha_f, w_alpha_b, b_alpha_b, gla_norm_w, conv_w, conv_b, w_out_a, w_out_b, w_o, norm2_w, w_router, w_gate, w_up, w_down, norm_f_w):
    d = D_MODEL
    bp, bs = c_prompt.shape[0], c_sample.shape[0]
    nrows = -(-(bp + bs) // 8) * 8
    c_all = jnp.concatenate([c_prompt, c_sample, jnp.zeros((nrows - bp - bs, d), F32)], axis=0)
    mod3 = _adaln(c_all, w_ada[0], b_ada[0]).reshape(nrows, 6, d)
    w = _prepare_weights(w_in[0], w_alpha_f[0], b_alpha_f[0], w_alpha_b[0], b_alpha_b[0], gla_norm_w[0],
                         conv_w[0], conv_b[0], w_out_a[0], w_out_b[0], w_o[0], norm1_w[0], norm2_w[0],
                         w_router[0], w_gate[0], w_up[0], w_down[0], norm_f_w)
    y_prompt = _trunk(x_prompt, mod3, 0, w)
    y_sample = _trunk(x_sample, mod3, bp, w)
    return (y_prompt, y_sample)
```

```python
import functools

import numpy as np
import jax
import jax.numpy as jnp
from jax import lax
from jax.experimental import pallas as pl
from jax.experimental.pallas import tpu as pltpu

F32 = jnp.float32
BF16 = jnp.bfloat16

D_MODEL = 1024
GLA_HEADS = 4
HEAD_K = 128
HEAD_V = 256
GATE_RANK = 16
GATE_TAU = 16.0
CHUNK = 64
N_EXPERTS = 16
EC_FACTOR = 2
D_EXPERT = 2816
EPS = 1e-6
LOG2E = 1.4426950408889634

Z_Q, Z_K, Z_V, Z_R, Z_CB, Z_CC, Z_CX, Z_GA, Z_GB, Z_LR = 0, 512, 1024, 2048, 3072, 4096, 5120, 6144, 7168, 8192
Z_WIDTH = 8448

UNIT = 4 * CHUNK
SUB = 128
WIN = SUB + 16
SMALL_WIN = 64
PAIR_WIN = 112
CWIN = 96
FFN_TM = 1024
VMEM_LIMIT = 56 * 1024 * 1024


def _dot(a, b):
    return jnp.dot(a, b, preferred_element_type=F32)


def _dot_nt(a, b):
    return lax.dot_general(a, b, (((1,), (1,)), ((), ())), preferred_element_type=F32)


def _sigmoid(x):
    return 0.5 * jnp.tanh(0.5 * x) + 0.5


def _dot_tn(a, b):
    return lax.dot_general(a, b, (((0,), (0,)), ((), ())), preferred_element_type=F32)


def _params(sem, limit=VMEM_LIMIT):
    return pltpu.CompilerParams(dimension_semantics=sem, vmem_limit_bytes=limit)


def _adaln_kernel(c_ref, w_ref, b_ref, o_ref):
    c = c_ref[...]
    s = c * jax.nn.sigmoid(c)
    o_ref[...] = jnp.dot(s, w_ref[...], preferred_element_type=F32,
                         precision=lax.Precision.HIGHEST) + b_ref[...]


def _adaln(c_all, w_ada, b_ada):
    nb, d = c_all.shape
    n = w_ada.shape[1]
    tn = 1536
    return pl.pallas_call(
        _adaln_kernel,
        out_shape=jax.ShapeDtypeStruct((nb, n), F32),
        grid=(n // tn,),
        in_specs=[pl.BlockSpec((nb, d), lambda j: (0, 0)),
                  pl.BlockSpec((d, tn), lambda j: (0, j)),
                  pl.BlockSpec((1, tn), lambda j: (0, j))],
        out_specs=pl.BlockSpec((nb, tn), lambda j: (0, j)),
        compiler_params=_params(("arbitrary",)),
        name="adaln",
    )(c_all, w_ada, b_ada.reshape(1, n))


def _inproj_kernel(x_ref, mod_ref, n1_ref, w_ref, z_ref):
    x = x_ref[0]
    ms = jnp.mean(x * x, axis=-1, keepdims=True)
    m = mod_ref[0]
    u = (x * lax.rsqrt(ms + EPS)) * (n1_ref[...] * (1.0 + m[1:2])) + m[0:1]
    z_ref[0] = _dot(u.astype(BF16), w_ref[...]).astype(BF16)


def _in_proj(x, mod3, boff, norm1_w, w_in_p):
    b, l, d = x.shape
    tm = min(1024, l)
    tn = 2816
    return pl.pallas_call(
        _inproj_kernel,
        out_shape=jax.ShapeDtypeStruct((b, l, Z_WIDTH), BF16),
        grid=(b, l // tm, Z_WIDTH // tn),
        in_specs=[pl.BlockSpec((1, tm, d), lambda bi, i, n: (bi, i, 0)),
                  pl.BlockSpec((1, 6, d), lambda bi, i, n: (bi + boff, 0, 0)),
                  pl.BlockSpec((1, d), lambda bi, i, n: (0, 0)),
                  pl.BlockSpec((d, tn), lambda bi, i, n: (0, n))],
        out_specs=pl.BlockSpec((1, tm, tn), lambda bi, i, n: (bi, i, n)),
        compiler_params=_params(("arbitrary", "arbitrary", "arbitrary")),
        name="in_proj",
    )(x, mod3, norm1_w.reshape(1, d), w_in_p)


def _gla_constants(reverse):
    r = np.arange(UNIT)[:, None]
    s = np.arange(UNIT)[None, :]
    cr, cs = r // CHUNK, s // CHUNK
    if reverse:
        tri = (cr == cs) & (s >= r)
        tr, ts = 3 - cr, 3 - cs
    else:
        tri = (cr == cs) & (s <= r)
        tr, ts = cr, cs
    code = np.zeros((UNIT, UNIT), np.int32)
    code[tri] = 1
    return jnp.asarray(tri, BF16), jnp.asarray(code)


def _gla_kernel(reverse, nunits, has_prev, *refs):
    if has_prev:
        q_ref, k_ref, v_ref, lr_ref, wp_ref, bp_ref, tri_ref, code_ref, prev_ref, o_ref, s_scr = refs
    else:
        q_ref, k_ref, v_ref, lr_ref, wp_ref, bp_ref, tri_ref, code_ref, o_ref, s_scr = refs
        prev_ref = None

    @pl.when(pl.program_id(2) == 0)
    def _():
        s_scr[...] = jnp.zeros_like(s_scr)

    order = (3, 2, 1, 0) if reverse else (0, 1, 2, 3)

    def to_mem(by_step):
        out = [None] * 4
        for t in range(4):
            out[order[t]] = by_step[t]
        return out

    def scale_chunks(arr, logf):
        parts = []
        for c in range(4):
            part = arr[c * CHUNK:(c + 1) * CHUNK]
            if logf[c] is not None:
                part = part * jnp.exp2(logf[c])
            parts.append(part)
        return jnp.concatenate(parts, axis=0).astype(BF16)

    def rows_of(ui):
        um = (nunits - 1 - ui) if reverse else ui
        return pl.ds(um * UNIT, UNIT)

    def decays(ui):
        pre = _dot(lr_ref[0, rows_of(ui), :], wp_ref[0]) + bp_ref[0]
        soft = jnp.log2(1.0 + jnp.exp2(jnp.abs(pre) * (-LOG2E)))
        la = (jnp.minimum(pre, 0.0) * LOG2E - soft) * (1.0 / GATE_TAU)
        hi = la.astype(BF16)
        lo = (la - hi.astype(F32)).astype(BF16)
        bb = _dot(tri_ref[...], jnp.concatenate([hi, lo], axis=1))
        return bb[:, :HEAD_K] + bb[:, HEAD_K:]

    def operands(ui, b):
        rows = rows_of(ui)
        edge = 0 if reverse else CHUNK - 1
        tot = [b[c * CHUNK + edge:c * CHUNK + edge + 1] for c in range(4)]
        tot_full = jnp.concatenate([jnp.broadcast_to(t, (CHUNK, HEAD_K)) for t in tot], axis=0)
        q = q_ref[0, rows, :].astype(F32)
        k = k_ref[0, rows, :].astype(F32)
        q_t = q * jnp.exp2(b)
        k_t = (k * jnp.exp2(-b)).astype(BF16)
        k_e = k * jnp.exp2(tot_full - b)
        q_tb = q_t.astype(BF16)
        k_eb = k_e.astype(BF16)
        bt = [tot[o] for o in order]

        zero = jnp.zeros((CHUNK, HEAD_K), BF16)
        e1 = jnp.exp2(bt[1])
        e2 = jnp.exp2(bt[2])
        qrows, krows = [None] * 4, [None] * 4
        for t in range(4):
            c = order[t]
            sl = slice(c * CHUNK, (c + 1) * CHUNK)
            qc, kc = q_tb[sl], k_eb[sl]
            q3 = qc if t == 2 else ((q_t[sl] * e2).astype(BF16) if t == 3 else zero)
            k3 = (k_e[sl] * e1).astype(BF16) if t == 0 else (kc if t == 1 else zero)
            qrows[c] = jnp.concatenate([qc if t == 1 else zero, qc if t == 3 else zero, q3], axis=1)
            krows[c] = jnp.concatenate([kc if t == 0 else zero, kc if t == 2 else zero, k3], axis=1)

        gq = to_mem([None, bt[0], bt[0] + bt[1], bt[0] + bt[1] + bt[2]])
        gk = to_mem([bt[1] + bt[2] + bt[3], bt[2] + bt[3], bt[3], None])
        dec = jnp.exp2(bt[0] + bt[1] + bt[2] + bt[3])
        return dict(q_t=q_tb, k_t=k_t, q_cat=jnp.concatenate(qrows, axis=0), k_cat=jnp.concatenate(krows, axis=0),
                    qs=scale_chunks(q_t, gq), ks=scale_chunks(k_e, gk), dec=dec)

    def scores(p):
        same = _dot_nt(p["q_t"], p["k_t"])
        cross = _dot_nt(p["q_cat"], p["k_cat"])
        return jnp.where(code_ref[...] == 1, same, cross).astype(BF16)

    def finish(ui, p, intra):
        rows = rows_of(ui)
        st = s_scr[...]
        o = intra + _dot_nt(p["qs"], st.astype(BF16))
        s_scr[...] = st * p["dec"] + _dot_tn(v_ref[0, rows, :], p["ks"])
        if prev_ref is not None:
            o = o + prev_ref[0, rows, :]
        o_ref[0, rows, :] = o

    stages = 5
    bs, ps, atts, intra = {}, {}, {}, {}
    for wave in range(nunits + stages - 1):
        for k in range(stages):
            ui = wave - k
            if not 0 <= ui < nunits:
                continue
            if k == 0:
                bs[ui] = decays(ui)
            elif k == 1:
                ps[ui] = operands(ui, bs.pop(ui))
            elif k == 2:
                atts[ui] = scores(ps[ui])
            elif k == 3:
                intra[ui] = _dot(atts.pop(ui), v_ref[0, rows_of(ui), :])
            else:
                finish(ui, ps.pop(ui), intra.pop(ui))


def _gla(z, wpre, bpre, reverse, prev):
    b, l, _ = z.shape
    lb = min(2048, l)
    nb = l // lb
    nunits = lb // UNIT
    tri, code = _gla_constants(reverse)

    def seq(i):
        return (nb - 1 - i) if reverse else i

    in_specs = [
        pl.BlockSpec((1, lb, HEAD_K), lambda bi, h, i: (bi, seq(i), Z_Q // HEAD_K + h)),
        pl.BlockSpec((1, lb, HEAD_K), lambda bi, h, i: (bi, seq(i), Z_K // HEAD_K + h)),
        pl.BlockSpec((1, lb, HEAD_V), lambda bi, h, i: (bi, seq(i), Z_V // HEAD_V + h)),
        pl.BlockSpec((1, lb, 128), lambda bi, h, i: (bi, seq(i), Z_LR // 128)),
        pl.BlockSpec((1, 128, HEAD_K), lambda bi, h, i: (h, 0, 0)),
        pl.BlockSpec((1, 1, HEAD_K), lambda bi, h, i: (h, 0, 0)),
        pl.BlockSpec((UNIT, UNIT), lambda bi, h, i: (0, 0)),
        pl.BlockSpec((UNIT, UNIT), lambda bi, h, i: (0, 0)),
    ]
    args = [z, z, z, z, wpre, bpre, tri, code]
    if prev is not None:
        in_specs.append(pl.BlockSpec((1, lb, HEAD_V), lambda bi, h, i: (bi, seq(i), h)))
        args.append(prev)
    return pl.pallas_call(
        functools.partial(_gla_kernel, reverse, nunits, prev is not None),
        out_shape=jax.ShapeDtypeStruct((b, l, GLA_HEADS * HEAD_V), F32),
        grid=(b, GLA_HEADS, nb),
        in_specs=in_specs,
        out_specs=pl.BlockSpec((1, lb, HEAD_V), lambda bi, h, i: (bi, seq(i), h)),
        scratch_shapes=[pltpu.VMEM((HEAD_V, HEAD_K), F32)],
        compiler_params=_params(("arbitrary", "arbitrary", "arbitrary")),
        name="gla_bwd" if reverse else "gla_fwd",
    )(*args)


def _mix_kernel(tm, x_ref, o_ref, r_ref, cb_ref, cc_ref, cx_ref, ccp_ref, cxp_ref, ccn_ref, cxn_ref,
                ga_ref, gb_ref, mod_ref, cw_ref, cbias_ref, woa_ref, wob_ref, wo_ref,
                n2_ref, wrt_ref, x1_ref, u2_ref, aff_ref):
    i = pl.program_id(1)
    last = pl.num_programs(1) - 1

    o = o_ref[0]
    parts = []
    for h in range(GLA_HEADS):
        oh = o[:, h * HEAD_V:(h + 1) * HEAD_V]
        ms = jnp.mean(oh * oh, axis=-1, keepdims=True)
        parts.append(oh * lax.rsqrt(ms + EPS))
    on = jnp.concatenate(parts, axis=1)
    r = r_ref[0]
    branch_a = _dot(on.astype(BF16) * (r * _sigmoid(r)), woa_ref[...])

    xc = cc_ref[0].astype(F32) * cx_ref[0].astype(F32)
    prev_row = ccp_ref[0, 15:16, :].astype(F32) * cxp_ref[0, 15:16, :].astype(F32)
    next_row = ccn_ref[0, 0:1, :].astype(F32) * cxn_ref[0, 0:1, :].astype(F32)
    prev_row = jnp.where(i > 0, prev_row, 0.0)
    next_row = jnp.where(i < last, next_row, 0.0)
    rowid = lax.broadcasted_iota(jnp.int32, (tm, 1), 0)
    xm1 = jnp.where(rowid == 0, prev_row, pltpu.roll(xc, 1, axis=0))
    xp1 = jnp.where(rowid == tm - 1, next_row, pltpu.roll(xc, tm - 1, axis=0))
    cw = cw_ref[...]
    conv = xm1 * cw[0:1] + xc * cw[1:2] + xp1 * cw[2:3] + cbias_ref[...]
    branch_b = _dot((cb_ref[0].astype(F32) * conv).astype(BF16), wob_ref[...])

    merged = (_sigmoid(ga_ref[0]) * branch_a.astype(BF16)
              + _sigmoid(gb_ref[0]) * branch_b.astype(BF16))
    mix = _dot(merged, wo_ref[...])

    m = mod_ref[0]
    x1 = x_ref[0] + m[2:3] * mix
    x1_ref[0] = x1
    ms = jnp.mean(x1 * x1, axis=-1, keepdims=True)
    u2 = (x1 * lax.rsqrt(ms + EPS)) * (n2_ref[...] * (1.0 + m[4:5])) + m[3:4]
    u2b = u2.astype(BF16)
    u2_ref[0] = u2b

    logits = _dot_nt(wrt_ref[...], u2b)
    ex = jnp.exp(logits - jnp.max(logits, axis=0, keepdims=True))
    aff = ex / jnp.sum(ex, axis=0, keepdims=True)
    for j in range(tm // SUB):
        aff_ref[j] = aff[:, j * SUB:(j + 1) * SUB]


def _mix_out(x, o, z, mod3, boff, conv_w, conv_b, woa, wob, wo, n2, wrt):
    b, l, d = x.shape
    tm = min(512, l)
    nt = l // tm
    hb = tm // 16
    nhalo = l // 16

    def col(c):
        return lambda bi, i: (bi, i, c // d)

    in_specs = [
        pl.BlockSpec((1, tm, d), lambda bi, i: (bi, i, 0)),
        pl.BlockSpec((1, tm, d), lambda bi, i: (bi, i, 0)),
        pl.BlockSpec((1, tm, d), col(Z_R)),
        pl.BlockSpec((1, tm, d), col(Z_CB)),
        pl.BlockSpec((1, tm, d), col(Z_CC)),
        pl.BlockSpec((1, tm, d), col(Z_CX)),
        pl.BlockSpec((1, 16, d), lambda bi, i: (bi, jnp.maximum(i * hb - 1, 0), Z_CC // d)),
        pl.BlockSpec((1, 16, d), lambda bi, i: (bi, jnp.maximum(i * hb - 1, 0), Z_CX // d)),
        pl.BlockSpec((1, 16, d), lambda bi, i: (bi, jnp.minimum((i + 1) * hb, nhalo - 1), Z_CC // d)),
        pl.BlockSpec((1, 16, d), lambda bi, i: (bi, jnp.minimum((i + 1) * hb, nhalo - 1), Z_CX // d)),
        pl.BlockSpec((1, tm, d), col(Z_GA)),
        pl.BlockSpec((1, tm, d), col(Z_GB)),
        pl.BlockSpec((1, 6, d), lambda bi, i: (bi + boff, 0, 0)),
        pl.BlockSpec((3, d), lambda bi, i: (0, 0)),
        pl.BlockSpec((1, d), lambda bi, i: (0, 0)),
        pl.BlockSpec((d, d), lambda bi, i: (0, 0)),
        pl.BlockSpec((d, d), lambda bi, i: (0, 0)),
        pl.BlockSpec((d, d), lambda bi, i: (0, 0)),
        pl.BlockSpec((1, d), lambda bi, i: (0, 0)),
        pl.BlockSpec((N_EXPERTS, d), lambda bi, i: (0, 0)),
    ]
    out_shape = (jax.ShapeDtypeStruct((b, l, d), F32),
                 jax.ShapeDtypeStruct((b, l, d), BF16),
                 jax.ShapeDtypeStruct((b * l // SUB, N_EXPERTS, SUB), F32))
    out_specs = (pl.BlockSpec((1, tm, d), lambda bi, i: (bi, i, 0)),
                 pl.BlockSpec((1, tm, d), lambda bi, i: (bi, i, 0)),
                 pl.BlockSpec((tm // SUB, N_EXPERTS, SUB), lambda bi, i: (bi * nt + i, 0, 0)))
    return pl.pallas_call(
        functools.partial(_mix_kernel, tm),
        out_shape=out_shape,
        grid=(b, nt),
        in_specs=in_specs,
        out_specs=out_specs,
        compiler_params=_params(("arbitrary", "arbitrary")),
        name="mix_out",
    )(x, o, z, z, z, z, z, z, z, z, z, z, mod3, conv_w, conv_b, woa, wob, wo, n2, wrt)


def _select_kernel(cap, nsb, aff_ref, u_ref, pos_ref, base_ref):
    capf = float(cap)

    def bits(j):
        return pltpu.bitcast(aff_ref[j], jnp.int32)

    def count(pred):
        def body(j, acc):
            return acc + jnp.where(pred(bits(j)), 1.0, 0.0)
        acc = lax.fori_loop(0, nsb, body, jnp.zeros((N_EXPERTS, SUB), F32), unroll=8)
        return jnp.sum(acc, axis=1, keepdims=True)

    def bit_body(t, thr):
        cand = thr | jnp.left_shift(jnp.int32(1), 30 - t)
        return jnp.where(count(lambda v: v >= cand) >= capf, cand, thr)

    thr = lax.fori_loop(0, 31, bit_body, jnp.zeros((N_EXPERTS, SUB), jnp.int32))
    need = capf - count(lambda v: v > thr)

    def blk(j, carry):
        ceq, csel = carry
        v = bits(j)
        gtf = jnp.where(v > thr, 1.0, 0.0)
        eqf = jnp.where(v == thr, 1.0, 0.0)
        r1 = _dot(eqf.astype(BF16), u_ref[...])
        eq_rank = r1[:, :SUB] - eqf + ceq
        sel = gtf + jnp.where(eq_rank < need, eqf, 0.0)
        r2 = _dot(sel.astype(BF16), u_ref[...])
        pos_ref[j] = jnp.where(sel > 0.5, r2[:, :SUB] + csel - 1.0, -1.0).astype(jnp.int32)
        base_ref[j] = csel.astype(jnp.int32)
        return ceq + r1[:, SUB:], csel + r2[:, SUB:]

    zero = jnp.zeros((N_EXPERTS, SUB), F32)
    lax.fori_loop(0, nsb, blk, (zero, zero), unroll=8)


def _select(aff3, cap):
    nsb = aff3.shape[0]
    tri = np.triu(np.ones((SUB, SUB), np.float32))
    umat = jnp.asarray(np.concatenate([tri, np.ones((SUB, SUB), np.float32)], axis=1), BF16)
    shp = jax.ShapeDtypeStruct((nsb, N_EXPERTS, SUB), jnp.int32)
    return pl.pallas_call(
        functools.partial(_select_kernel, cap, nsb),
        out_shape=(shp, shp),
        compiler_params=pltpu.CompilerParams(vmem_limit_bytes=VMEM_LIMIT),
        name="select",
    )(aff3, umat)


def _compact_kernel(nsb, start_ref, pos_ref, x_ref, xe_ref):
    e = pl.program_id(0)

    @pl.when(pl.program_id(1) == 0)
    def _():
        xe_ref[...] = jnp.zeros_like(xe_ref)

    rowi = lax.broadcasted_iota(jnp.int32, (WIN, SUB), 0)

    def body(s, carry):
        st = pl.multiple_of(start_ref[0, 0, s], 16)
        local = pos_ref[s, pl.ds(e, 1), :] - st
        onehot = jnp.where(rowi == local, 1.0, 0.0).astype(BF16)
        xs = x_ref[pl.ds(s * SUB, SUB), :]
        rows = _dot(onehot, xs).astype(BF16)
        xe_ref[0, pl.ds(st, WIN), :] = xe_ref[0, pl.ds(st, WIN), :] + rows
        return carry

    for s in range(nsb):
        body(s, 0)


def _compact(u2, pos3, start, cap, tbig):
    t, d = u2.shape
    nbig = t // tbig
    nsb = tbig // SUB
    return pl.pallas_call(
        functools.partial(_compact_kernel, nsb),
        out_shape=jax.ShapeDtypeStruct((N_EXPERTS, cap + CWIN, d), BF16),
        grid=(N_EXPERTS, nbig),
        in_specs=[pl.BlockSpec((1, 1, nsb), lambda e, b: (e * nbig + b, 0, 0), memory_space=pltpu.SMEM),
                  pl.BlockSpec((nsb, N_EXPERTS, SUB), lambda e, b: (b, 0, 0)),
                  pl.BlockSpec((tbig, d), lambda e, b: (b, 0))],
        out_specs=pl.BlockSpec((1, cap + CWIN, d), lambda e, b: (e, 0, 0)),
        compiler_params=_params(("arbitrary", "arbitrary")),
        name="compact",
    )(start.reshape(N_EXPERTS * nbig, 1, nsb), pos3, u2)


def _compact_small_kernel(nblk, cap, start_ref, next_ref, pos_ref, x_ref, xe_ref, strip_buf, tail_buf, zero_buf,
                          sems, zsem):
    p = pl.program_id(0)
    last = pl.num_programs(0) - 1
    slot = lax.rem(p, 2)

    @pl.when(p == 0)
    def _():
        tail_buf[...] = jnp.zeros_like(tail_buf)
        zero_buf[...] = jnp.zeros_like(zero_buf)

    def strip_copy(sl, e, row0):
        return pltpu.make_async_copy(strip_buf.at[sl, e], xe_ref.at[e, pl.ds(row0, CWIN), :], sems.at[sl, e])

    pos = jnp.concatenate([pos_ref[0], pos_ref[1]], axis=1)
    rowi = lax.broadcasted_iota(jnp.int32, (CWIN, 2 * SUB), 0)
    starts = [pl.multiple_of(start_ref[e * nblk + p], 16) for e in range(N_EXPERTS)]
    onehot = jnp.concatenate(
        [jnp.where(rowi == pos[e:e + 1] - starts[e], 1.0, 0.0).astype(BF16) for e in range(N_EXPERTS)], axis=0)
    rows = _dot(onehot, x_ref[...]).astype(BF16)
    for e in range(N_EXPERTS):
        strip_buf[slot, e, 0:16, :] = rows[e * CWIN:e * CWIN + 16] + tail_buf[e]
        strip_buf[slot, e, 16:CWIN, :] = rows[e * CWIN + 16:(e + 1) * CWIN]
        t0 = next_ref[e * nblk + p] - starts[e]
        t0c = pl.multiple_of(jnp.minimum(t0, CWIN - 16), 16)
        tail_buf[e] = jnp.where(t0 < CWIN, strip_buf[slot, e, pl.ds(t0c, 16), :], jnp.zeros((16, 1), BF16))

    for e in range(N_EXPERTS):
        @pl.when(p > 0)
        def _():
            strip_copy(1 - slot, e, starts[e]).wait()
        strip_copy(slot, e, starts[e]).start()

    @pl.when(p == last)
    def _():
        for e in range(N_EXPERTS):
            strip_copy(slot, e, starts[e]).wait()
            pad = pltpu.make_async_copy(zero_buf, xe_ref.at[e, pl.ds(cap, CWIN), :], zsem)
            pad.start()
            pad.wait()


def _compact_small(u2, pos3, start, start_next, cap):
    t, d = u2.shape
    nblk = t // (2 * SUB)
    grid_spec = pltpu.PrefetchScalarGridSpec(
        num_scalar_prefetch=2,
        grid=(nblk,),
        in_specs=[pl.BlockSpec((2, N_EXPERTS, SUB), lambda p, a, b: (p, 0, 0)),
                  pl.BlockSpec((2 * SUB, d), lambda p, a, b: (p, 0))],
        out_specs=pl.BlockSpec(memory_space=pl.ANY),
        scratch_shapes=[pltpu.VMEM((2, N_EXPERTS, CWIN, d), BF16),
                        pltpu.VMEM((N_EXPERTS, 16, d), BF16),
                        pltpu.VMEM((CWIN, d), BF16),
                        pltpu.SemaphoreType.DMA((2, N_EXPERTS)),
                        pltpu.SemaphoreType.DMA(())],
    )
    return pl.pallas_call(
        functools.partial(_compact_small_kernel, nblk, cap),
        out_shape=jax.ShapeDtypeStruct((N_EXPERTS, cap + CWIN, d), BF16),
        grid_spec=grid_spec,
        compiler_params=_params(("arbitrary",)),
        name="compact_small",
    )(start, start_next, pos3, u2)


def _ffn_kernel(chunks, x_ref, wg_ref, wu_ref, wd_ref, o_ref, acc_ref):
    xs = x_ref[0]
    for i, (c0, cw) in enumerate(chunks):
        g = _dot(xs, wg_ref[0, :, c0:c0 + cw])
        u = _dot(xs, wu_ref[0, :, c0:c0 + cw])
        h = ((g * jax.nn.sigmoid(g)) * u).astype(BF16)
        part = _dot(h, wd_ref[0, c0:c0 + cw, :])
        if i == 0:
            acc_ref[...] = part
        else:
            acc_ref[...] += part
    o_ref[0] = acc_ref[...].astype(BF16)


def _ffn(xe, cap, wg, wu, wd):
    e, _, d = xe.shape
    f = wg.shape[2]
    tm = min(FFN_TM, cap)
    chunks = tuple((c0, min(512, f - c0)) for c0 in range(0, f, 512))
    return pl.pallas_call(
        functools.partial(_ffn_kernel, chunks),
        out_shape=jax.ShapeDtypeStruct((e, cap, d), BF16),
        grid=(e, cap // tm),
        in_specs=[pl.BlockSpec((1, tm, d), lambda ei, m: (ei, m, 0)),
                  pl.BlockSpec((1, d, f), lambda ei, m: (ei, 0, 0)),
                  pl.BlockSpec((1, d, f), lambda ei, m: (ei, 0, 0)),
                  pl.BlockSpec((1, f, d), lambda ei, m: (ei, 0, 0))],
        out_specs=pl.BlockSpec((1, tm, d), lambda ei, m: (ei, m, 0)),
        scratch_shapes=[pltpu.VMEM((tm, d), F32)],
        compiler_params=_params(("arbitrary", "arbitrary")),
        name="expert_ffn",
    )(xe, wg, wu, wd)


def _uncompact_kernel(nsb, nbig, abig_ref, rel_ref, pos_ref, aff_ref, ye_ref, x1_ref, mod_ref, nf_ref,
                      out_ref, acc_ref):
    b = pl.program_id(0)
    e = pl.program_id(1)

    @pl.when(e == 0)
    def _():
        acc_ref[...] = jnp.zeros_like(acc_ref)

    rowi = lax.broadcasted_iota(jnp.int32, (WIN, SUB), 0)
    big = abig_ref[e * nbig + b]

    def body(s, carry):
        rel = pl.multiple_of(rel_ref[0, 0, s], 16)
        local = pos_ref[s, pl.ds(e, 1), :] - (big + rel)
        gate = aff_ref[s, pl.ds(e, 1), :]
        weights = jnp.where(rowi == local, gate, 0.0).astype(BF16)
        rows = pl.ds(s * SUB, SUB)
        acc_ref[rows, :] += _dot_tn(weights, ye_ref[pl.ds(rel, WIN), :])
        return carry

    for s in range(nsb):
        body(s, 0)

    @pl.when(e == pl.num_programs(1) - 1)
    def _():
        xo = x1_ref[...] + mod_ref[0][5:6] * acc_ref[...]
        ms = jnp.mean(xo * xo, axis=-1, keepdims=True)
        out_ref[...] = xo * lax.rsqrt(ms + EPS) * nf_ref[...]


def _uncompact(ye, pos3, aff3, abig, rel, x1, mod3, boff, seq_len, nf, tbig):
    t, d = x1.shape
    cap = ye.shape[1]
    nbig = t // tbig
    nsb = tbig // SUB
    wbig = tbig + 32
    grid_spec = pltpu.PrefetchScalarGridSpec(
        num_scalar_prefetch=1,
        grid=(nbig, N_EXPERTS),
        in_specs=[
            pl.BlockSpec((1, 1, nsb), lambda b, e, ab: (e * nbig + b, 0, 0), memory_space=pltpu.SMEM),
            pl.BlockSpec((nsb, N_EXPERTS, SUB), lambda b, e, ab: (b, 0, 0)),
            pl.BlockSpec((nsb, N_EXPERTS, SUB), lambda b, e, ab: (b, 0, 0)),
            pl.BlockSpec((pl.Element(wbig), pl.Element(d)),
                         lambda b, e, ab: (pl.multiple_of(e * cap + ab[e * nbig + b], 16), 0)),
            pl.BlockSpec((tbig, d), lambda b, e, ab: (b, 0)),
            pl.BlockSpec((1, 6, d), lambda b, e, ab: (boff + (b * tbig) // seq_len, 0, 0)),
            pl.BlockSpec((1, d), lambda b, e, ab: (0, 0)),
        ],
        out_specs=pl.BlockSpec((tbig, d), lambda b, e, ab: (b, 0)),
        scratch_shapes=[pltpu.VMEM((tbig, d), F32)],
    )
    return pl.pallas_call(
        functools.partial(_uncompact_kernel, nsb, nbig),
        out_shape=jax.ShapeDtypeStruct((t, d), F32),
        grid_spec=grid_spec,
        compiler_params=_params(("arbitrary", "arbitrary")),
        name="uncompact",
    )(abig, rel.reshape(N_EXPERTS * nbig, 1, nsb), pos3, aff3, ye.reshape(N_EXPERTS * cap, d), x1, mod3, nf)


def _uncompact_small_kernel(nsub, ab_ref, rel_ref, pos_ref, aff_ref, *refs):
    ye_refs = refs[:N_EXPERTS]
    x1_ref, mod_ref, nf_ref, out_ref = refs[N_EXPERTS:]
    p = pl.program_id(0)
    npair = nsub // 2
    rowi = lax.broadcasted_iota(jnp.int32, (SMALL_WIN, SUB), 0)
    ys = []
    for j in range(2):
        pos = pos_ref[j]
        aff = aff_ref[j]
        weights, windows = [], []
        for e in range(N_EXPERTS):
            rel = pl.multiple_of(rel_ref[e * nsub + 2 * p + j], 16)
            local = pos[e:e + 1] - (ab_ref[e * npair + p] + rel)
            weights.append(jnp.where(rowi == local, aff[e:e + 1], 0.0).astype(BF16))
            windows.append(ye_refs[e][pl.ds(rel, SMALL_WIN), :])
        ys.append(_dot_tn(jnp.concatenate(weights, axis=0), jnp.concatenate(windows, axis=0)))
    xo = x1_ref[...] + mod_ref[0][5:6] * jnp.concatenate(ys, axis=0)
    ms = jnp.mean(xo * xo, axis=-1, keepdims=True)
    out_ref[...] = xo * lax.rsqrt(ms + EPS) * nf_ref[...]


def _uncompact_small(ye, pos3, aff3, ab, rel, x1, mod3, boff, seq_len, nf):
    t, d = x1.shape
    cap = ye.shape[1]
    nsub = t // SUB
    npair = nsub // 2

    def window(e):
        return pl.BlockSpec((pl.Element(PAIR_WIN), pl.Element(d)),
                            lambda p, ab_ref, rel_ref: (pl.multiple_of(e * cap + ab_ref[e * npair + p], 16), 0))

    grid_spec = pltpu.PrefetchScalarGridSpec(
        num_scalar_prefetch=2,
        grid=(npair,),
        in_specs=[pl.BlockSpec((2, N_EXPERTS, SUB), lambda p, ab_ref, rel_ref: (p, 0, 0)),
                  pl.BlockSpec((2, N_EXPERTS, SUB), lambda p, ab_ref, rel_ref: (p, 0, 0))]
        + [window(e) for e in range(N_EXPERTS)]
        + [pl.BlockSpec((2 * SUB, d), lambda p, ab_ref, rel_ref: (p, 0)),
           pl.BlockSpec((1, 6, d), lambda p, ab_ref, rel_ref: (boff + (p * 2 * SUB) // seq_len, 0, 0)),
           pl.BlockSpec((1, d), lambda p, ab_ref, rel_ref: (0, 0))],
        out_specs=pl.BlockSpec((2 * SUB, d), lambda p, ab_ref, rel_ref: (p, 0)),
    )
    ye2 = ye.reshape(N_EXPERTS * cap, d)
    return pl.pallas_call(
        functools.partial(_uncompact_small_kernel, nsub),
        out_shape=jax.ShapeDtypeStruct((t, d), F32),
        grid_spec=grid_spec,
        compiler_params=_params(("arbitrary",)),
        name="uncompact_small",
    )(ab, rel, pos3, aff3, *([ye2] * N_EXPERTS), x1, mod3, nf)


def _floor16(v):
    return (v // 16) * 16


def _trunk(x, mod3, boff, w):
    b, l, d = x.shape
    t = b * l
    cap = EC_FACTOR * t // N_EXPERTS

    z = _in_proj(x, mod3, boff, w["norm1_w"], w["w_in"])
    o = _gla(z, w["wpre_f"], w["bpre_f"], False, None)
    o = _gla(z, w["wpre_b"], w["bpre_b"], True, o)
    x1, u2, aff3 = _mix_out(x, o, z, mod3, boff, w["conv_w"], w["conv_b"],
                            w["w_out_a"], w["w_out_b"], w["w_o"], w["norm2_w"], w["w_router_t"])
    pos3, base3 = _select(aff3, cap)

    base = base3[:, :, 0].T
    start = jnp.minimum(_floor16(base), cap - WIN)
    tbig_c = min(2048, t)
    tbig_u = min(1024, t)
    while tbig_u + 32 > cap:
        tbig_u //= 2
    nsb_u = tbig_u // SUB
    abig = jnp.minimum(_floor16(base[:, ::nsb_u]), cap - (tbig_u + 32))
    rel = start - jnp.repeat(abig, nsb_u, axis=1)

    base_next = jnp.concatenate([base[:, 1:], jnp.full((N_EXPERTS, 1), cap, jnp.int32)], axis=1)
    start_small = jnp.minimum(_floor16(base), cap - SMALL_WIN)
    ab_pair = jnp.minimum(_floor16(base[:, ::2]), cap - PAIR_WIN)
    rel_small = start_small - jnp.repeat(ab_pair, 2, axis=1)
    fits_small = (jnp.all(base_next - _floor16(base) <= SMALL_WIN)
                  & jnp.all(rel_small <= PAIR_WIN - SMALL_WIN))

    u2f = u2.reshape(t, d)
    start_blk = _floor16(base[:, ::2])
    next_blk = base_next[:, 1::2]
    xe = lax.cond(
        jnp.all(next_blk - start_blk <= CWIN),
        lambda: _compact_small(u2f, pos3, start_blk.reshape(-1), _floor16(next_blk).reshape(-1), cap),
        lambda: _compact(u2f, pos3, start, cap, tbig_c))
    ye = _ffn(xe, cap, w["w_gate"], w["w_up"], w["w_down"])
    x1f = x1.reshape(t, d)
    out = lax.cond(
        fits_small,
        lambda: _uncompact_small(ye, pos3, aff3, ab_pair.reshape(-1), rel_small.reshape(-1), x1f, mod3, boff,
                                 l, w["norm_f_w"]),
        lambda: _uncompact(ye, pos3, aff3, abig.reshape(-1), rel, x1f, mod3, boff, l, w["norm_f_w"], tbig_u))
    return out.reshape(b, l, d)


def _prepare_weights(w_in, w_alpha_f, b_alpha_f, w_alpha_b, b_alpha_b, gla_norm_w, conv_w, conv_b,
                     w_out_a, w_out_b, w_o, norm1_w, norm2_w, w_router, w_gate, w_up, w_down, norm_f_w):
    d = D_MODEL
    wq, wk, wv, wr, wlf, wlb, wcb, wcc, wcx, wga, wgb = jnp.split(
        w_in, np.cumsum([512, 512, 1024, 1024, 16, 16, 1024, 1024, 1024, 1024])[:].tolist(), axis=1)
    pad = jnp.zeros((d, Z_WIDTH - Z_LR - 2 * GATE_RANK), w_in.dtype)
    wq = wq * (HEAD_K ** -0.5)
    w_in_p = jnp.concatenate([wq, wk, wv, wr, wcb, wcc, wcx, wga, wgb, wlf, wlb, pad], axis=1).astype(BF16)

    def pre(wa, ba, row0):
        wh = wa.reshape(GATE_RANK, GLA_HEADS, HEAD_K).transpose(1, 0, 2)
        full = jnp.zeros((GLA_HEADS, 128, HEAD_K), F32).at[:, row0:row0 + GATE_RANK].set(wh)
        return full.astype(BF16), ba.reshape(GLA_HEADS, 1, HEAD_K)

    wpre_f, bpre_f = pre(w_alpha_f, b_alpha_f, 0)
    wpre_b, bpre_b = pre(w_alpha_b, b_alpha_b, GATE_RANK)
    return dict(
        w_in=w_in_p, wpre_f=wpre_f, bpre_f=bpre_f, wpre_b=wpre_b, bpre_b=bpre_b,
        conv_w=conv_w, conv_b=conv_b.reshape(1, d),
        w_out_a=(gla_norm_w.reshape(d, 1) * w_out_a).astype(BF16),
        w_out_b=w_out_b.astype(BF16), w_o=w_o.astype(BF16),
        norm1_w=norm1_w, norm2_w=norm2_w.reshape(1, d), w_router_t=w_router.T.astype(BF16),
        w_gate=w_gate.astype(BF16), w_up=w_up.astype(BF16), w_down=w_down.astype(BF16),
        norm_f_w=norm_f_w.reshape(1, d))


def kernel(x_prompt, x_sample, c_prompt, c_sample, w_ada, b_ada, norm1_w, w_in, w_alpha_f, b_alpha_f, w_alpha_b, b_alpha_b, gla_norm_w, conv_w, conv_b, w_out_a, w_out_b, w_o, norm2_w, w_router, w_gate, w_up, w_down, norm_f_w):
    d = D_MODEL
    bp, bs = c_prompt.shape[0], c_sample.shape[0]
    nrows = -(-(bp + bs) // 8) * 8
    c_all = jnp.concatenate([c_prompt, c_sample, jnp.zeros((nrows - bp - bs, d), F32)], axis=0)
    mod3 = _adaln(c_all, w_ada[0], b_ada[0]).reshape(nrows, 6, d)
    w = _prepare_weights(w_in[0], w_alpha_f[0], b_alpha_f[0], w_alpha_b[0], b_alpha_b[0], gla_norm_w[0],
                         conv_w[0], conv_b[0], w_out_a[0], w_out_b[0], w_o[0], norm1_w[0], norm2_w[0],
                         w_router[0], w_gate[0], w_up[0], w_down[0], norm_f_w)
    y_prompt = _trunk(x_prompt, mod3, 0, w)
    y_sample = _trunk(x_sample, mod3, bp, w)
    return (y_prompt, y_sample)
```

```python
import functools

import numpy as np
import jax
import jax.numpy as jnp
from jax import lax
from jax.experimental import pallas as pl
from jax.experimental.pallas import tpu as pltpu

F32 = jnp.float32
BF16 = jnp.bfloat16

D_MODEL = 1024
GLA_HEADS = 4
HEAD_K = 128
HEAD_V = 256
GATE_RANK = 16
GATE_TAU = 16.0
CHUNK = 64
N_EXPERTS = 16
EC_FACTOR = 2
D_EXPERT = 2816
EPS = 1e-6
LOG2E = 1.4426950408889634

Z_Q, Z_K, Z_V, Z_R, Z_CB, Z_CC, Z_CX, Z_GA, Z_GB, Z_LR = 0, 512, 1024, 2048, 3072, 4096, 5120, 6144, 7168, 8192
Z_WIDTH = 8448

UNIT = 4 * CHUNK
SUB = 128
WIN = SUB + 16
SMALL_WIN = 64
GROUP = 4
GROUP_WIN = SMALL_WIN + 48 * (GROUP - 1)
CWIN = 96
FFN_TM = 1024
VMEM_LIMIT = 56 * 1024 * 1024


def _dot(a, b):
    return jnp.dot(a, b, preferred_element_type=F32)


def _dot_nt(a, b):
    return lax.dot_general(a, b, (((1,), (1,)), ((), ())), preferred_element_type=F32)


def _sigmoid(x):
    return 0.5 * jnp.tanh(0.5 * x) + 0.5


def _dot_tn(a, b):
    return lax.dot_general(a, b, (((0,), (0,)), ((), ())), preferred_element_type=F32)


def _params(sem, limit=VMEM_LIMIT):
    return pltpu.CompilerParams(dimension_semantics=sem, vmem_limit_bytes=limit)


def _adaln_kernel(c_ref, w_ref, b_ref, o_ref):
    c = c_ref[...]
    s = c * jax.nn.sigmoid(c)
    o_ref[...] = jnp.dot(s, w_ref[...], preferred_element_type=F32,
                         precision=lax.Precision.HIGHEST) + b_ref[...]


def _adaln(c_all, w_ada, b_ada):
    nb, d = c_all.shape
    n = w_ada.shape[1]
    tn = 1536
    return pl.pallas_call(
        _adaln_kernel,
        out_shape=jax.ShapeDtypeStruct((nb, n), F32),
        grid=(n // tn,),
        in_specs=[pl.BlockSpec((nb, d), lambda j: (0, 0)),
                  pl.BlockSpec((d, tn), lambda j: (0, j)),
                  pl.BlockSpec((1, tn), lambda j: (0, j))],
        out_specs=pl.BlockSpec((nb, tn), lambda j: (0, j)),
        compiler_params=_params(("arbitrary",)),
        name="adaln",
    )(c_all, w_ada, b_ada.reshape(1, n))


CAST_SPLIT = 8


def _inproj_kernel(ncast, x_ref, mod_ref, n1_ref, w_ref, *refs):
    z_ref = refs[len(refs) // 2]
    x = x_ref[0]
    ms = jnp.mean(x * x, axis=-1, keepdims=True)
    m = mod_ref[0]
    u = (x * lax.rsqrt(ms + EPS)) * (n1_ref[...] * (1.0 + m[1:2])) + m[0:1]
    z_ref[0] = _dot(u.astype(BF16), w_ref[...]).astype(BF16)

    if ncast:
        lin = ((pl.program_id(0) * pl.num_programs(1) + pl.program_id(1)) * pl.num_programs(2)
               + pl.program_id(2))

        @pl.when(lin < ncast)
        def _():
            for src, dst in zip(refs[:3], refs[4:]):
                dst[...] = src[...].astype(BF16)


def _in_proj(x, mod3, boff, norm1_w, w_in_p, expert_weights=None):
    b, l, d = x.shape
    tm = min(1024, l)
    tn = 2816
    nt, nn = l // tm, Z_WIDTH // tn
    in_specs = [pl.BlockSpec((1, tm, d), lambda bi, i, n: (bi, i, 0)),
                pl.BlockSpec((1, 6, d), lambda bi, i, n: (bi + boff, 0, 0)),
                pl.BlockSpec((1, d), lambda bi, i, n: (0, 0)),
                pl.BlockSpec((d, tn), lambda bi, i, n: (0, n))]
    out_specs = [pl.BlockSpec((1, tm, tn), lambda bi, i, n: (bi, i, n))]
    out_shape = [jax.ShapeDtypeStruct((b, l, Z_WIDTH), BF16)]
    args = [x, mod3, norm1_w.reshape(1, d), w_in_p]
    ncast = 0
    if expert_weights is not None and N_EXPERTS * CAST_SPLIT > b * nt * nn:
        return (_in_proj(x, mod3, boff, norm1_w, w_in_p),) + tuple(wt.astype(BF16) for wt in expert_weights)
    if expert_weights is not None:
        ncast = N_EXPERTS * CAST_SPLIT

        def slab(bi, i, n):
            j = jnp.minimum((bi * nt + i) * nn + n, ncast - 1)
            return (j // CAST_SPLIT, j % CAST_SPLIT, 0)

        for wt in expert_weights:
            blk = (1, wt.shape[1] // CAST_SPLIT, wt.shape[2])
            in_specs.append(pl.BlockSpec(blk, slab))
            out_specs.append(pl.BlockSpec(blk, slab))
            out_shape.append(jax.ShapeDtypeStruct(wt.shape, BF16))
            args.append(wt)
    res = pl.pallas_call(
        functools.partial(_inproj_kernel, ncast),
        out_shape=tuple(out_shape),
        grid=(b, nt, nn),
        in_specs=in_specs,
        out_specs=tuple(out_specs),
        compiler_params=_params(("arbitrary", "arbitrary", "arbitrary")),
        name="in_proj",
    )(*args)
    return res if expert_weights is not None else res[0]


def _gla_constants(reverse):
    r = np.arange(UNIT)[:, None]
    s = np.arange(UNIT)[None, :]
    cr, cs = r // CHUNK, s // CHUNK
    if reverse:
        tri = (cr == cs) & (s >= r)
        tr, ts = 3 - cr, 3 - cs
    else:
        tri = (cr == cs) & (s <= r)
        tr, ts = cr, cs
    code = np.zeros((UNIT, UNIT), np.int32)
    code[tri] = 1
    return jnp.asarray(tri, BF16), jnp.asarray(code)


def _gla_kernel(reverse, nunits, has_prev, *refs):
    if has_prev:
        q_ref, k_ref, v_ref, lr_ref, wp_ref, bp_ref, tri_ref, code_ref, prev_ref, o_ref, s_scr = refs
    else:
        q_ref, k_ref, v_ref, lr_ref, wp_ref, bp_ref, tri_ref, code_ref, o_ref, s_scr = refs
        prev_ref = None

    @pl.when(pl.program_id(2) == 0)
    def _():
        s_scr[...] = jnp.zeros_like(s_scr)

    order = (3, 2, 1, 0) if reverse else (0, 1, 2, 3)

    def to_mem(by_step):
        out = [None] * 4
        for t in range(4):
            out[order[t]] = by_step[t]
        return out

    def scale_chunks(arr, logf):
        parts = []
        for c in range(4):
            part = arr[c * CHUNK:(c + 1) * CHUNK]
            if logf[c] is not None:
                part = part * jnp.exp2(logf[c])
            parts.append(part)
        return jnp.concatenate(parts, axis=0).astype(BF16)

    def rows_of(ui):
        um = (nunits - 1 - ui) if reverse else ui
        return pl.ds(um * UNIT, UNIT)

    def decays(ui):
        pre = _dot(lr_ref[0, rows_of(ui), :], wp_ref[0]) + bp_ref[0]
        soft = jnp.log2(1.0 + jnp.exp2(jnp.abs(pre) * (-LOG2E)))
        la = (jnp.minimum(pre, 0.0) * LOG2E - soft) * (1.0 / GATE_TAU)
        hi = la.astype(BF16)
        lo = (la - hi.astype(F32)).astype(BF16)
        bb = _dot(tri_ref[...], jnp.concatenate([hi, lo], axis=1))
        return bb[:, :HEAD_K] + bb[:, HEAD_K:]

    def operands(ui, b):
        rows = rows_of(ui)
        edge = 0 if reverse else CHUNK - 1
        tot = [b[c * CHUNK + edge:c * CHUNK + edge + 1] for c in range(4)]
        tot_full = jnp.concatenate([jnp.broadcast_to(t, (CHUNK, HEAD_K)) for t in tot], axis=0)
        q = q_ref[0, rows, :].astype(F32)
        k = k_ref[0, rows, :].astype(F32)
        q_t = q * jnp.exp2(b)
        k_t = (k * jnp.exp2(-b)).astype(BF16)
        k_e = k * jnp.exp2(tot_full - b)
        q_tb = q_t.astype(BF16)
        k_eb = k_e.astype(BF16)
        bt = [tot[o] for o in order]

        zero = jnp.zeros((CHUNK, HEAD_K), BF16)
        e1 = jnp.exp2(bt[1])
        e2 = jnp.exp2(bt[2])
        qrows, krows = [None] * 4, [None] * 4
        for t in range(4):
            c = order[t]
            sl = slice(c * CHUNK, (c + 1) * CHUNK)
            qc, kc = q_tb[sl], k_eb[sl]
            q3 = qc if t == 2 else ((q_t[sl] * e2).astype(BF16) if t == 3 else zero)
            k3 = (k_e[sl] * e1).astype(BF16) if t == 0 else (kc if t == 1 else zero)
            qrows[c] = jnp.concatenate([qc if t == 1 else zero, qc if t == 3 else zero, q3], axis=1)
            krows[c] = jnp.concatenate([kc if t == 0 else zero, kc if t == 2 else zero, k3], axis=1)

        gq = to_mem([None, bt[0], bt[0] + bt[1], bt[0] + bt[1] + bt[2]])
        gk = to_mem([bt[1] + bt[2] + bt[3], bt[2] + bt[3], bt[3], None])
        dec = jnp.exp2(bt[0] + bt[1] + bt[2] + bt[3])
        return dict(q_t=q_tb, k_t=k_t, q_cat=jnp.concatenate(qrows, axis=0), k_cat=jnp.concatenate(krows, axis=0),
                    qs=scale_chunks(q_t, gq), ks=scale_chunks(k_e, gk), dec=dec)

    def scores(p):
        same = _dot_nt(p["q_t"], p["k_t"])
        cross = _dot_nt(p["q_cat"], p["k_cat"])
        return jnp.where(code_ref[...] == 1, same, cross).astype(BF16)

    def finish(ui, p, intra):
        rows = rows_of(ui)
        st = s_scr[...]
        o = intra + _dot_nt(p["qs"], st.astype(BF16))
        s_scr[...] = st * p["dec"] + _dot_tn(v_ref[0, rows, :], p["ks"])
        if prev_ref is not None:
            o = o + prev_ref[0, rows, :]
        o_ref[0, rows, :] = o

    stages = 5
    bs, ps, atts, intra = {}, {}, {}, {}
    for wave in range(nunits + stages - 1):
        for k in range(stages):
            ui = wave - k
            if not 0 <= ui < nunits:
                continue
            if k == 0:
                bs[ui] = decays(ui)
            elif k == 1:
                ps[ui] = operands(ui, bs.pop(ui))
            elif k == 2:
                atts[ui] = scores(ps[ui])
            elif k == 3:
                intra[ui] = _dot(atts.pop(ui), v_ref[0, rows_of(ui), :])
            else:
                finish(ui, ps.pop(ui), intra.pop(ui))


def _gla(z, wpre, bpre, reverse, prev):
    b, l, _ = z.shape
    lb = min(2048, l)
    nb = l // lb
    nunits = lb // UNIT
    tri, code = _gla_constants(reverse)

    def seq(i):
        return (nb - 1 - i) if reverse else i

    in_specs = [
        pl.BlockSpec((1, lb, HEAD_K), lambda bi, h, i: (bi, seq(i), Z_Q // HEAD_K + h)),
        pl.BlockSpec((1, lb, HEAD_K), lambda bi, h, i: (bi, seq(i), Z_K // HEAD_K + h)),
        pl.BlockSpec((1, lb, HEAD_V), lambda bi, h, i: (bi, seq(i), Z_V // HEAD_V + h)),
        pl.BlockSpec((1, lb, 128), lambda bi, h, i: (bi, seq(i), Z_LR // 128)),
        pl.BlockSpec((1, 128, HEAD_K), lambda bi, h, i: (h, 0, 0)),
        pl.BlockSpec((1, 1, HEAD_K), lambda bi, h, i: (h, 0, 0)),
        pl.BlockSpec((UNIT, UNIT), lambda bi, h, i: (0, 0)),
        pl.BlockSpec((UNIT, UNIT), lambda bi, h, i: (0, 0)),
    ]
    args = [z, z, z, z, wpre, bpre, tri, code]
    if prev is not None:
        in_specs.append(pl.BlockSpec((1, lb, HEAD_V), lambda bi, h, i: (bi, seq(i), h)))
        args.append(prev)
    return pl.pallas_call(
        functools.partial(_gla_kernel, reverse, nunits, prev is not None),
        out_shape=jax.ShapeDtypeStruct((b, l, GLA_HEADS * HEAD_V), F32),
        grid=(b, GLA_HEADS, nb),
        in_specs=in_specs,
        out_specs=pl.BlockSpec((1, lb, HEAD_V), lambda bi, h, i: (bi, seq(i), h)),
        scratch_shapes=[pltpu.VMEM((HEAD_V, HEAD_K), F32)],
        compiler_params=_params(("arbitrary", "arbitrary", "arbitrary")),
        name="gla_bwd" if reverse else "gla_fwd",
    )(*args)


def _mix_kernel(tm, x_ref, o_ref, r_ref, cb_ref, cc_ref, cx_ref, ccp_ref, cxp_ref, ccn_ref, cxn_ref,
                ga_ref, gb_ref, mod_ref, cw_ref, cbias_ref, woa_ref, wob_ref, wo_ref,
                n2_ref, wrt_ref, x1_ref, u2_ref, aff_ref):
    i = pl.program_id(1)
    last = pl.num_programs(1) - 1

    o = o_ref[0]
    parts = []
    for h in range(GLA_HEADS):
        oh = o[:, h * HEAD_V:(h + 1) * HEAD_V]
        ms = jnp.mean(oh * oh, axis=-1, keepdims=True)
        parts.append(oh * lax.rsqrt(ms + EPS))
    on = jnp.concatenate(parts, axis=1)
    r = r_ref[0]
    branch_a = _dot(on.astype(BF16) * (r * _sigmoid(r)), woa_ref[...])

    xc = cc_ref[0].astype(F32) * cx_ref[0].astype(F32)
    prev_row = ccp_ref[0, 15:16, :].astype(F32) * cxp_ref[0, 15:16, :].astype(F32)
    next_row = ccn_ref[0, 0:1, :].astype(F32) * cxn_ref[0, 0:1, :].astype(F32)
    prev_row = jnp.where(i > 0, prev_row, 0.0)
    next_row = jnp.where(i < last, next_row, 0.0)
    rowid = lax.broadcasted_iota(jnp.int32, (tm, 1), 0)
    xm1 = jnp.where(rowid == 0, prev_row, pltpu.roll(xc, 1, axis=0))
    xp1 = jnp.where(rowid == tm - 1, next_row, pltpu.roll(xc, tm - 1, axis=0))
    cw = cw_ref[...]
    conv = xm1 * cw[0:1] + xc * cw[1:2] + xp1 * cw[2:3] + cbias_ref[...]
    branch_b = _dot((cb_ref[0].astype(F32) * conv).astype(BF16), wob_ref[...])

    merged = (_sigmoid(ga_ref[0]) * branch_a.astype(BF16)
              + _sigmoid(gb_ref[0]) * branch_b.astype(BF16))
    mix = _dot(merged, wo_ref[...])

    m = mod_ref[0]
    x1 = x_ref[0] + m[2:3] * mix
    x1_ref[0] = x1
    ms = jnp.mean(x1 * x1, axis=-1, keepdims=True)
    u2 = (x1 * lax.rsqrt(ms + EPS)) * (n2_ref[...] * (1.0 + m[4:5])) + m[3:4]
    u2b = u2.astype(BF16)
    u2_ref[0] = u2b

    logits = _dot_nt(wrt_ref[...], u2b)
    ex = jnp.exp(logits - jnp.max(logits, axis=0, keepdims=True))
    aff = ex / jnp.sum(ex, axis=0, keepdims=True)
    for j in range(tm // SUB):
        aff_ref[j] = aff[:, j * SUB:(j + 1) * SUB]


def _mix_out(x, o, z, mod3, boff, conv_w, conv_b, woa, wob, wo, n2, wrt):
    b, l, d = x.shape
    tm = min(512, l)
    nt = l // tm
    hb = tm // 16
    nhalo = l // 16

    def col(c):
        return lambda bi, i: (bi, i, c // d)

    in_specs = [
        pl.BlockSpec((1, tm, d), lambda bi, i: (bi, i, 0)),
        pl.BlockSpec((1, tm, d), lambda bi, i: (bi, i, 0)),
        pl.BlockSpec((1, tm, d), col(Z_R)),
        pl.BlockSpec((1, tm, d), col(Z_CB)),
        pl.BlockSpec((1, tm, d), col(Z_CC)),
        pl.BlockSpec((1, tm, d), col(Z_CX)),
        pl.BlockSpec((1, 16, d), lambda bi, i: (bi, jnp.maximum(i * hb - 1, 0), Z_CC // d)),
        pl.BlockSpec((1, 16, d), lambda bi, i: (bi, jnp.maximum(i * hb - 1, 0), Z_CX // d)),
        pl.BlockSpec((1, 16, d), lambda bi, i: (bi, jnp.minimum((i + 1) * hb, nhalo - 1), Z_CC // d)),
        pl.BlockSpec((1, 16, d), lambda bi, i: (bi, jnp.minimum((i + 1) * hb, nhalo - 1), Z_CX // d)),
        pl.BlockSpec((1, tm, d), col(Z_GA)),
        pl.BlockSpec((1, tm, d), col(Z_GB)),
        pl.BlockSpec((1, 6, d), lambda bi, i: (bi + boff, 0, 0)),
        pl.BlockSpec((3, d), lambda bi, i: (0, 0)),
        pl.BlockSpec((1, d), lambda bi, i: (0, 0)),
        pl.BlockSpec((d, d), lambda bi, i: (0, 0)),
        pl.BlockSpec((d, d), lambda bi, i: (0, 0)),
        pl.BlockSpec((d, d), lambda bi, i: (0, 0)),
        pl.BlockSpec((1, d), lambda bi, i: (0, 0)),
        pl.BlockSpec((N_EXPERTS, d), lambda bi, i: (0, 0)),
    ]
    out_shape = (jax.ShapeDtypeStruct((b, l, d), F32),
                 jax.ShapeDtypeStruct((b, l, d), BF16),
                 jax.ShapeDtypeStruct((b * l // SUB, N_EXPERTS, SUB), F32))
    out_specs = (pl.BlockSpec((1, tm, d), lambda bi, i: (bi, i, 0)),
                 pl.BlockSpec((1, tm, d), lambda bi, i: (bi, i, 0)),
                 pl.BlockSpec((tm // SUB, N_EXPERTS, SUB), lambda bi, i: (bi * nt + i, 0, 0)))
    return pl.pallas_call(
        functools.partial(_mix_kernel, tm),
        out_shape=out_shape,
        grid=(b, nt),
        in_specs=in_specs,
        out_specs=out_specs,
        compiler_params=_params(("arbitrary", "arbitrary")),
        name="mix_out",
    )(x, o, z, z, z, z, z, z, z, z, z, z, mod3, conv_w, conv_b, woa, wob, wo, n2, wrt)


def _select_kernel(cap, nsb, aff_ref, u_ref, pos_ref, base_ref):
    capf = float(cap)

    def bits(j):
        return pltpu.bitcast(aff_ref[j], jnp.int32)

    def count(pred):
        def body(j, acc):
            return acc + jnp.where(pred(bits(j)), 1.0, 0.0)
        acc = lax.fori_loop(0, nsb, body, jnp.zeros((N_EXPERTS, SUB), F32), unroll=8)
        return jnp.sum(acc, axis=1, keepdims=True)

    def bit_body(t, thr):
        cand = thr | jnp.left_shift(jnp.int32(1), 30 - t)
        return jnp.where(count(lambda v: v >= cand) >= capf, cand, thr)

    thr = lax.fori_loop(0, 31, bit_body, jnp.zeros((N_EXPERTS, SUB), jnp.int32))
    need = capf - count(lambda v: v > thr)

    def blk(j, carry):
        ceq, csel = carry
        v = bits(j)
        gtf = jnp.where(v > thr, 1.0, 0.0)
        eqf = jnp.where(v == thr, 1.0, 0.0)
        r1 = _dot(eqf.astype(BF16), u_ref[...])
        eq_rank = r1[:, :SUB] - eqf + ceq
        sel = gtf + jnp.where(eq_rank < need, eqf, 0.0)
        r2 = _dot(sel.astype(BF16), u_ref[...])
        pos_ref[j] = jnp.where(sel > 0.5, r2[:, :SUB] + csel - 1.0, -1.0).astype(jnp.int32)
        base_ref[j] = csel.astype(jnp.int32)
        return ceq + r1[:, SUB:], csel + r2[:, SUB:]

    zero = jnp.zeros((N_EXPERTS, SUB), F32)
    lax.fori_loop(0, nsb, blk, (zero, zero), unroll=8)


def _select(aff3, cap):
    nsb = aff3.shape[0]
    tri = np.triu(np.ones((SUB, SUB), np.float32))
    umat = jnp.asarray(np.concatenate([tri, np.ones((SUB, SUB), np.float32)], axis=1), BF16)
    shp = jax.ShapeDtypeStruct((nsb, N_EXPERTS, SUB), jnp.int32)
    return pl.pallas_call(
        functools.partial(_select_kernel, cap, nsb),
        out_shape=(shp, shp),
        compiler_params=pltpu.CompilerParams(vmem_limit_bytes=VMEM_LIMIT),
        name="select",
    )(aff3, umat)


def _compact_kernel(nsb, start_ref, pos_ref, x_ref, xe_ref):
    e = pl.program_id(0)

    @pl.when(pl.program_id(1) == 0)
    def _():
        xe_ref[...] = jnp.zeros_like(xe_ref)

    rowi = lax.broadcasted_iota(jnp.int32, (WIN, SUB), 0)

    def body(s, carry):
        st = pl.multiple_of(start_ref[0, 0, s], 16)
        local = pos_ref[s, pl.ds(e, 1), :] - st
        onehot = jnp.where(rowi == local, 1.0, 0.0).astype(BF16)
        xs = x_ref[pl.ds(s * SUB, SUB), :]
        rows = _dot(onehot, xs).astype(BF16)
        xe_ref[0, pl.ds(st, WIN), :] = xe_ref[0, pl.ds(st, WIN), :] + rows
        return carry

    for s in range(nsb):
        body(s, 0)


def _compact(u2, pos3, start, cap, tbig):
    t, d = u2.shape
    nbig = t // tbig
    nsb = tbig // SUB
    return pl.pallas_call(
        functools.partial(_compact_kernel, nsb),
        out_shape=jax.ShapeDtypeStruct((N_EXPERTS, cap + CWIN, d), BF16),
        grid=(N_EXPERTS, nbig),
        in_specs=[pl.BlockSpec((1, 1, nsb), lambda e, b: (e * nbig + b, 0, 0), memory_space=pltpu.SMEM),
                  pl.BlockSpec((nsb, N_EXPERTS, SUB), lambda e, b: (b, 0, 0)),
                  pl.BlockSpec((tbig, d), lambda e, b: (b, 0))],
        out_specs=pl.BlockSpec((1, cap + CWIN, d), lambda e, b: (e, 0, 0)),
        compiler_params=_params(("arbitrary", "arbitrary")),
        name="compact",
    )(start.reshape(N_EXPERTS * nbig, 1, nsb), pos3, u2)


def _compact_small_kernel(nblk, cap, start_ref, next_ref, pos_ref, x_ref, xe_ref, strip_buf, tail_buf, zero_buf,
                          sems, zsem):
    p = pl.program_id(0)
    last = pl.num_programs(0) - 1
    slot = lax.rem(p, 2)

    @pl.when(p == 0)
    def _():
        tail_buf[...] = jnp.zeros_like(tail_buf)
        zero_buf[...] = jnp.zeros_like(zero_buf)

    def strip_copy(sl, e, row0):
        return pltpu.make_async_copy(strip_buf.at[sl, e], xe_ref.at[e, pl.ds(row0, CWIN), :], sems.at[sl, e])

    pos = jnp.concatenate([pos_ref[0], pos_ref[1]], axis=1)
    rowi = lax.broadcasted_iota(jnp.int32, (CWIN, 2 * SUB), 0)
    starts = [pl.multiple_of(start_ref[e * nblk + p], 16) for e in range(N_EXPERTS)]
    onehot = jnp.concatenate(
        [jnp.where(rowi == pos[e:e + 1] - starts[e], 1.0, 0.0).astype(BF16) for e in range(N_EXPERTS)], axis=0)
    rows = _dot(onehot, x_ref[...]).astype(BF16)
    for e in range(N_EXPERTS):
        strip_buf[slot, e, 0:16, :] = rows[e * CWIN:e * CWIN + 16] + tail_buf[e]
        strip_buf[slot, e, 16:CWIN, :] = rows[e * CWIN + 16:(e + 1) * CWIN]
        t0 = next_ref[e * nblk + p] - starts[e]
        t0c = pl.multiple_of(jnp.minimum(t0, CWIN - 16), 16)
        tail_buf[e] = jnp.where(t0 < CWIN, strip_buf[slot, e, pl.ds(t0c, 16), :], jnp.zeros((16, 1), BF16))

    for e in range(N_EXPERTS):
        @pl.when(p > 0)
        def _():
            strip_copy(1 - slot, e, starts[e]).wait()
        strip_copy(slot, e, starts[e]).start()

    @pl.when(p == last)
    def _():
        for e in range(N_EXPERTS):
            strip_copy(slot, e, starts[e]).wait()
            pad = pltpu.make_async_copy(zero_buf, xe_ref.at[e, pl.ds(cap, CWIN), :], zsem)
            pad.start()
            pad.wait()


def _compact_small(u2, pos3, start, start_next, cap):
    t, d = u2.shape
    nblk = t // (2 * SUB)
    grid_spec = pltpu.PrefetchScalarGridSpec(
        num_scalar_prefetch=2,
        grid=(nblk,),
        in_specs=[pl.BlockSpec((2, N_EXPERTS, SUB), lambda p, a, b: (p, 0, 0)),
                  pl.BlockSpec((2 * SUB, d), lambda p, a, b: (p, 0))],
        out_specs=pl.BlockSpec(memory_space=pl.ANY),
        scratch_shapes=[pltpu.VMEM((2, N_EXPERTS, CWIN, d), BF16),
                        pltpu.VMEM((N_EXPERTS, 16, d), BF16),
                        pltpu.VMEM((CWIN, d), BF16),
                        pltpu.SemaphoreType.DMA((2, N_EXPERTS)),
                        pltpu.SemaphoreType.DMA(())],
    )
    return pl.pallas_call(
        functools.partial(_compact_small_kernel, nblk, cap),
        out_shape=jax.ShapeDtypeStruct((N_EXPERTS, cap + CWIN, d), BF16),
        grid_spec=grid_spec,
        compiler_params=_params(("arbitrary",)),
        name="compact_small",
    )(start, start_next, pos3, u2)


def _ffn_kernel(chunks, x_ref, wg_ref, wu_ref, wd_ref, o_ref, acc_ref):
    xs = x_ref[0]
    for i, (c0, cw) in enumerate(chunks):
        g = _dot(xs, wg_ref[0, :, c0:c0 + cw])
        u = _dot(xs, wu_ref[0, :, c0:c0 + cw])
        h = ((g * jax.nn.sigmoid(g)) * u).astype(BF16)
        part = _dot(h, wd_ref[0, c0:c0 + cw, :])
        if i == 0:
            acc_ref[...] = part
        else:
            acc_ref[...] += part
    o_ref[0] = acc_ref[...].astype(BF16)


def _ffn(xe, cap, wg, wu, wd):
    e, _, d = xe.shape
    f = wg.shape[2]
    tm = min(FFN_TM, cap)
    chunks = tuple((c0, min(512, f - c0)) for c0 in range(0, f, 512))
    return pl.pallas_call(
        functools.partial(_ffn_kernel, chunks),
        out_shape=jax.ShapeDtypeStruct((e, cap, d), BF16),
        grid=(e, cap // tm),
        in_specs=[pl.BlockSpec((1, tm, d), lambda ei, m: (ei, m, 0)),
                  pl.BlockSpec((1, d, f), lambda ei, m: (ei, 0, 0)),
                  pl.BlockSpec((1, d, f), lambda ei, m: (ei, 0, 0)),
                  pl.BlockSpec((1, f, d), lambda ei, m: (ei, 0, 0))],
        out_specs=pl.BlockSpec((1, tm, d), lambda ei, m: (ei, m, 0)),
        scratch_shapes=[pltpu.VMEM((tm, d), F32)],
        compiler_params=_params(("arbitrary", "arbitrary")),
        name="expert_ffn",
    )(xe, wg, wu, wd)


def _uncompact_kernel(nsb, nbig, abig_ref, rel_ref, pos_ref, aff_ref, ye_ref, x1_ref, mod_ref, nf_ref,
                      out_ref, acc_ref):
    b = pl.program_id(0)
    e = pl.program_id(1)

    @pl.when(e == 0)
    def _():
        acc_ref[...] = jnp.zeros_like(acc_ref)

    rowi = lax.broadcasted_iota(jnp.int32, (WIN, SUB), 0)
    big = abig_ref[e * nbig + b]

    def body(s, carry):
        rel = pl.multiple_of(rel_ref[0, 0, s], 16)
        local = pos_ref[s, pl.ds(e, 1), :] - (big + rel)
        gate = aff_ref[s, pl.ds(e, 1), :]
        weights = jnp.where(rowi == local, gate, 0.0).astype(BF16)
        rows = pl.ds(s * SUB, SUB)
        acc_ref[rows, :] += _dot_tn(weights, ye_ref[pl.ds(rel, WIN), :])
        return carry

    for s in range(nsb):
        body(s, 0)

    @pl.when(e == pl.num_programs(1) - 1)
    def _():
        xo = x1_ref[...] + mod_ref[0][5:6] * acc_ref[...]
        ms = jnp.mean(xo * xo, axis=-1, keepdims=True)
        out_ref[...] = xo * lax.rsqrt(ms + EPS) * nf_ref[...]


def _uncompact(ye, pos3, aff3, abig, rel, x1, mod3, boff, seq_len, nf, tbig):
    t, d = x1.shape
    cap = ye.shape[1]
    nbig = t // tbig
    nsb = tbig // SUB
    wbig = tbig + 32
    grid_spec = pltpu.PrefetchScalarGridSpec(
        num_scalar_prefetch=1,
        grid=(nbig, N_EXPERTS),
        in_specs=[
            pl.BlockSpec((1, 1, nsb), lambda b, e, ab: (e * nbig + b, 0, 0), memory_space=pltpu.SMEM),
            pl.BlockSpec((nsb, N_EXPERTS, SUB), lambda b, e, ab: (b, 0, 0)),
            pl.BlockSpec((nsb, N_EXPERTS, SUB), lambda b, e, ab: (b, 0, 0)),
            pl.BlockSpec((pl.Element(wbig), pl.Element(d)),
                         lambda b, e, ab: (pl.multiple_of(e * cap + ab[e * nbig + b], 16), 0)),
            pl.BlockSpec((tbig, d), lambda b, e, ab: (b, 0)),
            pl.BlockSpec((1, 6, d), lambda b, e, ab: (boff + (b * tbig) // seq_len, 0, 0)),
            pl.BlockSpec((1, d), lambda b, e, ab: (0, 0)),
        ],
        out_specs=pl.BlockSpec((tbig, d), lambda b, e, ab: (b, 0)),
        scratch_shapes=[pltpu.VMEM((tbig, d), F32)],
    )
    return pl.pallas_call(
        functools.partial(_uncompact_kernel, nsb, nbig),
        out_shape=jax.ShapeDtypeStruct((t, d), F32),
        grid_spec=grid_spec,
        compiler_params=_params(("arbitrary", "arbitrary")),
        name="uncompact",
    )(abig, rel.reshape(N_EXPERTS * nbig, 1, nsb), pos3, aff3, ye.reshape(N_EXPERTS * cap, d), x1, mod3, nf)


def _uncompact_small_kernel(nsub, ab_ref, rel_ref, pos_ref, aff_ref, *refs):
    ye_refs = refs[:N_EXPERTS]
    x1_ref, mod_ref, nf_ref, out_ref = refs[N_EXPERTS:]
    p = pl.program_id(0)
    npair = nsub // GROUP
    rowi = lax.broadcasted_iota(jnp.int32, (SMALL_WIN, SUB), 0)
    ys = []
    for j in range(GROUP):
        pos = pos_ref[j]
        aff = aff_ref[j]
        weights, windows = [], []
        for e in range(N_EXPERTS):
            rel = pl.multiple_of(rel_ref[e * nsub + GROUP * p + j], 16)
            local = pos[e:e + 1] - (ab_ref[e * npair + p] + rel)
            weights.append(jnp.where(rowi == local, aff[e:e + 1], 0.0).astype(BF16))
            windows.append(ye_refs[e][pl.ds(rel, SMALL_WIN), :])
        ys.append(_dot_tn(jnp.concatenate(weights, axis=0), jnp.concatenate(windows, axis=0)))
    xo = x1_ref[...] + mod_ref[0][5:6] * jnp.concatenate(ys, axis=0)
    ms = jnp.mean(xo * xo, axis=-1, keepdims=True)
    out_ref[...] = xo * lax.rsqrt(ms + EPS) * nf_ref[...]


def _uncompact_small(ye, pos3, aff3, ab, rel, x1, mod3, boff, seq_len, nf):
    t, d = x1.shape
    cap = ye.shape[1]
    nsub = t // SUB
    npair = nsub // GROUP

    def window(e):
        return pl.BlockSpec((pl.Element(GROUP_WIN), pl.Element(d)),
                            lambda p, ab_ref, rel_ref: (pl.multiple_of(e * cap + ab_ref[e * npair + p], 16), 0))

    grid_spec = pltpu.PrefetchScalarGridSpec(
        num_scalar_prefetch=2,
        grid=(npair,),
        in_specs=[pl.BlockSpec((GROUP, N_EXPERTS, SUB), lambda p, ab_ref, rel_ref: (p, 0, 0)),
                  pl.BlockSpec((GROUP, N_EXPERTS, SUB), lambda p, ab_ref, rel_ref: (p, 0, 0))]
        + [window(e) for e in range(N_EXPERTS)]
        + [pl.BlockSpec((GROUP * SUB, d), lambda p, ab_ref, rel_ref: (p, 0)),
           pl.BlockSpec((1, 6, d), lambda p, ab_ref, rel_ref: (boff + (p * GROUP * SUB) // seq_len, 0, 0)),
           pl.BlockSpec((1, d), lambda p, ab_ref, rel_ref: (0, 0))],
        out_specs=pl.BlockSpec((GROUP * SUB, d), lambda p, ab_ref, rel_ref: (p, 0)),
    )
    ye2 = ye.reshape(N_EXPERTS * cap, d)
    return pl.pallas_call(
        functools.partial(_uncompact_small_kernel, nsub),
        out_shape=jax.ShapeDtypeStruct((t, d), F32),
        grid_spec=grid_spec,
        compiler_params=_params(("arbitrary",)),
        name="uncompact_small",
    )(ab, rel, pos3, aff3, *([ye2] * N_EXPERTS), x1, mod3, nf)


def _floor16(v):
    return (v // 16) * 16


def _trunk(x, mod3, boff, w, experts):
    b, l, d = x.shape
    t = b * l
    cap = EC_FACTOR * t // N_EXPERTS

    if experts[0].dtype == BF16:
        z = _in_proj(x, mod3, boff, w["norm1_w"], w["w_in"])
    else:
        z, *experts = _in_proj(x, mod3, boff, w["norm1_w"], w["w_in"], experts)
    o = _gla(z, w["wpre_f"], w["bpre_f"], False, None)
    o = _gla(z, w["wpre_b"], w["bpre_b"], True, o)
    x1, u2, aff3 = _mix_out(x, o, z, mod3, boff, w["conv_w"], w["conv_b"],
                            w["w_out_a"], w["w_out_b"], w["w_o"], w["norm2_w"], w["w_router_t"])
    pos3, base3 = _select(aff3, cap)

    base = base3[:, :, 0].T
    start = jnp.minimum(_floor16(base), cap - WIN)
    tbig_c = min(2048, t)
    tbig_u = min(1024, t)
    while tbig_u + 32 > cap:
        tbig_u //= 2
    nsb_u = tbig_u // SUB
    abig = jnp.minimum(_floor16(base[:, ::nsb_u]), cap - (tbig_u + 32))
    rel = start - jnp.repeat(abig, nsb_u, axis=1)

    base_next = jnp.concatenate([base[:, 1:], jnp.full((N_EXPERTS, 1), cap, jnp.int32)], axis=1)
    start_small = jnp.minimum(_floor16(base), cap - SMALL_WIN)
    ab_pair = jnp.minimum(_floor16(base[:, ::GROUP]), cap - GROUP_WIN)
    rel_small = start_small - jnp.repeat(ab_pair, GROUP, axis=1)
    fits_small = (jnp.all(base_next - _floor16(base) <= SMALL_WIN)
                  & jnp.all(rel_small <= GROUP_WIN - SMALL_WIN))

    u2f = u2.reshape(t, d)
    start_blk = _floor16(base[:, ::2])
    next_blk = base_next[:, 1::2]
    xe = lax.cond(
        jnp.all(next_blk - start_blk <= CWIN),
        lambda: _compact_small(u2f, pos3, start_blk.reshape(-1), _floor16(next_blk).reshape(-1), cap),
        lambda: _compact(u2f, pos3, start, cap, tbig_c))
    ye = _ffn(xe, cap, *experts)
    x1f = x1.reshape(t, d)
    out = lax.cond(
        fits_small,
        lambda: _uncompact_small(ye, pos3, aff3, ab_pair.reshape(-1), rel_small.reshape(-1), x1f, mod3, boff,
                                 l, w["norm_f_w"]),
        lambda: _uncompact(ye, pos3, aff3, abig.reshape(-1), rel, x1f, mod3, boff, l, w["norm_f_w"], tbig_u))
    return out.reshape(b, l, d), tuple(experts)


def _prepare_weights(w_in, w_alpha_f, b_alpha_f, w_alpha_b, b_alpha_b, gla_norm_w, conv_w, conv_b,
                     w_out_a, w_out_b, w_o, norm1_w, norm2_w, w_router, norm_f_w):
    d = D_MODEL
    wq, wk, wv, wr, wlf, wlb, wcb, wcc, wcx, wga, wgb = jnp.split(
        w_in, np.cumsum([512, 512, 1024, 1024, 16, 16, 1024, 1024, 1024, 1024])[:].tolist(), axis=1)
    pad = jnp.zeros((d, Z_WIDTH - Z_LR - 2 * GATE_RANK), w_in.dtype)
    wq = wq * (HEAD_K ** -0.5)
    w_in_p = jnp.concatenate([wq, wk, wv, wr, wcb, wcc, wcx, wga, wgb, wlf, wlb, pad], axis=1).astype(BF16)

    def pre(wa, ba, row0):
        wh = wa.reshape(GATE_RANK, GLA_HEADS, HEAD_K).transpose(1, 0, 2)
        full = jnp.zeros((GLA_HEADS, 128, HEAD_K), F32).at[:, row0:row0 + GATE_RANK].set(wh)
        return full.astype(BF16), ba.reshape(GLA_HEADS, 1, HEAD_K)

    wpre_f, bpre_f = pre(w_alpha_f, b_alpha_f, 0)
    wpre_b, bpre_b = pre(w_alpha_b, b_alpha_b, GATE_RANK)
    return dict(
        w_in=w_in_p, wpre_f=wpre_f, bpre_f=bpre_f, wpre_b=wpre_b, bpre_b=bpre_b,
        conv_w=conv_w, conv_b=conv_b.reshape(1, d),
        w_out_a=(gla_norm_w.reshape(d, 1) * w_out_a).astype(BF16),
        w_out_b=w_out_b.astype(BF16), w_o=w_o.astype(BF16),
        norm1_w=norm1_w, norm2_w=norm2_w.reshape(1, d), w_router_t=w_router.T.astype(BF16),
        norm_f_w=norm_f_w.reshape(1, d))


def kernel(x_prompt, x_sample, c_prompt, c_sample, w_ada, b_ada, norm1_w, w_in, w_alpha_f, b_alpha_f, w_alpha_b, b_alpha_b, gla_norm_w, conv_w, conv_b, w_out_a, w_out_b, w_o, norm2_w, w_router, w_gate, w_up, w_down, norm_f_w):
    d = D_MODEL
    bp, bs = c_prompt.shape[0], c_sample.shape[0]
    nrows = -(-(bp + bs) // 8) * 8
    c_all = jnp.concatenate([c_prompt, c_sample, jnp.zeros((nrows - bp - bs, d), F32)], axis=0)
    mod3 = _adaln(c_all, w_ada[0], b_ada[0]).reshape(nrows, 6, d)
    w = _prepare_weights(w_in[0], w_alpha_f[0], b_alpha_f[0], w_alpha_b[0], b_alpha_b[0], gla_norm_w[0],
                         conv_w[0], conv_b[0], w_out_a[0], w_out_b[0], w_o[0], norm1_w[0], norm2_w[0],
                         w_router[0], norm_f_w)
    y_prompt, experts = _trunk(x_prompt, mod3, 0, w, (w_gate[0], w_up[0], w_down[0]))
    y_sample, _ = _trunk(x_sample, mod3, bp, w, experts)
    return (y_prompt, y_sample)
```

```python
import functools

import numpy as np
import jax
import jax.numpy as jnp
from jax import lax
from jax.experimental import pallas as pl
from jax.experimental.pallas import tpu as pltpu

F32 = jnp.float32
BF16 = jnp.bfloat16

D_MODEL = 1024
GLA_HEADS = 4
HEAD_K = 128
HEAD_V = 256
GATE_RANK = 16
GATE_TAU = 16.0
CHUNK = 64
N_EXPERTS = 16
EC_FACTOR = 2
D_EXPERT = 2816
EPS = 1e-6
LOG2E = 1.4426950408889634

Z_Q, Z_K, Z_V, Z_R, Z_CB, Z_CC, Z_CX, Z_GA, Z_GB, Z_LR = 0, 512, 1024, 2048, 3072, 4096, 5120, 6144, 7168, 8192
Z_WIDTH = 8448

UNIT = 4 * CHUNK
SUB = 128
WIN = SUB + 16
SMALL_WIN = 64
GROUP = 8
GROUP_WIN = 256
CWIN = 80
FFN_TM = 1024
VMEM_LIMIT = 56 * 1024 * 1024


def _dot(a, b):
    return jnp.dot(a, b, preferred_element_type=F32)


def _dot_nt(a, b):
    return lax.dot_general(a, b, (((1,), (1,)), ((), ())), preferred_element_type=F32)


def _sigmoid(x):
    return 0.5 * jnp.tanh(0.5 * x) + 0.5


def _dot_tn(a, b):
    return lax.dot_general(a, b, (((0,), (0,)), ((), ())), preferred_element_type=F32)


def _params(sem, limit=VMEM_LIMIT):
    return pltpu.CompilerParams(dimension_semantics=sem, vmem_limit_bytes=limit)


def _adaln_kernel(c_ref, w_ref, b_ref, o_ref):
    c = c_ref[...]
    s = c * jax.nn.sigmoid(c)
    o_ref[...] = jnp.dot(s, w_ref[...], preferred_element_type=F32,
                         precision=lax.Precision.HIGHEST) + b_ref[...]


def _adaln(c_all, w_ada, b_ada):
    nb, d = c_all.shape
    n = w_ada.shape[1]
    tn = 1536
    return pl.pallas_call(
        _adaln_kernel,
        out_shape=jax.ShapeDtypeStruct((nb, n), F32),
        grid=(n // tn,),
        in_specs=[pl.BlockSpec((nb, d), lambda j: (0, 0)),
                  pl.BlockSpec((d, tn), lambda j: (0, j)),
                  pl.BlockSpec((1, tn), lambda j: (0, j))],
        out_specs=pl.BlockSpec((nb, tn), lambda j: (0, j)),
        compiler_params=_params(("arbitrary",)),
        name="adaln",
    )(c_all, w_ada, b_ada.reshape(1, n))


CAST_SPLIT = 8


def _inproj_kernel(ncast, x_ref, mod_ref, n1_ref, w_ref, *refs):
    z_ref = refs[len(refs) // 2]
    x = x_ref[0]
    ms = jnp.mean(x * x, axis=-1, keepdims=True)
    m = mod_ref[0]
    u = (x * lax.rsqrt(ms + EPS)) * (n1_ref[...] * (1.0 + m[1:2])) + m[0:1]
    z_ref[0] = _dot(u.astype(BF16), w_ref[...]).astype(BF16)

    if ncast:
        lin = ((pl.program_id(0) * pl.num_programs(1) + pl.program_id(1)) * pl.num_programs(2)
               + pl.program_id(2))

        @pl.when(lin < ncast)
        def _():
            for src, dst in zip(refs[:3], refs[4:]):
                dst[...] = src[...].astype(BF16)


def _in_proj(x, mod3, boff, norm1_w, w_in_p, expert_weights=None):
    b, l, d = x.shape
    tm = min(1024, l)
    tn = 2816
    nt, nn = l // tm, Z_WIDTH // tn
    in_specs = [pl.BlockSpec((1, tm, d), lambda bi, i, n: (bi, i, 0)),
                pl.BlockSpec((1, 6, d), lambda bi, i, n: (bi + boff, 0, 0)),
                pl.BlockSpec((1, d), lambda bi, i, n: (0, 0)),
                pl.BlockSpec((d, tn), lambda bi, i, n: (0, n))]
    out_specs = [pl.BlockSpec((1, tm, tn), lambda bi, i, n: (bi, i, n))]
    out_shape = [jax.ShapeDtypeStruct((b, l, Z_WIDTH), BF16)]
    args = [x, mod3, norm1_w.reshape(1, d), w_in_p]
    ncast = 0
    if expert_weights is not None and N_EXPERTS * CAST_SPLIT > b * nt * nn:
        return (_in_proj(x, mod3, boff, norm1_w, w_in_p),) + tuple(wt.astype(BF16) for wt in expert_weights)
    if expert_weights is not None:
        ncast = N_EXPERTS * CAST_SPLIT

        def slab(bi, i, n):
            j = jnp.minimum((bi * nt + i) * nn + n, ncast - 1)
            return (j // CAST_SPLIT, j % CAST_SPLIT, 0)

        for wt in expert_weights:
            blk = (1, wt.shape[1] // CAST_SPLIT, wt.shape[2])
            in_specs.append(pl.BlockSpec(blk, slab))
            out_specs.append(pl.BlockSpec(blk, slab))
            out_shape.append(jax.ShapeDtypeStruct(wt.shape, BF16))
            args.append(wt)
    res = pl.pallas_call(
        functools.partial(_inproj_kernel, ncast),
        out_shape=tuple(out_shape),
        grid=(b, nt, nn),
        in_specs=in_specs,
        out_specs=tuple(out_specs),
        compiler_params=_params(("arbitrary", "arbitrary", "arbitrary")),
        name="in_proj",
    )(*args)
    return res if expert_weights is not None else res[0]


def _gla_constants(reverse):
    r = np.arange(UNIT)[:, None]
    s = np.arange(UNIT)[None, :]
    cr, cs = r // CHUNK, s // CHUNK
    if reverse:
        tri = (cr == cs) & (s >= r)
        tr, ts = 3 - cr, 3 - cs
    else:
        tri = (cr == cs) & (s <= r)
        tr, ts = cr, cs
    code = np.zeros((UNIT, UNIT), np.int32)
    code[tri] = 1
    return jnp.asarray(tri, BF16), jnp.asarray(code)


def _gla_kernel(reverse, nunits, has_prev, *refs):
    if has_prev:
        q_ref, k_ref, v_ref, lr_ref, wp_ref, bp_ref, tri_ref, code_ref, prev_ref, o_ref, s_scr = refs
    else:
        q_ref, k_ref, v_ref, lr_ref, wp_ref, bp_ref, tri_ref, code_ref, o_ref, s_scr = refs
        prev_ref = None

    @pl.when(pl.program_id(2) == 0)
    def _():
        s_scr[...] = jnp.zeros_like(s_scr)

    order = (3, 2, 1, 0) if reverse else (0, 1, 2, 3)

    def to_mem(by_step):
        out = [None] * 4
        for t in range(4):
            out[order[t]] = by_step[t]
        return out

    def scale_chunks(arr, logf):
        parts = []
        for c in range(4):
            part = arr[c * CHUNK:(c + 1) * CHUNK]
            if logf[c] is not None:
                part = part * jnp.exp2(logf[c])
            parts.append(part)
        return jnp.concatenate(parts, axis=0).astype(BF16)

    def rows_of(ui):
        um = (nunits - 1 - ui) if reverse else ui
        return pl.ds(um * UNIT, UNIT)

    def decays(ui):
        pre = _dot(lr_ref[0, rows_of(ui), :], wp_ref[0]) + bp_ref[0]
        soft = jnp.log2(1.0 + jnp.exp2(jnp.abs(pre) * (-LOG2E)))
        la = (jnp.minimum(pre, 0.0) * LOG2E - soft) * (1.0 / GATE_TAU)
        hi = la.astype(BF16)
        lo = (la - hi.astype(F32)).astype(BF16)
        bb = _dot(tri_ref[...], jnp.concatenate([hi, lo], axis=1))
        return bb[:, :HEAD_K] + bb[:, HEAD_K:]

    def operands(ui, b):
        rows = rows_of(ui)
        edge = 0 if reverse else CHUNK - 1
        tot = [b[c * CHUNK + edge:c * CHUNK + edge + 1] for c in range(4)]
        tot_full = jnp.concatenate([jnp.broadcast_to(t, (CHUNK, HEAD_K)) for t in tot], axis=0)
        q = q_ref[0, rows, :].astype(F32)
        k = k_ref[0, rows, :].astype(F32)
        q_t = q * jnp.exp2(b)
        k_t = (k * jnp.exp2(-b)).astype(BF16)
        k_e = k * jnp.exp2(tot_full - b)
        q_tb = q_t.astype(BF16)
        k_eb = k_e.astype(BF16)
        bt = [tot[o] for o in order]

        zero = jnp.zeros((CHUNK, HEAD_K), BF16)
        e1 = jnp.exp2(bt[1])
        e2 = jnp.exp2(bt[2])
        qrows, krows = [None] * 4, [None] * 4
        for t in range(4):
            c = order[t]
            sl = slice(c * CHUNK, (c + 1) * CHUNK)
            qc, kc = q_tb[sl], k_eb[sl]
            q3 = qc if t == 2 else ((q_t[sl] * e2).astype(BF16) if t == 3 else zero)
            k3 = (k_e[sl] * e1).astype(BF16) if t == 0 else (kc if t == 1 else zero)
            qrows[c] = jnp.concatenate([qc if t == 1 else zero, qc if t == 3 else zero, q3], axis=1)
            krows[c] = jnp.concatenate([kc if t == 0 else zero, kc if t == 2 else zero, k3], axis=1)

        gq = to_mem([None, bt[0], bt[0] + bt[1], bt[0] + bt[1] + bt[2]])
        gk = to_mem([bt[1] + bt[2] + bt[3], bt[2] + bt[3], bt[3], None])
        dec = jnp.exp2(bt[0] + bt[1] + bt[2] + bt[3])
        return dict(q_t=q_tb, k_t=k_t, q_cat=jnp.concatenate(qrows, axis=0), k_cat=jnp.concatenate(krows, axis=0),
                    qs=scale_chunks(q_t, gq), ks=scale_chunks(k_e, gk), dec=dec)

    def scores(p):
        same = _dot_nt(p["q_t"], p["k_t"])
        cross = _dot_nt(p["q_cat"], p["k_cat"])
        return jnp.where(code_ref[...] == 1, same, cross).astype(BF16)

    def finish(ui, p, intra):
        rows = rows_of(ui)
        st = s_scr[...]
        o = intra + _dot_nt(p["qs"], st.astype(BF16))
        s_scr[...] = st * p["dec"] + _dot_tn(v_ref[0, rows, :], p["ks"])
        if prev_ref is not None:
            o = o + prev_ref[0, rows, :]
        o_ref[0, rows, :] = o

    stages = 5
    bs, ps, atts, intra = {}, {}, {}, {}
    for wave in range(nunits + stages - 1):
        for k in range(stages):
            ui = wave - k
            if not 0 <= ui < nunits:
                continue
            if k == 0:
                bs[ui] = decays(ui)
            elif k == 1:
                ps[ui] = operands(ui, bs.pop(ui))
            elif k == 2:
                atts[ui] = scores(ps[ui])
            elif k == 3:
                intra[ui] = _dot(atts.pop(ui), v_ref[0, rows_of(ui), :])
            else:
                finish(ui, ps.pop(ui), intra.pop(ui))


def _gla(z, wpre, bpre, reverse, prev):
    b, l, _ = z.shape
    lb = min(2048, l)
    nb = l // lb
    nunits = lb // UNIT
    tri, code = _gla_constants(reverse)

    def seq(i):
        return (nb - 1 - i) if reverse else i

    in_specs = [
        pl.BlockSpec((1, lb, HEAD_K), lambda bi, h, i: (bi, seq(i), Z_Q // HEAD_K + h)),
        pl.BlockSpec((1, lb, HEAD_K), lambda bi, h, i: (bi, seq(i), Z_K // HEAD_K + h)),
        pl.BlockSpec((1, lb, HEAD_V), lambda bi, h, i: (bi, seq(i), Z_V // HEAD_V + h)),
        pl.BlockSpec((1, lb, 128), lambda bi, h, i: (bi, seq(i), Z_LR // 128)),
        pl.BlockSpec((1, 128, HEAD_K), lambda bi, h, i: (h, 0, 0)),
        pl.BlockSpec((1, 1, HEAD_K), lambda bi, h, i: (h, 0, 0)),
        pl.BlockSpec((UNIT, UNIT), lambda bi, h, i: (0, 0)),
        pl.BlockSpec((UNIT, UNIT), lambda bi, h, i: (0, 0)),
    ]
    args = [z, z, z, z, wpre, bpre, tri, code]
    if prev is not None:
        in_specs.append(pl.BlockSpec((1, lb, HEAD_V), lambda bi, h, i: (bi, seq(i), h)))
        args.append(prev)
    return pl.pallas_call(
        functools.partial(_gla_kernel, reverse, nunits, prev is not None),
        out_shape=jax.ShapeDtypeStruct((b, l, GLA_HEADS * HEAD_V), F32),
        grid=(b, GLA_HEADS, nb),
        in_specs=in_specs,
        out_specs=pl.BlockSpec((1, lb, HEAD_V), lambda bi, h, i: (bi, seq(i), h)),
        scratch_shapes=[pltpu.VMEM((HEAD_V, HEAD_K), F32)],
        compiler_params=_params(("arbitrary", "arbitrary", "arbitrary")),
        name="gla_bwd" if reverse else "gla_fwd",
    )(*args)


def _mix_kernel(tm, x_ref, o_ref, r_ref, cb_ref, cc_ref, cx_ref, ccp_ref, cxp_ref, ccn_ref, cxn_ref,
                ga_ref, gb_ref, mod_ref, cw_ref, cbias_ref, woa_ref, wob_ref, wo_ref,
                n2_ref, wrt_ref, x1_ref, u2_ref, aff_ref):
    i = pl.program_id(1)
    last = pl.num_programs(1) - 1

    o = o_ref[0]
    parts = []
    for h in range(GLA_HEADS):
        oh = o[:, h * HEAD_V:(h + 1) * HEAD_V]
        ms = jnp.mean(oh * oh, axis=-1, keepdims=True)
        parts.append(oh * lax.rsqrt(ms + EPS))
    on = jnp.concatenate(parts, axis=1)
    r = r_ref[0]
    branch_a = _dot(on.astype(BF16) * (r * _sigmoid(r)), woa_ref[...])

    xc = cc_ref[0].astype(F32) * cx_ref[0].astype(F32)
    prev_row = ccp_ref[0, 15:16, :].astype(F32) * cxp_ref[0, 15:16, :].astype(F32)
    next_row = ccn_ref[0, 0:1, :].astype(F32) * cxn_ref[0, 0:1, :].astype(F32)
    prev_row = jnp.where(i > 0, prev_row, 0.0)
    next_row = jnp.where(i < last, next_row, 0.0)
    rowid = lax.broadcasted_iota(jnp.int32, (tm, 1), 0)
    xm1 = jnp.where(rowid == 0, prev_row, pltpu.roll(xc, 1, axis=0))
    xp1 = jnp.where(rowid == tm - 1, next_row, pltpu.roll(xc, tm - 1, axis=0))
    cw = cw_ref[...]
    conv = xm1 * cw[0:1] + xc * cw[1:2] + xp1 * cw[2:3] + cbias_ref[...]
    branch_b = _dot((cb_ref[0].astype(F32) * conv).astype(BF16), wob_ref[...])

    merged = (_sigmoid(ga_ref[0]) * branch_a.astype(BF16)
              + _sigmoid(gb_ref[0]) * branch_b.astype(BF16))
    mix = _dot(merged, wo_ref[...])

    m = mod_ref[0]
    x1 = x_ref[0] + m[2:3] * mix
    x1_ref[0] = x1
    ms = jnp.mean(x1 * x1, axis=-1, keepdims=True)
    u2 = (x1 * lax.rsqrt(ms + EPS)) * (n2_ref[...] * (1.0 + m[4:5])) + m[3:4]
    u2b = u2.astype(BF16)
    u2_ref[0] = u2b

    logits = _dot_nt(wrt_ref[...], u2b)
    ex = jnp.exp(logits - jnp.max(logits, axis=0, keepdims=True))
    aff = ex / jnp.sum(ex, axis=0, keepdims=True)
    for j in range(tm // SUB):
        aff_ref[j] = aff[:, j * SUB:(j + 1) * SUB]


def _mix_out(x, o, z, mod3, boff, conv_w, conv_b, woa, wob, wo, n2, wrt):
    b, l, d = x.shape
    tm = min(512, l)
    nt = l // tm
    hb = tm // 16
    nhalo = l // 16

    def col(c):
        return lambda bi, i: (bi, i, c // d)

    in_specs = [
        pl.BlockSpec((1, tm, d), lambda bi, i: (bi, i, 0)),
        pl.BlockSpec((1, tm, d), lambda bi, i: (bi, i, 0)),
        pl.BlockSpec((1, tm, d), col(Z_R)),
        pl.BlockSpec((1, tm, d), col(Z_CB)),
        pl.BlockSpec((1, tm, d), col(Z_CC)),
        pl.BlockSpec((1, tm, d), col(Z_CX)),
        pl.BlockSpec((1, 16, d), lambda bi, i: (bi, jnp.maximum(i * hb - 1, 0), Z_CC // d)),
        pl.BlockSpec((1, 16, d), lambda bi, i: (bi, jnp.maximum(i * hb - 1, 0), Z_CX // d)),
        pl.BlockSpec((1, 16, d), lambda bi, i: (bi, jnp.minimum((i + 1) * hb, nhalo - 1), Z_CC // d)),
        pl.BlockSpec((1, 16, d), lambda bi, i: (bi, jnp.minimum((i + 1) * hb, nhalo - 1), Z_CX // d)),
        pl.BlockSpec((1, tm, d), col(Z_GA)),
        pl.BlockSpec((1, tm, d), col(Z_GB)),
        pl.BlockSpec((1, 6, d), lambda bi, i: (bi + boff, 0, 0)),
        pl.BlockSpec((3, d), lambda bi, i: (0, 0)),
        pl.BlockSpec((1, d), lambda bi, i: (0, 0)),
        pl.BlockSpec((d, d), lambda bi, i: (0, 0)),
        pl.BlockSpec((d, d), lambda bi, i: (0, 0)),
        pl.BlockSpec((d, d), lambda bi, i: (0, 0)),
        pl.BlockSpec((1, d), lambda bi, i: (0, 0)),
        pl.BlockSpec((N_EXPERTS, d), lambda bi, i: (0, 0)),
    ]
    out_shape = (jax.ShapeDtypeStruct((b, l, d), F32),
                 jax.ShapeDtypeStruct((b, l, d), BF16),
                 jax.ShapeDtypeStruct((b * l // SUB, N_EXPERTS, SUB), F32))
    out_specs = (pl.BlockSpec((1, tm, d), lambda bi, i: (bi, i, 0)),
                 pl.BlockSpec((1, tm, d), lambda bi, i: (bi, i, 0)),
                 pl.BlockSpec((tm // SUB, N_EXPERTS, SUB), lambda bi, i: (bi * nt + i, 0, 0)))
    return pl.pallas_call(
        functools.partial(_mix_kernel, tm),
        out_shape=out_shape,
        grid=(b, nt),
        in_specs=in_specs,
        out_specs=out_specs,
        compiler_params=_params(("arbitrary", "arbitrary")),
        name="mix_out",
    )(x, o, z, z, z, z, z, z, z, z, z, z, mod3, conv_w, conv_b, woa, wob, wo, n2, wrt)


def _select_kernel(cap, nsb, aff_ref, u_ref, pos_ref, base_ref):
    capf = float(cap)

    def bits(j):
        return pltpu.bitcast(aff_ref[j], jnp.int32)

    def count(pred):
        def body(j, acc):
            return acc + jnp.where(pred(bits(j)), 1.0, 0.0)
        acc = lax.fori_loop(0, nsb, body, jnp.zeros((N_EXPERTS, SUB), F32), unroll=8)
        return jnp.sum(acc, axis=1, keepdims=True)

    def bit_body(t, thr):
        cand = thr | jnp.left_shift(jnp.int32(1), 30 - t)
        return jnp.where(count(lambda v: v >= cand) >= capf, cand, thr)

    thr = lax.fori_loop(0, 31, bit_body, jnp.zeros((N_EXPERTS, SUB), jnp.int32))
    need = capf - count(lambda v: v > thr)

    def blk(j, carry):
        ceq, csel = carry
        v = bits(j)
        gtf = jnp.where(v > thr, 1.0, 0.0)
        eqf = jnp.where(v == thr, 1.0, 0.0)
        r1 = _dot(eqf.astype(BF16), u_ref[...])
        eq_rank = r1[:, :SUB] - eqf + ceq
        sel = gtf + jnp.where(eq_rank < need, eqf, 0.0)
        r2 = _dot(sel.astype(BF16), u_ref[...])
        pos_ref[j] = jnp.where(sel > 0.5, r2[:, :SUB] + csel - 1.0, -1.0).astype(jnp.int32)
        base_ref[j] = csel.astype(jnp.int32)
        return ceq + r1[:, SUB:], csel + r2[:, SUB:]

    zero = jnp.zeros((N_EXPERTS, SUB), F32)
    lax.fori_loop(0, nsb, blk, (zero, zero), unroll=8)


def _select(aff3, cap):
    nsb = aff3.shape[0]
    tri = np.triu(np.ones((SUB, SUB), np.float32))
    umat = jnp.asarray(np.concatenate([tri, np.ones((SUB, SUB), np.float32)], axis=1), BF16)
    shp = jax.ShapeDtypeStruct((nsb, N_EXPERTS, SUB), jnp.int32)
    return pl.pallas_call(
        functools.partial(_select_kernel, cap, nsb),
        out_shape=(shp, shp),
        compiler_params=pltpu.CompilerParams(vmem_limit_bytes=VMEM_LIMIT),
        name="select",
    )(aff3, umat)


def _compact_kernel(nsb, start_ref, pos_ref, x_ref, xe_ref):
    e = pl.program_id(0)

    @pl.when(pl.program_id(1) == 0)
    def _():
        xe_ref[...] = jnp.zeros_like(xe_ref)

    rowi = lax.broadcasted_iota(jnp.int32, (WIN, SUB), 0)

    def body(s, carry):
        st = pl.multiple_of(start_ref[0, 0, s], 16)
        local = pos_ref[s, pl.ds(e, 1), :] - st
        onehot = jnp.where(rowi == local, 1.0, 0.0).astype(BF16)
        xs = x_ref[pl.ds(s * SUB, SUB), :]
        rows = _dot(onehot, xs).astype(BF16)
        xe_ref[0, pl.ds(st, WIN), :] = xe_ref[0, pl.ds(st, WIN), :] + rows
        return carry

    for s in range(nsb):
        body(s, 0)


def _compact(u2, pos3, start, cap, tbig):
    t, d = u2.shape
    nbig = t // tbig
    nsb = tbig // SUB
    return pl.pallas_call(
        functools.partial(_compact_kernel, nsb),
        out_shape=jax.ShapeDtypeStruct((N_EXPERTS, cap + CWIN, d), BF16),
        grid=(N_EXPERTS, nbig),
        in_specs=[pl.BlockSpec((1, 1, nsb), lambda e, b: (e * nbig + b, 0, 0), memory_space=pltpu.SMEM),
                  pl.BlockSpec((nsb, N_EXPERTS, SUB), lambda e, b: (b, 0, 0)),
                  pl.BlockSpec((tbig, d), lambda e, b: (b, 0))],
        out_specs=pl.BlockSpec((1, cap + CWIN, d), lambda e, b: (e, 0, 0)),
        compiler_params=_params(("arbitrary", "arbitrary")),
        name="compact",
    )(start.reshape(N_EXPERTS * nbig, 1, nsb), pos3, u2)


def _compact_small_kernel(nblk, cap, start_ref, next_ref, pos_ref, x_ref, xe_ref, strip_buf, tail_buf, zero_buf,
                          sems, zsem):
    p = pl.program_id(0)
    last = pl.num_programs(0) - 1
    slot = lax.rem(p, 2)

    @pl.when(p == 0)
    def _():
        tail_buf[...] = jnp.zeros_like(tail_buf)
        zero_buf[...] = jnp.zeros_like(zero_buf)

    def strip_copy(sl, e, row0):
        return pltpu.make_async_copy(strip_buf.at[sl, e], xe_ref.at[e, pl.ds(row0, CWIN), :], sems.at[sl, e])

    pos = jnp.concatenate([pos_ref[0], pos_ref[1]], axis=1)
    rowi = lax.broadcasted_iota(jnp.int32, (CWIN, 2 * SUB), 0)
    starts = [pl.multiple_of(start_ref[e * nblk + p], 16) for e in range(N_EXPERTS)]
    onehot = jnp.concatenate(
        [jnp.where(rowi == pos[e:e + 1] - starts[e], 1.0, 0.0).astype(BF16) for e in range(N_EXPERTS)], axis=0)
    rows = _dot(onehot, x_ref[...]).astype(BF16)
    for e in range(N_EXPERTS):
        strip_buf[slot, e, 0:16, :] = rows[e * CWIN:e * CWIN + 16] + tail_buf[e]
        strip_buf[slot, e, 16:CWIN, :] = rows[e * CWIN + 16:(e + 1) * CWIN]
        t0 = next_ref[e * nblk + p] - starts[e]
        t0c = pl.multiple_of(jnp.minimum(t0, CWIN - 16), 16)
        tail_buf[e] = jnp.where(t0 < CWIN, strip_buf[slot, e, pl.ds(t0c, 16), :], jnp.zeros((16, 1), BF16))

    for e in range(N_EXPERTS):
        @pl.when(p > 0)
        def _():
            strip_copy(1 - slot, e, starts[e]).wait()
        strip_copy(slot, e, starts[e]).start()

    @pl.when(p == last)
    def _():
        for e in range(N_EXPERTS):
            strip_copy(slot, e, starts[e]).wait()
            pad = pltpu.make_async_copy(zero_buf, xe_ref.at[e, pl.ds(cap, CWIN), :], zsem)
            pad.start()
            pad.wait()


def _compact_small(u2, pos3, start, start_next, cap):
    t, d = u2.shape
    nblk = t // (2 * SUB)
    grid_spec = pltpu.PrefetchScalarGridSpec(
        num_scalar_prefetch=2,
        grid=(nblk,),
        in_specs=[pl.BlockSpec((2, N_EXPERTS, SUB), lambda p, a, b: (p, 0, 0)),
                  pl.BlockSpec((2 * SUB, d), lambda p, a, b: (p, 0))],
        out_specs=pl.BlockSpec(memory_space=pl.ANY),
        scratch_shapes=[pltpu.VMEM((2, N_EXPERTS, CWIN, d), BF16),
                        pltpu.VMEM((N_EXPERTS, 16, d), BF16),
                        pltpu.VMEM((CWIN, d), BF16),
                        pltpu.SemaphoreType.DMA((2, N_EXPERTS)),
                        pltpu.SemaphoreType.DMA(())],
    )
    return pl.pallas_call(
        functools.partial(_compact_small_kernel, nblk, cap),
        out_shape=jax.ShapeDtypeStruct((N_EXPERTS, cap + CWIN, d), BF16),
        grid_spec=grid_spec,
        compiler_params=_params(("arbitrary",)),
        name="compact_small",
    )(start, start_next, pos3, u2)


def _ffn_kernel(chunks, x_ref, wg_ref, wu_ref, wd_ref, o_ref, acc_ref):
    xs = x_ref[0]
    for i, (c0, cw) in enumerate(chunks):
        g = _dot(xs, wg_ref[0, :, c0:c0 + cw])
        u = _dot(xs, wu_ref[0, :, c0:c0 + cw])
        h = ((g * jax.nn.sigmoid(g)) * u).astype(BF16)
        part = _dot(h, wd_ref[0, c0:c0 + cw, :])
        if i == 0:
            acc_ref[...] = part
        else:
            acc_ref[...] += part
    o_ref[0] = acc_ref[...].astype(BF16)


def _ffn(xe, cap, wg, wu, wd):
    e, _, d = xe.shape
    f = wg.shape[2]
    tm = min(FFN_TM, cap)
    chunks = tuple((c0, min(512, f - c0)) for c0 in range(0, f, 512))
    return pl.pallas_call(
        functools.partial(_ffn_kernel, chunks),
        out_shape=jax.ShapeDtypeStruct((e, cap, d), BF16),
        grid=(e, cap // tm),
        in_specs=[pl.BlockSpec((1, tm, d), lambda ei, m: (ei, m, 0)),
                  pl.BlockSpec((1, d, f), lambda ei, m: (ei, 0, 0)),
                  pl.BlockSpec((1, d, f), lambda ei, m: (ei, 0, 0)),
                  pl.BlockSpec((1, f, d), lambda ei, m: (ei, 0, 0))],
        out_specs=pl.BlockSpec((1, tm, d), lambda ei, m: (ei, m, 0)),
        scratch_shapes=[pltpu.VMEM((tm, d), F32)],
        compiler_params=_params(("arbitrary", "arbitrary")),
        name="expert_ffn",
    )(xe, wg, wu, wd)


def _uncompact_kernel(nsb, nbig, abig_ref, rel_ref, pos_ref, aff_ref, ye_ref, x1_ref, mod_ref, nf_ref,
                      out_ref, acc_ref):
    b = pl.program_id(0)
    e = pl.program_id(1)

    @pl.when(e == 0)
    def _():
        acc_ref[...] = jnp.zeros_like(acc_ref)

    rowi = lax.broadcasted_iota(jnp.int32, (WIN, SUB), 0)
    big = abig_ref[e * nbig + b]

    def body(s, carry):
        rel = pl.multiple_of(rel_ref[0, 0, s], 16)
        local = pos_ref[s, pl.ds(e, 1), :] - (big + rel)
        gate = aff_ref[s, pl.ds(e, 1), :]
        weights = jnp.where(rowi == local, gate, 0.0).astype(BF16)
        rows = pl.ds(s * SUB, SUB)
        acc_ref[rows, :] += _dot_tn(weights, ye_ref[pl.ds(rel, WIN), :])
        return carry

    for s in range(nsb):
        body(s, 0)

    @pl.when(e == pl.num_programs(1) - 1)
    def _():
        xo = x1_ref[...] + mod_ref[0][5:6] * acc_ref[...]
        ms = jnp.mean(xo * xo, axis=-1, keepdims=True)
        out_ref[...] = xo * lax.rsqrt(ms + EPS) * nf_ref[...]


def _uncompact(ye, pos3, aff3, abig, rel, x1, mod3, boff, seq_len, nf, tbig):
    t, d = x1.shape
    cap = ye.shape[1]
    nbig = t // tbig
    nsb = tbig // SUB
    wbig = tbig + 32
    grid_spec = pltpu.PrefetchScalarGridSpec(
        num_scalar_prefetch=1,
        grid=(nbig, N_EXPERTS),
        in_specs=[
            pl.BlockSpec((1, 1, nsb), lambda b, e, ab: (e * nbig + b, 0, 0), memory_space=pltpu.SMEM),
            pl.BlockSpec((nsb, N_EXPERTS, SUB), lambda b, e, ab: (b, 0, 0)),
            pl.BlockSpec((nsb, N_EXPERTS, SUB), lambda b, e, ab: (b, 0, 0)),
            pl.BlockSpec((pl.Element(wbig), pl.Element(d)),
                         lambda b, e, ab: (pl.multiple_of(e * cap + ab[e * nbig + b], 16), 0)),
            pl.BlockSpec((tbig, d), lambda b, e, ab: (b, 0)),
            pl.BlockSpec((1, 6, d), lambda b, e, ab: (boff + (b * tbig) // seq_len, 0, 0)),
            pl.BlockSpec((1, d), lambda b, e, ab: (0, 0)),
        ],
        out_specs=pl.BlockSpec((tbig, d), lambda b, e, ab: (b, 0)),
        scratch_shapes=[pltpu.VMEM((tbig, d), F32)],
    )
    return pl.pallas_call(
        functools.partial(_uncompact_kernel, nsb, nbig),
        out_shape=jax.ShapeDtypeStruct((t, d), F32),
        grid_spec=grid_spec,
        compiler_params=_params(("arbitrary", "arbitrary")),
        name="uncompact",
    )(abig, rel.reshape(N_EXPERTS * nbig, 1, nsb), pos3, aff3, ye.reshape(N_EXPERTS * cap, d), x1, mod3, nf)


def _uncompact_small_kernel(nsub, ab_ref, rel_ref, pos_ref, aff_ref, *refs):
    ye_refs = refs[:N_EXPERTS]
    x1_ref, mod_ref, nf_ref, out_ref = refs[N_EXPERTS:]
    p = pl.program_id(0)
    npair = nsub // GROUP
    rowi = lax.broadcasted_iota(jnp.int32, (SMALL_WIN, SUB), 0)
    ys = []
    for j in range(GROUP):
        pos = pos_ref[j]
        aff = aff_ref[j]
        weights, windows = [], []
        for e in range(N_EXPERTS):
            rel = pl.multiple_of(rel_ref[e * nsub + GROUP * p + j], 16)
            local = pos[e:e + 1] - (ab_ref[e * npair + p] + rel)
            weights.append(jnp.where(rowi == local, aff[e:e + 1], 0.0).astype(BF16))
            windows.append(ye_refs[e][pl.ds(rel, SMALL_WIN), :])
        ys.append(_dot_tn(jnp.concatenate(weights, axis=0), jnp.concatenate(windows, axis=0)))
    xo = x1_ref[...] + mod_ref[0][5:6] * jnp.concatenate(ys, axis=0)
    ms = jnp.mean(xo * xo, axis=-1, keepdims=True)
    out_ref[...] = xo * lax.rsqrt(ms + EPS) * nf_ref[...]


def _uncompact_small(ye, pos3, aff3, ab, rel, x1, mod3, boff, seq_len, nf):
    t, d = x1.shape
    cap = ye.shape[1]
    nsub = t // SUB
    npair = nsub // GROUP

    def window(e):
        return pl.BlockSpec((pl.Element(GROUP_WIN), pl.Element(d)),
                            lambda p, ab_ref, rel_ref: (pl.multiple_of(e * cap + ab_ref[e * npair + p], 16), 0))

    grid_spec = pltpu.PrefetchScalarGridSpec(
        num_scalar_prefetch=2,
        grid=(npair,),
        in_specs=[pl.BlockSpec((GROUP, N_EXPERTS, SUB), lambda p, ab_ref, rel_ref: (p, 0, 0)),
                  pl.BlockSpec((GROUP, N_EXPERTS, SUB), lambda p, ab_ref, rel_ref: (p, 0, 0))]
        + [window(e) for e in range(N_EXPERTS)]
        + [pl.BlockSpec((GROUP * SUB, d), lambda p, ab_ref, rel_ref: (p, 0)),
           pl.BlockSpec((1, 6, d), lambda p, ab_ref, rel_ref: (boff + (p * GROUP * SUB) // seq_len, 0, 0)),
           pl.BlockSpec((1, d), lambda p, ab_ref, rel_ref: (0, 0))],
        out_specs=pl.BlockSpec((GROUP * SUB, d), lambda p, ab_ref, rel_ref: (p, 0)),
    )
    ye2 = ye.reshape(N_EXPERTS * cap, d)
    return pl.pallas_call(
        functools.partial(_uncompact_small_kernel, nsub),
        out_shape=jax.ShapeDtypeStruct((t, d), F32),
        grid_spec=grid_spec,
        compiler_params=_params(("arbitrary",)),
        name="uncompact_small",
    )(ab, rel, pos3, aff3, *([ye2] * N_EXPERTS), x1, mod3, nf)


def _floor16(v):
    return (v // 16) * 16


def _trunk(x, mod3, boff, w, experts):
    b, l, d = x.shape
    t = b * l
    cap = EC_FACTOR * t // N_EXPERTS

    if experts[0].dtype == BF16:
        z = _in_proj(x, mod3, boff, w["norm1_w"], w["w_in"])
    else:
        z, *experts = _in_proj(x, mod3, boff, w["norm1_w"], w["w_in"], experts)
    o = _gla(z, w["wpre_f"], w["bpre_f"], False, None)
    o = _gla(z, w["wpre_b"], w["bpre_b"], True, o)
    x1, u2, aff3 = _mix_out(x, o, z, mod3, boff, w["conv_w"], w["conv_b"],
                            w["w_out_a"], w["w_out_b"], w["w_o"], w["norm2_w"], w["w_router_t"])
    pos3, base3 = _select(aff3, cap)

    base = base3[:, :, 0].T
    start = jnp.minimum(_floor16(base), cap - WIN)
    tbig_c = min(2048, t)
    tbig_u = min(1024, t)
    while tbig_u + 32 > cap:
        tbig_u //= 2
    nsb_u = tbig_u // SUB
    abig = jnp.minimum(_floor16(base[:, ::nsb_u]), cap - (tbig_u + 32))
    rel = start - jnp.repeat(abig, nsb_u, axis=1)

    base_next = jnp.concatenate([base[:, 1:], jnp.full((N_EXPERTS, 1), cap, jnp.int32)], axis=1)
    start_small = jnp.minimum(_floor16(base), cap - SMALL_WIN)
    ab_pair = jnp.minimum(_floor16(base[:, ::GROUP]), cap - GROUP_WIN)
    rel_small = start_small - jnp.repeat(ab_pair, GROUP, axis=1)
    fits_small = (jnp.all(base_next - _floor16(base) <= SMALL_WIN)
                  & jnp.all(rel_small <= GROUP_WIN - SMALL_WIN))

    u2f = u2.reshape(t, d)
    start_blk = _floor16(base[:, ::2])
    next_blk = base_next[:, 1::2]
    xe = lax.cond(
        jnp.all(next_blk - start_blk <= CWIN),
        lambda: _compact_small(u2f, pos3, start_blk.reshape(-1), _floor16(next_blk).reshape(-1), cap),
        lambda: _compact(u2f, pos3, start, cap, tbig_c))
    ye = _ffn(xe, cap, *experts)
    x1f = x1.reshape(t, d)
    out = lax.cond(
        fits_small,
        lambda: _uncompact_small(ye, pos3, aff3, ab_pair.reshape(-1), rel_small.reshape(-1), x1f, mod3, boff,
                                 l, w["norm_f_w"]),
        lambda: _uncompact(ye, pos3, aff3, abig.reshape(-1), rel, x1f, mod3, boff, l, w["norm_f_w"], tbig_u))
    return out.reshape(b, l, d), tuple(experts)


def _prepare_weights(w_in, w_alpha_f, b_alpha_f, w_alpha_b, b_alpha_b, gla_norm_w, conv_w, conv_b,
                     w_out_a, w_out_b, w_o, norm1_w, norm2_w, w_router, norm_f_w):
    d = D_MODEL
    wq, wk, wv, wr, wlf, wlb, wcb, wcc, wcx, wga, wgb = jnp.split(
        w_in, np.cumsum([512, 512, 1024, 1024, 16, 16, 1024, 1024, 1024, 1024])[:].tolist(), axis=1)
    pad = jnp.zeros((d, Z_WIDTH - Z_LR - 2 * GATE_RANK), w_in.dtype)
    wq = wq * (HEAD_K ** -0.5)
    w_in_p = jnp.concatenate([wq, wk, wv, wr, wcb, wcc, wcx, wga, wgb, wlf, wlb, pad], axis=1).astype(BF16)

    def pre(wa, ba, row0):
        wh = wa.reshape(GATE_RANK, GLA_HEADS, HEAD_K).transpose(1, 0, 2)
        full = jnp.zeros((GLA_HEADS, 128, HEAD_K), F32).at[:, row0:row0 + GATE_RANK].set(wh)
        return full.astype(BF16), ba.reshape(GLA_HEADS, 1, HEAD_K)

    wpre_f, bpre_f = pre(w_alpha_f, b_alpha_f, 0)
    wpre_b, bpre_b = pre(w_alpha_b, b_alpha_b, GATE_RANK)
    return dict(
        w_in=w_in_p, wpre_f=wpre_f, bpre_f=bpre_f, wpre_b=wpre_b, bpre_b=bpre_b,
        conv_w=conv_w, conv_b=conv_b.reshape(1, d),
        w_out_a=(gla_norm_w.reshape(d, 1) * w_out_a).astype(BF16),
        w_out_b=w_out_b.astype(BF16), w_o=w_o.astype(BF16),
        norm1_w=norm1_w, norm2_w=norm2_w.reshape(1, d), w_router_t=w_router.T.astype(BF16),
        norm_f_w=norm_f_w.reshape(1, d))


def kernel(x_prompt, x_sample, c_prompt, c_sample, w_ada, b_ada, norm1_w, w_in, w_alpha_f, b_alpha_f, w_alpha_b, b_alpha_b, gla_norm_w, conv_w, conv_b, w_out_a, w_out_b, w_o, norm2_w, w_router, w_gate, w_up, w_down, norm_f_w):
    d = D_MODEL
    bp, bs = c_prompt.shape[0], c_sample.shape[0]
    nrows = -(-(bp + bs) // 8) * 8
    c_all = jnp.concatenate([c_prompt, c_sample, jnp.zeros((nrows - bp - bs, d), F32)], axis=0)
    mod3 = _adaln(c_all, w_ada[0], b_ada[0]).reshape(nrows, 6, d)
    w = _prepare_weights(w_in[0], w_alpha_f[0], b_alpha_f[0], w_alpha_b[0], b_alpha_b[0], gla_norm_w[0],
                         conv_w[0], conv_b[0], w_out_a[0], w_out_b[0], w_o[0], norm1_w[0], norm2_w[0],
                         w_router[0], norm_f_w)
    y_prompt, experts = _trunk(x_prompt, mod3, 0, w, (w_gate[0], w_up[0], w_down[0]))
    y_sample, _ = _trunk(x_sample, mod3, bp, w, experts)
    return (y_prompt, y_sample)
```

```python
import functools

import numpy as np
import jax
import jax.numpy as jnp
from jax import lax
from jax.experimental import pallas as pl
from jax.experimental.pallas import tpu as pltpu

F32 = jnp.float32
BF16 = jnp.bfloat16

D_MODEL = 1024
GLA_HEADS = 4
HEAD_K = 128
HEAD_V = 256
GATE_RANK = 16
GATE_TAU = 16.0
CHUNK = 64
N_EXPERTS = 16
EC_FACTOR = 2
D_EXPERT = 2816
EPS = 1e-6
LOG2E = 1.4426950408889634

Z_R, Z_CB, Z_CC, Z_CX, Z_GA, Z_GB, Z_LR = 2048, 3072, 4096, 5120, 6144, 7168, 8192
HEAD_W = 2 * HEAD_K + HEAD_V
Z_WIDTH = 8448

UNIT = 4 * CHUNK
SUB = 128
WIN = SUB + 16
SMALL_WIN = 64
GROUP = 4
GROUP_WIN = SMALL_WIN + 48 * (GROUP - 1)
CWIN = 96
FFN_TM = 1024
VMEM_LIMIT = 56 * 1024 * 1024


def _dot(a, b):
    return jnp.dot(a, b, preferred_element_type=F32)


def _dot_nt(a, b):
    return lax.dot_general(a, b, (((1,), (1,)), ((), ())), preferred_element_type=F32)


def _sigmoid(x):
    return 0.5 * jnp.tanh(0.5 * x) + 0.5


def _dot_tn(a, b):
    return lax.dot_general(a, b, (((0,), (0,)), ((), ())), preferred_element_type=F32)


def _params(sem, limit=VMEM_LIMIT):
    return pltpu.CompilerParams(dimension_semantics=sem, vmem_limit_bytes=limit)


def _adaln_kernel(c_ref, w_ref, b_ref, o_ref):
    c = c_ref[...]
    s = c * jax.nn.sigmoid(c)
    o_ref[...] = jnp.dot(s, w_ref[...], preferred_element_type=F32,
                         precision=lax.Precision.HIGHEST) + b_ref[...]


def _adaln(c_all, w_ada, b_ada):
    nb, d = c_all.shape
    n = w_ada.shape[1]
    tn = 1536
    return pl.pallas_call(
        _adaln_kernel,
        out_shape=jax.ShapeDtypeStruct((nb, n), F32),
        grid=(n // tn,),
        in_specs=[pl.BlockSpec((nb, d), lambda j: (0, 0)),
                  pl.BlockSpec((d, tn), lambda j: (0, j)),
                  pl.BlockSpec((1, tn), lambda j: (0, j))],
        out_specs=pl.BlockSpec((nb, tn), lambda j: (0, j)),
        compiler_params=_params(("arbitrary",)),
        name="adaln",
    )(c_all, w_ada, b_ada.reshape(1, n))


CAST_SPLIT = 8


def _inproj_kernel(ncast, x_ref, mod_ref, n1_ref, w_ref, *refs):
    z_ref = refs[len(refs) // 2]
    x = x_ref[0]
    ms = jnp.mean(x * x, axis=-1, keepdims=True)
    m = mod_ref[0]
    u = (x * lax.rsqrt(ms + EPS)) * (n1_ref[...] * (1.0 + m[1:2])) + m[0:1]
    z_ref[0] = _dot(u.astype(BF16), w_ref[...]).astype(BF16)

    if ncast:
        lin = ((pl.program_id(0) * pl.num_programs(1) + pl.program_id(1)) * pl.num_programs(2)
               + pl.program_id(2))

        @pl.when(lin < ncast)
        def _():
            for src, dst in zip(refs[:3], refs[4:]):
                dst[...] = src[...].astype(BF16)


def _in_proj(x, mod3, boff, norm1_w, w_in_p, expert_weights=None):
    b, l, d = x.shape
    tm = min(1024, l)
    tn = 2816
    nt, nn = l // tm, Z_WIDTH // tn
    in_specs = [pl.BlockSpec((1, tm, d), lambda bi, i, n: (bi, i, 0)),
                pl.BlockSpec((1, 6, d), lambda bi, i, n: (bi + boff, 0, 0)),
                pl.BlockSpec((1, d), lambda bi, i, n: (0, 0)),
                pl.BlockSpec((d, tn), lambda bi, i, n: (0, n))]
    out_specs = [pl.BlockSpec((1, tm, tn), lambda bi, i, n: (bi, i, n))]
    out_shape = [jax.ShapeDtypeStruct((b, l, Z_WIDTH), BF16)]
    args = [x, mod3, norm1_w.reshape(1, d), w_in_p]
    ncast = 0
    if expert_weights is not None and N_EXPERTS * CAST_SPLIT > b * nt * nn:
        return (_in_proj(x, mod3, boff, norm1_w, w_in_p),) + tuple(wt.astype(BF16) for wt in expert_weights)
    if expert_weights is not None:
        ncast = N_EXPERTS * CAST_SPLIT

        def slab(bi, i, n):
            j = jnp.minimum((bi * nt + i) * nn + n, ncast - 1)
            return (j // CAST_SPLIT, j % CAST_SPLIT, 0)

        for wt in expert_weights:
            blk = (1, wt.shape[1] // CAST_SPLIT, wt.shape[2])
            in_specs.append(pl.BlockSpec(blk, slab))
            out_specs.append(pl.BlockSpec(blk, slab))
            out_shape.append(jax.ShapeDtypeStruct(wt.shape, BF16))
            args.append(wt)
    res = pl.pallas_call(
        functools.partial(_inproj_kernel, ncast),
        out_shape=tuple(out_shape),
        grid=(b, nt, nn),
        in_specs=in_specs,
        out_specs=tuple(out_specs),
        compiler_params=_params(("arbitrary", "arbitrary", "arbitrary")),
        name="in_proj",
    )(*args)
    return res if expert_weights is not None else res[0]


def _gla_constants(reverse):
    r = np.arange(UNIT)[:, None]
    s = np.arange(UNIT)[None, :]
    cr, cs = r // CHUNK, s // CHUNK
    if reverse:
        tri = (cr == cs) & (s >= r)
        tr, ts = 3 - cr, 3 - cs
    else:
        tri = (cr == cs) & (s <= r)
        tr, ts = cr, cs
    code = np.zeros((UNIT, UNIT), np.int32)
    code[tri] = 1
    return jnp.asarray(tri, BF16), jnp.asarray(code)


def _gla_kernel(reverse, nunits, has_prev, *refs):
    if has_prev:
        qkv_ref, lr_ref, wp_ref, bp_ref, tri_ref, code_ref, prev_ref, o_ref, s_scr = refs
    else:
        qkv_ref, lr_ref, wp_ref, bp_ref, tri_ref, code_ref, o_ref, s_scr = refs
        prev_ref = None

    def v_of(rows):
        return qkv_ref[0, rows, 2 * HEAD_K:HEAD_W]

    @pl.when(pl.program_id(2) == 0)
    def _():
        s_scr[...] = jnp.zeros_like(s_scr)

    order = (3, 2, 1, 0) if reverse else (0, 1, 2, 3)

    def to_mem(by_step):
        out = [None] * 4
        for t in range(4):
            out[order[t]] = by_step[t]
        return out

    def scale_chunks(arr, logf):
        parts = []
        for c in range(4):
            part = arr[c * CHUNK:(c + 1) * CHUNK]
            if logf[c] is not None:
                part = part * jnp.exp2(logf[c])
            parts.append(part)
        return jnp.concatenate(parts, axis=0).astype(BF16)

    def rows_of(ui):
        um = (nunits - 1 - ui) if reverse else ui
        return pl.ds(um * UNIT, UNIT)

    def decays(ui):
        pre = _dot(lr_ref[0, rows_of(ui), :], wp_ref[0]) + bp_ref[0]
        soft = jnp.log2(1.0 + jnp.exp2(jnp.abs(pre) * (-LOG2E)))
        la = (jnp.minimum(pre, 0.0) * LOG2E - soft) * (1.0 / GATE_TAU)
        hi = la.astype(BF16)
        lo = (la - hi.astype(F32)).astype(BF16)
        bb = _dot(tri_ref[...], jnp.concatenate([hi, lo], axis=1))
        return bb[:, :HEAD_K] + bb[:, HEAD_K:]

    def operands(ui, b):
        rows = rows_of(ui)
        edge = 0 if reverse else CHUNK - 1
        tot = [b[c * CHUNK + edge:c * CHUNK + edge + 1] for c in range(4)]
        tot_full = jnp.concatenate([jnp.broadcast_to(t, (CHUNK, HEAD_K)) for t in tot], axis=0)
        q = qkv_ref[0, rows, 0:HEAD_K].astype(F32)
        k = qkv_ref[0, rows, HEAD_K:2 * HEAD_K].astype(F32)
        q_t = q * jnp.exp2(b)
        k_t = (k * jnp.exp2(-b)).astype(BF16)
        k_e = k * jnp.exp2(tot_full - b)
        q_tb = q_t.astype(BF16)
        k_eb = k_e.astype(BF16)
        bt = [tot[o] for o in order]

        zero = jnp.zeros((CHUNK, HEAD_K), BF16)
        e1 = jnp.exp2(bt[1])
        e2 = jnp.exp2(bt[2])
        qrows, krows = [None] * 4, [None] * 4
        for t in range(4):
            c = order[t]
            sl = slice(c * CHUNK, (c + 1) * CHUNK)
            qc, kc = q_tb[sl], k_eb[sl]
            q3 = qc if t == 2 else ((q_t[sl] * e2).astype(BF16) if t == 3 else zero)
            k3 = (k_e[sl] * e1).astype(BF16) if t == 0 else (kc if t == 1 else zero)
            qrows[c] = jnp.concatenate([qc if t == 1 else zero, qc if t == 3 else zero, q3], axis=1)
            krows[c] = jnp.concatenate([kc if t == 0 else zero, kc if t == 2 else zero, k3], axis=1)

        gq = to_mem([None, bt[0], bt[0] + bt[1], bt[0] + bt[1] + bt[2]])
        gk = to_mem([bt[1] + bt[2] + bt[3], bt[2] + bt[3], bt[3], None])
        dec = jnp.exp2(bt[0] + bt[1] + bt[2] + bt[3])
        return dict(q_t=q_tb, k_t=k_t, q_cat=jnp.concatenate(qrows, axis=0), k_cat=jnp.concatenate(krows, axis=0),
                    qs=scale_chunks(q_t, gq), ks=scale_chunks(k_e, gk), dec=dec)

    def scores(p):
        same = _dot_nt(p["q_t"], p["k_t"])
        cross = _dot_nt(p["q_cat"], p["k_cat"])
        return jnp.where(code_ref[...] == 1, same, cross).astype(BF16)

    def finish(ui, p, intra):
        rows = rows_of(ui)
        st = s_scr[...]
        o = intra + _dot_nt(p["qs"], st.astype(BF16))
        s_scr[...] = st * p["dec"] + _dot_tn(v_of(rows), p["ks"])
        if prev_ref is not None:
            o = o + prev_ref[0, rows, :]
        o_ref[0, rows, :] = o

    stages = 5
    bs, ps, atts, intra = {}, {}, {}, {}
    for wave in range(nunits + stages - 1):
        for k in range(stages):
            ui = wave - k
            if not 0 <= ui < nunits:
                continue
            if k == 0:
                bs[ui] = decays(ui)
            elif k == 1:
                ps[ui] = operands(ui, bs.pop(ui))
            elif k == 2:
                atts[ui] = scores(ps[ui])
            elif k == 3:
                intra[ui] = _dot(atts.pop(ui), v_of(rows_of(ui)))
            else:
                finish(ui, ps.pop(ui), intra.pop(ui))


def _gla(z, wpre, bpre, reverse, prev):
    b, l, _ = z.shape
    lb = min(2048, l)
    nb = l // lb
    nunits = lb // UNIT
    tri, code = _gla_constants(reverse)

    def seq(i):
        return (nb - 1 - i) if reverse else i

    in_specs = [
        pl.BlockSpec((1, lb, HEAD_W), lambda bi, h, i: (bi, seq(i), h)),
        pl.BlockSpec((1, lb, 128), lambda bi, h, i: (bi, seq(i), Z_LR // 128)),
        pl.BlockSpec((1, 128, HEAD_K), lambda bi, h, i: (h, 0, 0)),
        pl.BlockSpec((1, 1, HEAD_K), lambda bi, h, i: (h, 0, 0)),
        pl.BlockSpec((UNIT, UNIT), lambda bi, h, i: (0, 0)),
        pl.BlockSpec((UNIT, UNIT), lambda bi, h, i: (0, 0)),
    ]
    args = [z, z, wpre, bpre, tri, code]
    if prev is not None:
        in_specs.append(pl.BlockSpec((1, lb, HEAD_V), lambda bi, h, i: (bi, seq(i), h)))
        args.append(prev)
    return pl.pallas_call(
        functools.partial(_gla_kernel, reverse, nunits, prev is not None),
        out_shape=jax.ShapeDtypeStruct((b, l, GLA_HEADS * HEAD_V), F32),
        grid=(b, GLA_HEADS, nb),
        in_specs=in_specs,
        out_specs=pl.BlockSpec((1, lb, HEAD_V), lambda bi, h, i: (bi, seq(i), h)),
        scratch_shapes=[pltpu.VMEM((HEAD_V, HEAD_K), F32)],
        compiler_params=_params(("arbitrary", "arbitrary", "arbitrary")),
        name="gla_bwd" if reverse else "gla_fwd",
    )(*args)


def _mix_kernel(tm, x_ref, o_ref, r_ref, cb_ref, cc_ref, cx_ref, ccp_ref, cxp_ref, ccn_ref, cxn_ref,
                ga_ref, gb_ref, mod_ref, cw_ref, cbias_ref, woa_ref, wob_ref, wo_ref,
                n2_ref, wrt_ref, x1_ref, u2_ref, aff_ref):
    i = pl.program_id(1)
    last = pl.num_programs(1) - 1

    o = o_ref[0]
    parts = []
    for h in range(GLA_HEADS):
        oh = o[:, h * HEAD_V:(h + 1) * HEAD_V]
        ms = jnp.mean(oh * oh, axis=-1, keepdims=True)
        parts.append(oh * lax.rsqrt(ms + EPS))
    on = jnp.concatenate(parts, axis=1)
    r = r_ref[0]
    branch_a = _dot(on.astype(BF16) * (r * _sigmoid(r)), woa_ref[...])

    xc = cc_ref[0].astype(F32) * cx_ref[0].astype(F32)
    prev_row = ccp_ref[0, 15:16, :].astype(F32) * cxp_ref[0, 15:16, :].astype(F32)
    next_row = ccn_ref[0, 0:1, :].astype(F32) * cxn_ref[0, 0:1, :].astype(F32)
    prev_row = jnp.where(i > 0, prev_row, 0.0)
    next_row = jnp.where(i < last, next_row, 0.0)
    rowid = lax.broadcasted_iota(jnp.int32, (tm, 1), 0)
    xm1 = jnp.where(rowid == 0, prev_row, pltpu.roll(xc, 1, axis=0))
    xp1 = jnp.where(rowid == tm - 1, next_row, pltpu.roll(xc, tm - 1, axis=0))
    cw = cw_ref[...]
    conv = xm1 * cw[0:1] + xc * cw[1:2] + xp1 * cw[2:3] + cbias_ref[...]
    branch_b = _dot((cb_ref[0].astype(F32) * conv).astype(BF16), wob_ref[...])

    merged = (_sigmoid(ga_ref[0]) * branch_a.astype(BF16)
              + _sigmoid(gb_ref[0]) * branch_b.astype(BF16))
    mix = _dot(merged, wo_ref[...])

    m = mod_ref[0]
    x1 = x_ref[0] + m[2:3] * mix
    x1_ref[0] = x1
    ms = jnp.mean(x1 * x1, axis=-1, keepdims=True)
    u2 = (x1 * lax.rsqrt(ms + EPS)) * (n2_ref[...] * (1.0 + m[4:5])) + m[3:4]
    u2b = u2.astype(BF16)
    u2_ref[0] = u2b

    logits = _dot_nt(wrt_ref[...], u2b)
    ex = jnp.exp(logits - jnp.max(logits, axis=0, keepdims=True))
    aff = ex / jnp.sum(ex, axis=0, keepdims=True)
    for j in range(tm // SUB):
        aff_ref[j] = aff[:, j * SUB:(j + 1) * SUB]


def _mix_out(x, o, z, mod3, boff, conv_w, conv_b, woa, wob, wo, n2, wrt):
    b, l, d = x.shape
    tm = min(512, l)
    nt = l // tm
    hb = tm // 16
    nhalo = l // 16

    def col(c):
        return lambda bi, i: (bi, i, c // d)

    in_specs = [
        pl.BlockSpec((1, tm, d), lambda bi, i: (bi, i, 0)),
        pl.BlockSpec((1, tm, d), lambda bi, i: (bi, i, 0)),
        pl.BlockSpec((1, tm, d), col(Z_R)),
        pl.BlockSpec((1, tm, d), col(Z_CB)),
        pl.BlockSpec((1, tm, d), col(Z_CC)),
        pl.BlockSpec((1, tm, d), col(Z_CX)),
        pl.BlockSpec((1, 16, d), lambda bi, i: (bi, jnp.maximum(i * hb - 1, 0), Z_CC // d)),
        pl.BlockSpec((1, 16, d), lambda bi, i: (bi, jnp.maximum(i * hb - 1, 0), Z_CX // d)),
        pl.BlockSpec((1, 16, d), lambda bi, i: (bi, jnp.minimum((i + 1) * hb, nhalo - 1), Z_CC // d)),
        pl.BlockSpec((1, 16, d), lambda bi, i: (bi, jnp.minimum((i + 1) * hb, nhalo - 1), Z_CX // d)),
        pl.BlockSpec((1, tm, d), col(Z_GA)),
        pl.BlockSpec((1, tm, d), col(Z_GB)),
        pl.BlockSpec((1, 6, d), lambda bi, i: (bi + boff, 0, 0)),
        pl.BlockSpec((3, d), lambda bi, i: (0, 0)),
        pl.BlockSpec((1, d), lambda bi, i: (0, 0)),
        pl.BlockSpec((d, d), lambda bi, i: (0, 0)),
        pl.BlockSpec((d, d), lambda bi, i: (0, 0)),
        pl.BlockSpec((d, d), lambda bi, i: (0, 0)),
        pl.BlockSpec((1, d), lambda bi, i: (0, 0)),
        pl.BlockSpec((N_EXPERTS, d), lambda bi, i: (0, 0)),
    ]
    out_shape = (jax.ShapeDtypeStruct((b, l, d), F32),
                 jax.ShapeDtypeStruct((b, l, d), BF16),
                 jax.ShapeDtypeStruct((b * l // SUB, N_EXPERTS, SUB), F32))
    out_specs = (pl.BlockSpec((1, tm, d), lambda bi, i: (bi, i, 0)),
                 pl.BlockSpec((1, tm, d), lambda bi, i: (bi, i, 0)),
                 pl.BlockSpec((tm // SUB, N_EXPERTS, SUB), lambda bi, i: (bi * nt + i, 0, 0)))
    return pl.pallas_call(
        functools.partial(_mix_kernel, tm),
        out_shape=out_shape,
        grid=(b, nt),
        in_specs=in_specs,
        out_specs=out_specs,
        compiler_params=_params(("arbitrary", "arbitrary")),
        name="mix_out",
    )(x, o, z, z, z, z, z, z, z, z, z, z, mod3, conv_w, conv_b, woa, wob, wo, n2, wrt)


def _select_kernel(cap, nsb, aff_ref, u_ref, pos_ref, base_ref):
    capf = float(cap)

    def bits(j):
        return pltpu.bitcast(aff_ref[j], jnp.int32)

    def count(pred):
        def body(j, acc):
            return acc + jnp.where(pred(bits(j)), 1.0, 0.0)
        acc = lax.fori_loop(0, nsb, body, jnp.zeros((N_EXPERTS, SUB), F32), unroll=8)
        return jnp.sum(acc, axis=1, keepdims=True)

    def bit_body(t, thr):
        cand = thr | jnp.left_shift(jnp.int32(1), 30 - t)
        return jnp.where(count(lambda v: v >= cand) >= capf, cand, thr)

    thr = lax.fori_loop(0, 31, bit_body, jnp.zeros((N_EXPERTS, SUB), jnp.int32))
    need = capf - count(lambda v: v > thr)

    def blk(j, carry):
        ceq, csel = carry
        v = bits(j)
        gtf = jnp.where(v > thr, 1.0, 0.0)
        eqf = jnp.where(v == thr, 1.0, 0.0)
        r1 = _dot(eqf.astype(BF16), u_ref[...])
        eq_rank = r1[:, :SUB] - eqf + ceq
        sel = gtf + jnp.where(eq_rank < need, eqf, 0.0)
        r2 = _dot(sel.astype(BF16), u_ref[...])
        pos_ref[j] = jnp.where(sel > 0.5, r2[:, :SUB] + csel - 1.0, -1.0).astype(jnp.int32)
        base_ref[j] = csel.astype(jnp.int32)
        return ceq + r1[:, SUB:], csel + r2[:, SUB:]

    zero = jnp.zeros((N_EXPERTS, SUB), F32)
    lax.fori_loop(0, nsb, blk, (zero, zero), unroll=8)


def _select(aff3, cap):
    nsb = aff3.shape[0]
    tri = np.triu(np.ones((SUB, SUB), np.float32))
    umat = jnp.asarray(np.concatenate([tri, np.ones((SUB, SUB), np.float32)], axis=1), BF16)
    shp = jax.ShapeDtypeStruct((nsb, N_EXPERTS, SUB), jnp.int32)
    return pl.pallas_call(
        functools.partial(_select_kernel, cap, nsb),
        out_shape=(shp, shp),
        compiler_params=pltpu.CompilerParams(vmem_limit_bytes=VMEM_LIMIT),
        name="select",
    )(aff3, umat)


def _compact_kernel(nsb, start_ref, pos_ref, x_ref, xe_ref):
    e = pl.program_id(0)

    @pl.when(pl.program_id(1) == 0)
    def _():
        xe_ref[...] = jnp.zeros_like(xe_ref)

    rowi = lax.broadcasted_iota(jnp.int32, (WIN, SUB), 0)

    def body(s, carry):
        st = pl.multiple_of(start_ref[0, 0, s], 16)
        local = pos_ref[s, pl.ds(e, 1), :] - st
        onehot = jnp.where(rowi == local, 1.0, 0.0).astype(BF16)
        xs = x_ref[pl.ds(s * SUB, SUB), :]
        rows = _dot(onehot, xs).astype(BF16)
        xe_ref[0, pl.ds(st, WIN), :] = xe_ref[0, pl.ds(st, WIN), :] + rows
        return carry

    for s in range(nsb):
        body(s, 0)


def _compact(u2, pos3, start, cap, tbig):
    t, d = u2.shape
    nbig = t // tbig
    nsb = tbig // SUB
    return pl.pallas_call(
        functools.partial(_compact_kernel, nsb),
        out_shape=jax.ShapeDtypeStruct((N_EXPERTS, cap + CWIN, d), BF16),
        grid=(N_EXPERTS, nbig),
        in_specs=[pl.BlockSpec((1, 1, nsb), lambda e, b: (e * nbig + b, 0, 0), memory_space=pltpu.SMEM),
                  pl.BlockSpec((nsb, N_EXPERTS, SUB), lambda e, b: (b, 0, 0)),
                  pl.BlockSpec((tbig, d), lambda e, b: (b, 0))],
        out_specs=pl.BlockSpec((1, cap + CWIN, d), lambda e, b: (e, 0, 0)),
        compiler_params=_params(("arbitrary", "arbitrary")),
        name="compact",
    )(start.reshape(N_EXPERTS * nbig, 1, nsb), pos3, u2)


def _compact_small_kernel(nblk, cap, start_ref, next_ref, pos_ref, x_ref, xe_ref, strip_buf, tail_buf, zero_buf,
                          sems, zsem):
    p = pl.program_id(0)
    last = pl.num_programs(0) - 1
    slot = lax.rem(p, 2)

    @pl.when(p == 0)
    def _():
        tail_buf[...] = jnp.zeros_like(tail_buf)
        zero_buf[...] = jnp.zeros_like(zero_buf)

    def strip_copy(sl, e, row0):
        return pltpu.make_async_copy(strip_buf.at[sl, e], xe_ref.at[e, pl.ds(row0, CWIN), :], sems.at[sl, e])

    pos = jnp.concatenate([pos_ref[0], pos_ref[1]], axis=1)
    rowi = lax.broadcasted_iota(jnp.int32, (CWIN, 2 * SUB), 0)
    starts = [pl.multiple_of(start_ref[e * nblk + p], 16) for e in range(N_EXPERTS)]
    onehot = jnp.concatenate(
        [jnp.where(rowi == pos[e:e + 1] - starts[e], 1.0, 0.0).astype(BF16) for e in range(N_EXPERTS)], axis=0)
    rows = _dot(onehot, x_ref[...]).astype(BF16)
    for e in range(N_EXPERTS):
        strip_buf[slot, e, 0:16, :] = rows[e * CWIN:e * CWIN + 16] + tail_buf[e]
        strip_buf[slot, e, 16:CWIN, :] = rows[e * CWIN + 16:(e + 1) * CWIN]
        t0 = next_ref[e * nblk + p] - starts[e]
        t0c = pl.multiple_of(jnp.minimum(t0, CWIN - 16), 16)
        tail_buf[e] = jnp.where(t0 < CWIN, strip_buf[slot, e, pl.ds(t0c, 16), :], jnp.zeros((16, 1), BF16))

    for e in range(N_EXPERTS):
        @pl.when(p > 0)
        def _():
            strip_copy(1 - slot, e, starts[e]).wait()
        strip_copy(slot, e, starts[e]).start()

    @pl.when(p == last)
    def _():
        for e in range(N_EXPERTS):
            strip_copy(slot, e, starts[e]).wait()
            pad = pltpu.make_async_copy(zero_buf, xe_ref.at[e, pl.ds(cap, CWIN), :], zsem)
            pad.start()
            pad.wait()


def _compact_small(u2, pos3, start, start_next, cap):
    t, d = u2.shape
    nblk = t // (2 * SUB)
    grid_spec = pltpu.PrefetchScalarGridSpec(
        num_scalar_prefetch=2,
        grid=(nblk,),
        in_specs=[pl.BlockSpec((2, N_EXPERTS, SUB), lambda p, a, b: (p, 0, 0)),
                  pl.BlockSpec((2 * SUB, d), lambda p, a, b: (p, 0))],
        out_specs=pl.BlockSpec(memory_space=pl.ANY),
        scratch_shapes=[pltpu.VMEM((2, N_EXPERTS, CWIN, d), BF16),
                        pltpu.VMEM((N_EXPERTS, 16, d), BF16),
                        pltpu.VMEM((CWIN, d), BF16),
                        pltpu.SemaphoreType.DMA((2, N_EXPERTS)),
                        pltpu.SemaphoreType.DMA(())],
    )
    return pl.pallas_call(
        functools.partial(_compact_small_kernel, nblk, cap),
        out_shape=jax.ShapeDtypeStruct((N_EXPERTS, cap + CWIN, d), BF16),
        grid_spec=grid_spec,
        compiler_params=_params(("arbitrary",)),
        name="compact_small",
    )(start, start_next, pos3, u2)


def _ffn_kernel(chunks, x_ref, wg_ref, wu_ref, wd_ref, o_ref, acc_ref):
    xs = x_ref[0]
    for i, (c0, cw) in enumerate(chunks):
        g = _dot(xs, wg_ref[0, :, c0:c0 + cw])
        u = _dot(xs, wu_ref[0, :, c0:c0 + cw])
        h = ((g * jax.nn.sigmoid(g)) * u).astype(BF16)
        part = _dot(h, wd_ref[0, c0:c0 + cw, :])
        if i == 0:
            acc_ref[...] = part
        else:
            acc_ref[...] += part
    o_ref[0] = acc_ref[...].astype(BF16)


def _ffn(xe, cap, wg, wu, wd):
    e, _, d = xe.shape
    f = wg.shape[2]
    tm = min(FFN_TM, cap)
    chunks = tuple((c0, min(512, f - c0)) for c0 in range(0, f, 512))
    return pl.pallas_call(
        functools.partial(_ffn_kernel, chunks),
        out_shape=jax.ShapeDtypeStruct((e, cap, d), BF16),
        grid=(e, cap // tm),
        in_specs=[pl.BlockSpec((1, tm, d), lambda ei, m: (ei, m, 0)),
                  pl.BlockSpec((1, d, f), lambda ei, m: (ei, 0, 0)),
                  pl.BlockSpec((1, d, f), lambda ei, m: (ei, 0, 0)),
                  pl.BlockSpec((1, f, d), lambda ei, m: (ei, 0, 0))],
        out_specs=pl.BlockSpec((1, tm, d), lambda ei, m: (ei, m, 0)),
        scratch_shapes=[pltpu.VMEM((tm, d), F32)],
        compiler_params=_params(("arbitrary", "arbitrary")),
        name="expert_ffn",
    )(xe, wg, wu, wd)


def _uncompact_kernel(nsb, nbig, abig_ref, rel_ref, pos_ref, aff_ref, ye_ref, x1_ref, mod_ref, nf_ref,
                      out_ref, acc_ref):
    b = pl.program_id(0)
    e = pl.program_id(1)

    @pl.when(e == 0)
    def _():
        acc_ref[...] = jnp.zeros_like(acc_ref)

    rowi = lax.broadcasted_iota(jnp.int32, (WIN, SUB), 0)
    big = abig_ref[e * nbig + b]

    def body(s, carry):
        rel = pl.multiple_of(rel_ref[0, 0, s], 16)
        local = pos_ref[s, pl.ds(e, 1), :] - (big + rel)
        gate = aff_ref[s, pl.ds(e, 1), :]
        weights = jnp.where(rowi == local, gate, 0.0).astype(BF16)
        rows = pl.ds(s * SUB, SUB)
        acc_ref[rows, :] += _dot_tn(weights, ye_ref[pl.ds(rel, WIN), :])
        return carry

    for s in range(nsb):
        body(s, 0)

    @pl.when(e == pl.num_programs(1) - 1)
    def _():
        xo = x1_ref[...] + mod_ref[0][5:6] * acc_ref[...]
        ms = jnp.mean(xo * xo, axis=-1, keepdims=True)
        out_ref[...] = xo * lax.rsqrt(ms + EPS) * nf_ref[...]


def _uncompact(ye, pos3, aff3, abig, rel, x1, mod3, boff, seq_len, nf, tbig):
    t, d = x1.shape
    cap = ye.shape[1]
    nbig = t // tbig
    nsb = tbig // SUB
    wbig = tbig + 32
    grid_spec = pltpu.PrefetchScalarGridSpec(
        num_scalar_prefetch=1,
        grid=(nbig, N_EXPERTS),
        in_specs=[
            pl.BlockSpec((1, 1, nsb), lambda b, e, ab: (e * nbig + b, 0, 0), memory_space=pltpu.SMEM),
            pl.BlockSpec((nsb, N_EXPERTS, SUB), lambda b, e, ab: (b, 0, 0)),
            pl.BlockSpec((nsb, N_EXPERTS, SUB), lambda b, e, ab: (b, 0, 0)),
            pl.BlockSpec((pl.Element(wbig), pl.Element(d)),
                         lambda b, e, ab: (pl.multiple_of(e * cap + ab[e * nbig + b], 16), 0)),
            pl.BlockSpec((tbig, d), lambda b, e, ab: (b, 0)),
            pl.BlockSpec((1, 6, d), lambda b, e, ab: (boff + (b * tbig) // seq_len, 0, 0)),
            pl.BlockSpec((1, d), lambda b, e, ab: (0, 0)),
        ],
        out_specs=pl.BlockSpec((tbig, d), lambda b, e, ab: (b, 0)),
        scratch_shapes=[pltpu.VMEM((tbig, d), F32)],
    )
    return pl.pallas_call(
        functools.partial(_uncompact_kernel, nsb, nbig),
        out_shape=jax.ShapeDtypeStruct((t, d), F32),
        grid_spec=grid_spec,
        compiler_params=_params(("arbitrary", "arbitrary")),
        name="uncompact",
    )(abig, rel.reshape(N_EXPERTS * nbig, 1, nsb), pos3, aff3, ye.reshape(N_EXPERTS * cap, d), x1, mod3, nf)


def _uncompact_small_kernel(nsub, ab_ref, rel_ref, pos_ref, aff_ref, *refs):
    ye_refs = refs[:N_EXPERTS]
    x1_ref, mod_ref, nf_ref, out_ref = refs[N_EXPERTS:]
    p = pl.program_id(0)
    npair = nsub // GROUP
    rowi = lax.broadcasted_iota(jnp.int32, (SMALL_WIN, SUB), 0)
    ys = []
    for j in range(GROUP):
        pos = pos_ref[j]
        aff = aff_ref[j]
        weights, windows = [], []
        for e in range(N_EXPERTS):
            rel = pl.multiple_of(rel_ref[e * nsub + GROUP * p + j], 16)
            local = pos[e:e + 1] - (ab_ref[e * npair + p] + rel)
            weights.append(jnp.where(rowi == local, aff[e:e + 1], 0.0).astype(BF16))
            windows.append(ye_refs[e][pl.ds(rel, SMALL_WIN), :])
        ys.append(_dot_tn(jnp.concatenate(weights, axis=0), jnp.concatenate(windows, axis=0)))
    xo = x1_ref[...] + mod_ref[0][5:6] * jnp.concatenate(ys, axis=0)
    ms = jnp.mean(xo * xo, axis=-1, keepdims=True)
    out_ref[...] = xo * lax.rsqrt(ms + EPS) * nf_ref[...]


def _uncompact_small(ye, pos3, aff3, ab, rel, x1, mod3, boff, seq_len, nf):
    t, d = x1.shape
    cap = ye.shape[1]
    nsub = t // SUB
    npair = nsub // GROUP

    def window(e):
        return pl.BlockSpec((pl.Element(GROUP_WIN), pl.Element(d)),
                            lambda p, ab_ref, rel_ref: (pl.multiple_of(e * cap + ab_ref[e * npair + p], 16), 0))

    grid_spec = pltpu.PrefetchScalarGridSpec(
        num_scalar_prefetch=2,
        grid=(npair,),
        in_specs=[pl.BlockSpec((GROUP, N_EXPERTS, SUB), lambda p, ab_ref, rel_ref: (p, 0, 0)),
                  pl.BlockSpec((GROUP, N_EXPERTS, SUB), lambda p, ab_ref, rel_ref: (p, 0, 0))]
        + [window(e) for e in range(N_EXPERTS)]
        + [pl.BlockSpec((GROUP * SUB, d), lambda p, ab_ref, rel_ref: (p, 0)),
           pl.BlockSpec((1, 6, d), lambda p, ab_ref, rel_ref: (boff + (p * GROUP * SUB) // seq_len, 0, 0)),
           pl.BlockSpec((1, d), lambda p, ab_ref, rel_ref: (0, 0))],
        out_specs=pl.BlockSpec((GROUP * SUB, d), lambda p, ab_ref, rel_ref: (p, 0)),
    )
    ye2 = ye.reshape(N_EXPERTS * cap, d)
    return pl.pallas_call(
        functools.partial(_uncompact_small_kernel, nsub),
        out_shape=jax.ShapeDtypeStruct((t, d), F32),
        grid_spec=grid_spec,
        compiler_params=_params(("arbitrary",)),
        name="uncompact_small",
    )(ab, rel, pos3, aff3, *([ye2] * N_EXPERTS), x1, mod3, nf)


def _floor16(v):
    return (v // 16) * 16


def _trunk(x, mod3, boff, w, experts):
    b, l, d = x.shape
    t = b * l
    cap = EC_FACTOR * t // N_EXPERTS

    if experts[0].dtype == BF16:
        z = _in_proj(x, mod3, boff, w["norm1_w"], w["w_in"])
    else:
        z, *experts = _in_proj(x, mod3, boff, w["norm1_w"], w["w_in"], experts)
    o = _gla(z, w["wpre_f"], w["bpre_f"], False, None)
    o = _gla(z, w["wpre_b"], w["bpre_b"], True, o)
    x1, u2, aff3 = _mix_out(x, o, z, mod3, boff, w["conv_w"], w["conv_b"],
                            w["w_out_a"], w["w_out_b"], w["w_o"], w["norm2_w"], w["w_router_t"])
    pos3, base3 = _select(aff3, cap)

    base = base3[:, :, 0].T
    start = jnp.minimum(_floor16(base), cap - WIN)
    tbig_c = min(2048, t)
    tbig_u = min(1024, t)
    while tbig_u + 32 > cap:
        tbig_u //= 2
    nsb_u = tbig_u // SUB
    abig = jnp.minimum(_floor16(base[:, ::nsb_u]), cap - (tbig_u + 32))
    rel = start - jnp.repeat(abig, nsb_u, axis=1)

    base_next = jnp.concatenate([base[:, 1:], jnp.full((N_EXPERTS, 1), cap, jnp.int32)], axis=1)
    start_small = jnp.minimum(_floor16(base), cap - SMALL_WIN)
    ab_pair = jnp.minimum(_floor16(base[:, ::GROUP]), cap - GROUP_WIN)
    rel_small = start_small - jnp.repeat(ab_pair, GROUP, axis=1)
    fits_small = (jnp.all(base_next - _floor16(base) <= SMALL_WIN)
                  & jnp.all(rel_small <= GROUP_WIN - SMALL_WIN))

    u2f = u2.reshape(t, d)
    start_blk = _floor16(base[:, ::2])
    next_blk = base_next[:, 1::2]
    xe = lax.cond(
        jnp.all(next_blk - start_blk <= CWIN),
        lambda: _compact_small(u2f, pos3, start_blk.reshape(-1), _floor16(next_blk).reshape(-1), cap),
        lambda: _compact(u2f, pos3, start, cap, tbig_c))
    ye = _ffn(xe, cap, *experts)
    x1f = x1.reshape(t, d)
    out = lax.cond(
        fits_small,
        lambda: _uncompact_small(ye, pos3, aff3, ab_pair.reshape(-1), rel_small.reshape(-1), x1f, mod3, boff,
                                 l, w["norm_f_w"]),
        lambda: _uncompact(ye, pos3, aff3, abig.reshape(-1), rel, x1f, mod3, boff, l, w["norm_f_w"], tbig_u))
    return out.reshape(b, l, d), tuple(experts)


def _prepare_weights(w_in, w_alpha_f, b_alpha_f, w_alpha_b, b_alpha_b, gla_norm_w, conv_w, conv_b,
                     w_out_a, w_out_b, w_o, norm1_w, norm2_w, w_router, norm_f_w):
    d = D_MODEL
    wq, wk, wv, wr, wlf, wlb, wcb, wcc, wcx, wga, wgb = jnp.split(
        w_in, np.cumsum([512, 512, 1024, 1024, 16, 16, 1024, 1024, 1024, 1024])[:].tolist(), axis=1)
    pad = jnp.zeros((d, Z_WIDTH - Z_LR - 2 * GATE_RANK), w_in.dtype)
    wq = wq * (HEAD_K ** -0.5)
    wqkv = jnp.concatenate([wq.reshape(d, GLA_HEADS, HEAD_K), wk.reshape(d, GLA_HEADS, HEAD_K),
                            wv.reshape(d, GLA_HEADS, HEAD_V)], axis=2).reshape(d, GLA_HEADS * HEAD_W)
    w_in_p = jnp.concatenate([wqkv, wr, wcb, wcc, wcx, wga, wgb, wlf, wlb, pad], axis=1).astype(BF16)

    def pre(wa, ba, row0):
        wh = wa.reshape(GATE_RANK, GLA_HEADS, HEAD_K).transpose(1, 0, 2)
        full = jnp.zeros((GLA_HEADS, 128, HEAD_K), F32).at[:, row0:row0 + GATE_RANK].set(wh)
        return full.astype(BF16), ba.reshape(GLA_HEADS, 1, HEAD_K)

    wpre_f, bpre_f = pre(w_alpha_f, b_alpha_f, 0)
    wpre_b, bpre_b = pre(w_alpha_b, b_alpha_b, GATE_RANK)
    return dict(
        w_in=w_in_p, wpre_f=wpre_f, bpre_f=bpre_f, wpre_b=wpre_b, bpre_b=bpre_b,
        conv_w=conv_w, conv_b=conv_b.reshape(1, d),
        w_out_a=(gla_norm_w.reshape(d, 1) * w_out_a).astype(BF16),
        w_out_b=w_out_b.astype(BF16), w_o=w_o.astype(BF16),
        norm1_w=norm1_w, norm2_w=norm2_w.reshape(1, d), w_router_t=w_router.T.astype(BF16),
        norm_f_w=norm_f_w.reshape(1, d))


def kernel(x_prompt, x_sample, c_prompt, c_sample, w_ada, b_ada, norm1_w, w_in, w_alpha_f, b_alpha_f, w_alpha_b, b_alpha_b, gla_norm_w, conv_w, conv_b, w_out_a, w_out_b, w_o, norm2_w, w_router, w_gate, w_up, w_down, norm_f_w):
    d = D_MODEL
    bp, bs = c_prompt.shape[0], c_sample.shape[0]
    nrows = -(-(bp + bs) // 8) * 8
    c_all = jnp.concatenate([c_prompt, c_sample, jnp.zeros((nrows - bp - bs, d), F32)], axis=0)
    mod3 = _adaln(c_all, w_ada[0], b_ada[0]).reshape(nrows, 6, d)
    w = _prepare_weights(w_in[0], w_alpha_f[0], b_alpha_f[0], w_alpha_b[0], b_alpha_b[0], gla_norm_w[0],
                         conv_w[0], conv_b[0], w_out_a[0], w_out_b[0], w_o[0], norm1_w[0], norm2_w[0],
                         w_router[0], norm_f_w)
    y_prompt, experts = _trunk(x_prompt, mod3, 0, w, (w_gate[0], w_up[0], w_down[0]))
    y_sample, _ = _trunk(x_sample, mod3, bp, w, experts)
    return (y_prompt, y_sample)
```

```python
import functools

import numpy as np
import jax
import jax.numpy as jnp
from jax import lax
from jax.experimental import pallas as pl
from jax.experimental.pallas import tpu as pltpu

F32 = jnp.float32
BF16 = jnp.bfloat16

D_MODEL = 1024
GLA_HEADS = 4
HEAD_K = 128
HEAD_V = 256
GATE_RANK = 16
GATE_TAU = 16.0
CHUNK = 64
N_EXPERTS = 16
EC_FACTOR = 2
D_EXPERT = 2816
EPS = 1e-6
LOG2E = 1.4426950408889634

Z_R, Z_CB, Z_CC, Z_CX, Z_GA, Z_GB, Z_LR = 2048, 3072, 4096, 5120, 6144, 7168, 8192
HEAD_W = 2 * HEAD_K + HEAD_V
Z_WIDTH = 8448

UNIT = 4 * CHUNK
SUB = 128
WIN = SUB + 16
SMALL_WIN = 64
GROUP = 4
GROUP_WIN = SMALL_WIN + 48 * (GROUP - 1)
CWIN = 96
FFN_TM = 1024
VMEM_LIMIT = 56 * 1024 * 1024


def _dot(a, b):
    return jnp.dot(a, b, preferred_element_type=F32)


def _dot_nt(a, b):
    return lax.dot_general(a, b, (((1,), (1,)), ((), ())), preferred_element_type=F32)


def _sigmoid(x):
    return 0.5 * jnp.tanh(0.5 * x) + 0.5


def _dot_tn(a, b):
    return lax.dot_general(a, b, (((0,), (0,)), ((), ())), preferred_element_type=F32)


def _params(sem, limit=VMEM_LIMIT):
    return pltpu.CompilerParams(dimension_semantics=sem, vmem_limit_bytes=limit)


def _adaln_kernel(c_ref, w_ref, b_ref, o_ref):
    c = c_ref[...]
    s = c * jax.nn.sigmoid(c)
    o_ref[...] = jnp.dot(s, w_ref[...], preferred_element_type=F32,
                         precision=lax.Precision.HIGHEST) + b_ref[...]


def _adaln(c_all, w_ada, b_ada):
    nb, d = c_all.shape
    n = w_ada.shape[1]
    tn = 1536
    return pl.pallas_call(
        _adaln_kernel,
        out_shape=jax.ShapeDtypeStruct((nb, n), F32),
        grid=(n // tn,),
        in_specs=[pl.BlockSpec((nb, d), lambda j: (0, 0)),
                  pl.BlockSpec((d, tn), lambda j: (0, j)),
                  pl.BlockSpec((1, tn), lambda j: (0, j))],
        out_specs=pl.BlockSpec((nb, tn), lambda j: (0, j)),
        compiler_params=_params(("arbitrary",)),
        name="adaln",
    )(c_all, w_ada, b_ada.reshape(1, n))


CAST_SPLIT = 8


def _inproj_kernel(ncast, x_ref, mod_ref, n1_ref, w_ref, *refs):
    z_ref = refs[len(refs) // 2]
    x = x_ref[0]
    ms = jnp.mean(x * x, axis=-1, keepdims=True)
    m = mod_ref[0]
    u = (x * lax.rsqrt(ms + EPS)) * (n1_ref[...] * (1.0 + m[1:2])) + m[0:1]
    z_ref[0] = _dot(u.astype(BF16), w_ref[...]).astype(BF16)

    if ncast:
        lin = ((pl.program_id(0) * pl.num_programs(1) + pl.program_id(1)) * pl.num_programs(2)
               + pl.program_id(2))

        @pl.when(lin < ncast)
        def _():
            for src, dst in zip(refs[:3], refs[4:]):
                dst[...] = src[...].astype(BF16)


def _in_proj(x, mod3, boff, norm1_w, w_in_p, expert_weights=None):
    b, l, d = x.shape
    tm = min(1024, l)
    tn = 2816
    nt, nn = l // tm, Z_WIDTH // tn
    in_specs = [pl.BlockSpec((1, tm, d), lambda n, bi, i: (bi, i, 0)),
                pl.BlockSpec((1, 6, d), lambda n, bi, i: (bi + boff, 0, 0)),
                pl.BlockSpec((1, d), lambda n, bi, i: (0, 0)),
                pl.BlockSpec((d, tn), lambda n, bi, i: (0, n))]
    out_specs = [pl.BlockSpec((1, tm, tn), lambda n, bi, i: (bi, i, n))]
    out_shape = [jax.ShapeDtypeStruct((b, l, Z_WIDTH), BF16)]
    args = [x, mod3, norm1_w.reshape(1, d), w_in_p]
    ncast = 0
    if expert_weights is not None and N_EXPERTS * CAST_SPLIT > b * nt * nn:
        return (_in_proj(x, mod3, boff, norm1_w, w_in_p),) + tuple(wt.astype(BF16) for wt in expert_weights)
    if expert_weights is not None:
        ncast = N_EXPERTS * CAST_SPLIT

        def slab(n, bi, i):
            j = jnp.minimum((n * b + bi) * nt + i, ncast - 1)
            return (j // CAST_SPLIT, j % CAST_SPLIT, 0)

        for wt in expert_weights:
            blk = (1, wt.shape[1] // CAST_SPLIT, wt.shape[2])
            in_specs.append(pl.BlockSpec(blk, slab))
            out_specs.append(pl.BlockSpec(blk, slab))
            out_shape.append(jax.ShapeDtypeStruct(wt.shape, BF16))
            args.append(wt)
    res = pl.pallas_call(
        functools.partial(_inproj_kernel, ncast),
        out_shape=tuple(out_shape),
        grid=(nn, b, nt),
        in_specs=in_specs,
        out_specs=tuple(out_specs),
        compiler_params=_params(("arbitrary", "arbitrary", "arbitrary")),
        name="in_proj",
    )(*args)
    return res if expert_weights is not None else res[0]


def _gla_constants(reverse):
    r = np.arange(UNIT)[:, None]
    s = np.arange(UNIT)[None, :]
    cr, cs = r // CHUNK, s // CHUNK
    if reverse:
        tri = (cr == cs) & (s >= r)
        tr, ts = 3 - cr, 3 - cs
    else:
        tri = (cr == cs) & (s <= r)
        tr, ts = cr, cs
    code = np.zeros((UNIT, UNIT), np.int32)
    code[tri] = 1
    return jnp.asarray(tri, BF16), jnp.asarray(code)


def _gla_kernel(reverse, nunits, has_prev, *refs):
    if has_prev:
        qkv_ref, lr_ref, wp_ref, bp_ref, tri_ref, code_ref, prev_ref, o_ref, s_scr = refs
    else:
        qkv_ref, lr_ref, wp_ref, bp_ref, tri_ref, code_ref, o_ref, s_scr = refs
        prev_ref = None

    def v_of(rows):
        return qkv_ref[0, rows, 2 * HEAD_K:HEAD_W]

    @pl.when(pl.program_id(2) == 0)
    def _():
        s_scr[...] = jnp.zeros_like(s_scr)

    order = (3, 2, 1, 0) if reverse else (0, 1, 2, 3)

    def to_mem(by_step):
        out = [None] * 4
        for t in range(4):
            out[order[t]] = by_step[t]
        return out

    def scale_chunks(arr, logf):
        parts = []
        for c in range(4):
            part = arr[c * CHUNK:(c + 1) * CHUNK]
            if logf[c] is not None:
                part = part * jnp.exp2(logf[c])
            parts.append(part)
        return jnp.concatenate(parts, axis=0).astype(BF16)

    def rows_of(ui):
        um = (nunits - 1 - ui) if reverse else ui
        return pl.ds(um * UNIT, UNIT)

    def decays(ui):
        pre = _dot(lr_ref[0, rows_of(ui), :], wp_ref[0]) + bp_ref[0]
        soft = jnp.log2(1.0 + jnp.exp2(jnp.abs(pre) * (-LOG2E)))
        la = (jnp.minimum(pre, 0.0) * LOG2E - soft) * (1.0 / GATE_TAU)
        hi = la.astype(BF16)
        lo = (la - hi.astype(F32)).astype(BF16)
        bb = _dot(tri_ref[...], jnp.concatenate([hi, lo], axis=1))
        return bb[:, :HEAD_K] + bb[:, HEAD_K:]

    def operands(ui, b):
        rows = rows_of(ui)
        edge = 0 if reverse else CHUNK - 1
        tot = [b[c * CHUNK + edge:c * CHUNK + edge + 1] for c in range(4)]
        tot_full = jnp.concatenate([jnp.broadcast_to(t, (CHUNK, HEAD_K)) for t in tot], axis=0)
        q = qkv_ref[0, rows, 0:HEAD_K].astype(F32)
        k = qkv_ref[0, rows, HEAD_K:2 * HEAD_K].astype(F32)
        q_t = q * jnp.exp2(b)
        k_t = (k * jnp.exp2(-b)).astype(BF16)
        k_e = k * jnp.exp2(tot_full - b)
        q_tb = q_t.astype(BF16)
        k_eb = k_e.astype(BF16)
        bt = [tot[o] for o in order]

        zero = jnp.zeros((CHUNK, HEAD_K), BF16)
        e1 = jnp.exp2(bt[1])
        e2 = jnp.exp2(bt[2])
        qrows, krows = [None] * 4, [None] * 4
        for t in range(4):
            c = order[t]
            sl = slice(c * CHUNK, (c + 1) * CHUNK)
            qc, kc = q_tb[sl], k_eb[sl]
            q3 = qc if t == 2 else ((q_t[sl] * e2).astype(BF16) if t == 3 else zero)
            k3 = (k_e[sl] * e1).astype(BF16) if t == 0 else (kc if t == 1 else zero)
            qrows[c] = jnp.concatenate([qc if t == 1 else zero, qc if t == 3 else zero, q3], axis=1)
            krows[c] = jnp.concatenate([kc if t == 0 else zero, kc if t == 2 else zero, k3], axis=1)

        gq = to_mem([None, bt[0], bt[0] + bt[1], bt[0] + bt[1] + bt[2]])
        gk = to_mem([bt[1] + bt[2] + bt[3], bt[2] + bt[3], bt[3], None])
        dec = jnp.exp2(bt[0] + bt[1] + bt[2] + bt[3])
        return dict(q_t=q_tb, k_t=k_t, q_cat=jnp.concatenate(qrows, axis=0), k_cat=jnp.concatenate(krows, axis=0),
                    qs=scale_chunks(q_t, gq), ks=scale_chunks(k_e, gk), dec=dec)

    def scores(p):
        same = _dot_nt(p["q_t"], p["k_t"])
        cross = _dot_nt(p["q_cat"], p["k_cat"])
        return jnp.where(code_ref[...] == 1, same, cross).astype(BF16)

    def finish(ui, p, intra):
        rows = rows_of(ui)
        st = s_scr[...]
        o = intra + _dot_nt(p["qs"], st.astype(BF16))
        s_scr[...] = st * p["dec"] + _dot_tn(v_of(rows), p["ks"])
        if prev_ref is not None:
            o = o + prev_ref[0, rows, :]
        o_ref[0, rows, :] = o

    stages = 5
    bs, ps, atts, intra = {}, {}, {}, {}
    for wave in range(nunits + stages - 1):
        for k in range(stages):
            ui = wave - k
            if not 0 <= ui < nunits:
                continue
            if k == 0:
                bs[ui] = decays(ui)
            elif k == 1:
                ps[ui] = operands(ui, bs.pop(ui))
            elif k == 2:
                atts[ui] = scores(ps[ui])
            elif k == 3:
                intra[ui] = _dot(atts.pop(ui), v_of(rows_of(ui)))
            else:
                finish(ui, ps.pop(ui), intra.pop(ui))


def _gla(z, wpre, bpre, reverse, prev):
    b, l, _ = z.shape
    lb = min(2048, l)
    nb = l // lb
    nunits = lb // UNIT
    tri, code = _gla_constants(reverse)

    def seq(i):
        return (nb - 1 - i) if reverse else i

    in_specs = [
        pl.BlockSpec((1, lb, HEAD_W), lambda bi, h, i: (bi, seq(i), h)),
        pl.BlockSpec((1, lb, 128), lambda bi, h, i: (bi, seq(i), Z_LR // 128)),
        pl.BlockSpec((1, 128, HEAD_K), lambda bi, h, i: (h, 0, 0)),
        pl.BlockSpec((1, 1, HEAD_K), lambda bi, h, i: (h, 0, 0)),
        pl.BlockSpec((UNIT, UNIT), lambda bi, h, i: (0, 0)),
        pl.BlockSpec((UNIT, UNIT), lambda bi, h, i: (0, 0)),
    ]
    args = [z, z, wpre, bpre, tri, code]
    if prev is not None:
        in_specs.append(pl.BlockSpec((1, lb, HEAD_V), lambda bi, h, i: (bi, seq(i), h)))
        args.append(prev)
    return pl.pallas_call(
        functools.partial(_gla_kernel, reverse, nunits, prev is not None),
        out_shape=jax.ShapeDtypeStruct((b, l, GLA_HEADS * HEAD_V), F32),
        grid=(b, GLA_HEADS, nb),
        in_specs=in_specs,
        out_specs=pl.BlockSpec((1, lb, HEAD_V), lambda bi, h, i: (bi, seq(i), h)),
        scratch_shapes=[pltpu.VMEM((HEAD_V, HEAD_K), F32)],
        compiler_params=_params(("arbitrary", "arbitrary", "arbitrary")),
        name="gla_bwd" if reverse else "gla_fwd",
    )(*args)


def _mix_kernel(tm, x_ref, o_ref, r_ref, cb_ref, cc_ref, cx_ref, ccp_ref, cxp_ref, ccn_ref, cxn_ref,
                ga_ref, gb_ref, mod_ref, cw_ref, cbias_ref, woa_ref, wob_ref, wo_ref,
                n2_ref, wrt_ref, x1_ref, u2_ref, aff_ref):
    i = pl.program_id(1)
    last = pl.num_programs(1) - 1

    o = o_ref[0]
    parts = []
    for h in range(GLA_HEADS):
        oh = o[:, h * HEAD_V:(h + 1) * HEAD_V]
        ms = jnp.mean(oh * oh, axis=-1, keepdims=True)
        parts.append(oh * lax.rsqrt(ms + EPS))
    on = jnp.concatenate(parts, axis=1)
    r = r_ref[0]
    branch_a = _dot(on.astype(BF16) * (r * _sigmoid(r)), woa_ref[...])

    xc = cc_ref[0].astype(F32) * cx_ref[0].astype(F32)
    prev_row = ccp_ref[0, 15:16, :].astype(F32) * cxp_ref[0, 15:16, :].astype(F32)
    next_row = ccn_ref[0, 0:1, :].astype(F32) * cxn_ref[0, 0:1, :].astype(F32)
    prev_row = jnp.where(i > 0, prev_row, 0.0)
    next_row = jnp.where(i < last, next_row, 0.0)
    rowid = lax.broadcasted_iota(jnp.int32, (tm, 1), 0)
    xm1 = jnp.where(rowid == 0, prev_row, pltpu.roll(xc, 1, axis=0))
    xp1 = jnp.where(rowid == tm - 1, next_row, pltpu.roll(xc, tm - 1, axis=0))
    cw = cw_ref[...]
    conv = xm1 * cw[0:1] + xc * cw[1:2] + xp1 * cw[2:3] + cbias_ref[...]
    branch_b = _dot((cb_ref[0].astype(F32) * conv).astype(BF16), wob_ref[...])

    merged = (_sigmoid(ga_ref[0]) * branch_a.astype(BF16)
              + _sigmoid(gb_ref[0]) * branch_b.astype(BF16))
    mix = _dot(merged, wo_ref[...])

    m = mod_ref[0]
    x1 = x_ref[0] + m[2:3] * mix
    x1_ref[0] = x1
    ms = jnp.mean(x1 * x1, axis=-1, keepdims=True)
    u2 = (x1 * lax.rsqrt(ms + EPS)) * (n2_ref[...] * (1.0 + m[4:5])) + m[3:4]
    u2b = u2.astype(BF16)
    u2_ref[0] = u2b

    logits = _dot_nt(wrt_ref[...], u2b)
    ex = jnp.exp(logits - jnp.max(logits, axis=0, keepdims=True))
    aff = ex / jnp.sum(ex, axis=0, keepdims=True)
    for j in range(tm // SUB):
        aff_ref[j] = aff[:, j * SUB:(j + 1) * SUB]


def _mix_out(x, o, z, mod3, boff, conv_w, conv_b, woa, wob, wo, n2, wrt):
    b, l, d = x.shape
    tm = min(512, l)
    nt = l // tm
    hb = tm // 16
    nhalo = l // 16

    def col(c):
        return lambda bi, i: (bi, i, c // d)

    in_specs = [
        pl.BlockSpec((1, tm, d), lambda bi, i: (bi, i, 0)),
        pl.BlockSpec((1, tm, d), lambda bi, i: (bi, i, 0)),
        pl.BlockSpec((1, tm, d), col(Z_R)),
        pl.BlockSpec((1, tm, d), col(Z_CB)),
        pl.BlockSpec((1, tm, d), col(Z_CC)),
        pl.BlockSpec((1, tm, d), col(Z_CX)),
        pl.BlockSpec((1, 16, d), lambda bi, i: (bi, jnp.maximum(i * hb - 1, 0), Z_CC // d)),
        pl.BlockSpec((1, 16, d), lambda bi, i: (bi, jnp.maximum(i * hb - 1, 0), Z_CX // d)),
        pl.BlockSpec((1, 16, d), lambda bi, i: (bi, jnp.minimum((i + 1) * hb, nhalo - 1), Z_CC // d)),
        pl.BlockSpec((1, 16, d), lambda bi, i: (bi, jnp.minimum((i + 1) * hb, nhalo - 1), Z_CX // d)),
        pl.BlockSpec((1, tm, d), col(Z_GA)),
        pl.BlockSpec((1, tm, d), col(Z_GB)),
        pl.BlockSpec((1, 6, d), lambda bi, i: (bi + boff, 0, 0)),
        pl.BlockSpec((3, d), lambda bi, i: (0, 0)),
        pl.BlockSpec((1, d), lambda bi, i: (0, 0)),
        pl.BlockSpec((d, d), lambda bi, i: (0, 0)),
        pl.BlockSpec((d, d), lambda bi, i: (0, 0)),
        pl.BlockSpec((d, d), lambda bi, i: (0, 0)),
        pl.BlockSpec((1, d), lambda bi, i: (0, 0)),
        pl.BlockSpec((N_EXPERTS, d), lambda bi, i: (0, 0)),
    ]
    out_shape = (jax.ShapeDtypeStruct((b, l, d), F32),
                 jax.ShapeDtypeStruct((b, l, d), BF16),
                 jax.ShapeDtypeStruct((b * l // SUB, N_EXPERTS, SUB), F32))
    out_specs = (pl.BlockSpec((1, tm, d), lambda bi, i: (bi, i, 0)),
                 pl.BlockSpec((1, tm, d), lambda bi, i: (bi, i, 0)),
                 pl.BlockSpec((tm // SUB, N_EXPERTS, SUB), lambda bi, i: (bi * nt + i, 0, 0)))
    return pl.pallas_call(
        functools.partial(_mix_kernel, tm),
        out_shape=out_shape,
        grid=(b, nt),
        in_specs=in_specs,
        out_specs=out_specs,
        compiler_params=_params(("arbitrary", "arbitrary")),
        name="mix_out",
    )(x, o, z, z, z, z, z, z, z, z, z, z, mod3, conv_w, conv_b, woa, wob, wo, n2, wrt)


def _select_kernel(cap, nsb, aff_ref, u_ref, pos_ref, base_ref):
    capf = float(cap)

    def bits(j):
        return pltpu.bitcast(aff_ref[j], jnp.int32)

    def count(pred):
        def body(j, acc):
            return acc + jnp.where(pred(bits(j)), 1.0, 0.0)
        acc = lax.fori_loop(0, nsb, body, jnp.zeros((N_EXPERTS, SUB), F32), unroll=8)
        return jnp.sum(acc, axis=1, keepdims=True)

    def bit_body(t, thr):
        cand = thr | jnp.left_shift(jnp.int32(1), 30 - t)
        return jnp.where(count(lambda v: v >= cand) >= capf, cand, thr)

    thr = lax.fori_loop(0, 31, bit_body, jnp.zeros((N_EXPERTS, SUB), jnp.int32))
    need = capf - count(lambda v: v > thr)

    def blk(j, carry):
        ceq, csel = carry
        v = bits(j)
        gtf = jnp.where(v > thr, 1.0, 0.0)
        eqf = jnp.where(v == thr, 1.0, 0.0)
        r1 = _dot(eqf.astype(BF16), u_ref[...])
        eq_rank = r1[:, :SUB] - eqf + ceq
        sel = gtf + jnp.where(eq_rank < need, eqf, 0.0)
        r2 = _dot(sel.astype(BF16), u_ref[...])
        pos_ref[j] = jnp.where(sel > 0.5, r2[:, :SUB] + csel - 1.0, -1.0).astype(jnp.int32)
        base_ref[j] = csel.astype(jnp.int32)
        return ceq + r1[:, SUB:], csel + r2[:, SUB:]

    zero = jnp.zeros((N_EXPERTS, SUB), F32)
    lax.fori_loop(0, nsb, blk, (zero, zero), unroll=8)


def _select(aff3, cap):
    nsb = aff3.shape[0]
    tri = np.triu(np.ones((SUB, SUB), np.float32))
    umat = jnp.asarray(np.concatenate([tri, np.ones((SUB, SUB), np.float32)], axis=1), BF16)
    shp = jax.ShapeDtypeStruct((nsb, N_EXPERTS, SUB), jnp.int32)
    return pl.pallas_call(
        functools.partial(_select_kernel, cap, nsb),
        out_shape=(shp, shp),
        compiler_params=pltpu.CompilerParams(vmem_limit_bytes=VMEM_LIMIT),
        name="select",
    )(aff3, umat)


def _compact_kernel(nsb, start_ref, pos_ref, x_ref, xe_ref):
    e = pl.program_id(0)

    @pl.when(pl.program_id(1) == 0)
    def _():
        xe_ref[...] = jnp.zeros_like(xe_ref)

    rowi = lax.broadcasted_iota(jnp.int32, (WIN, SUB), 0)

    def body(s, carry):
        st = pl.multiple_of(start_ref[0, 0, s], 16)
        local = pos_ref[s, pl.ds(e, 1), :] - st
        onehot = jnp.where(rowi == local, 1.0, 0.0).astype(BF16)
        xs = x_ref[pl.ds(s * SUB, SUB), :]
        rows = _dot(onehot, xs).astype(BF16)
        xe_ref[0, pl.ds(st, WIN), :] = xe_ref[0, pl.ds(st, WIN), :] + rows
        return carry

    for s in range(nsb):
        body(s, 0)


def _compact(u2, pos3, start, cap, tbig):
    t, d = u2.shape
    nbig = t // tbig
    nsb = tbig // SUB
    return pl.pallas_call(
        functools.partial(_compact_kernel, nsb),
        out_shape=jax.ShapeDtypeStruct((N_EXPERTS, cap + CWIN, d), BF16),
        grid=(N_EXPERTS, nbig),
        in_specs=[pl.BlockSpec((1, 1, nsb), lambda e, b: (e * nbig + b, 0, 0), memory_space=pltpu.SMEM),
                  pl.BlockSpec((nsb, N_EXPERTS, SUB), lambda e, b: (b, 0, 0)),
                  pl.BlockSpec((tbig, d), lambda e, b: (b, 0))],
        out_specs=pl.BlockSpec((1, cap + CWIN, d), lambda e, b: (e, 0, 0)),
        compiler_params=_params(("arbitrary", "arbitrary")),
        name="compact",
    )(start.reshape(N_EXPERTS * nbig, 1, nsb), pos3, u2)


def _compact_small_kernel(nblk, cap, start_ref, next_ref, pos_ref, x_ref, xe_ref, strip_buf, tail_buf, zero_buf,
                          sems, zsem):
    p = pl.program_id(0)
    last = pl.num_programs(0) - 1
    slot = lax.rem(p, 2)

    @pl.when(p == 0)
    def _():
        tail_buf[...] = jnp.zeros_like(tail_buf)
        zero_buf[...] = jnp.zeros_like(zero_buf)

    def strip_copy(sl, e, row0):
        return pltpu.make_async_copy(strip_buf.at[sl, e], xe_ref.at[e, pl.ds(row0, CWIN), :], sems.at[sl, e])

    pos = jnp.concatenate([pos_ref[0], pos_ref[1]], axis=1)
    rowi = lax.broadcasted_iota(jnp.int32, (CWIN, 2 * SUB), 0)
    starts = [pl.multiple_of(start_ref[e * nblk + p], 16) for e in range(N_EXPERTS)]
    onehot = jnp.concatenate(
        [jnp.where(rowi == pos[e:e + 1] - starts[e], 1.0, 0.0).astype(BF16) for e in range(N_EXPERTS)], axis=0)
    rows = _dot(onehot, x_ref[...]).astype(BF16)
    for e in range(N_EXPERTS):
        strip_buf[slot, e, 0:16, :] = rows[e * CWIN:e * CWIN + 16] + tail_buf[e]
        strip_buf[slot, e, 16:CWIN, :] = rows[e * CWIN + 16:(e + 1) * CWIN]
        t0 = next_ref[e * nblk + p] - starts[e]
        t0c = pl.multiple_of(jnp.minimum(t0, CWIN - 16), 16)
        tail_buf[e] = jnp.where(t0 < CWIN, strip_buf[slot, e, pl.ds(t0c, 16), :], jnp.zeros((16, 1), BF16))

    for e in range(N_EXPERTS):
        @pl.when(p > 0)
        def _():
            strip_copy(1 - slot, e, starts[e]).wait()
        strip_copy(slot, e, starts[e]).start()

    @pl.when(p == last)
    def _():
        for e in range(N_EXPERTS):
            strip_copy(slot, e, starts[e]).wait()
            pad = pltpu.make_async_copy(zero_buf, xe_ref.at[e, pl.ds(cap, CWIN), :], zsem)
            pad.start()
            pad.wait()


def _compact_small(u2, pos3, start, start_next, cap):
    t, d = u2.shape
    nblk = t // (2 * SUB)
    grid_spec = pltpu.PrefetchScalarGridSpec(
        num_scalar_prefetch=2,
        grid=(nblk,),
        in_specs=[pl.BlockSpec((2, N_EXPERTS, SUB), lambda p, a, b: (p, 0, 0)),
                  pl.BlockSpec((2 * SUB, d), lambda p, a, b: (p, 0))],
        out_specs=pl.BlockSpec(memory_space=pl.ANY),
        scratch_shapes=[pltpu.VMEM((2, N_EXPERTS, CWIN, d), BF16),
                        pltpu.VMEM((N_EXPERTS, 16, d), BF16),
                        pltpu.VMEM((CWIN, d), BF16),
                        pltpu.SemaphoreType.DMA((2, N_EXPERTS)),
                        pltpu.SemaphoreType.DMA(())],
    )
    return pl.pallas_call(
        functools.partial(_compact_small_kernel, nblk, cap),
        out_shape=jax.ShapeDtypeStruct((N_EXPERTS, cap + CWIN, d), BF16),
        grid_spec=grid_spec,
        compiler_params=_params(("arbitrary",)),
        name="compact_small",
    )(start, start_next, pos3, u2)


def _ffn_kernel(chunks, x_ref, wg_ref, wu_ref, wd_ref, o_ref, acc_ref):
    xs = x_ref[0]
    for i, (c0, cw) in enumerate(chunks):
        g = _dot(xs, wg_ref[0, :, c0:c0 + cw])
        u = _dot(xs, wu_ref[0, :, c0:c0 + cw])
        h = ((g * jax.nn.sigmoid(g)) * u).astype(BF16)
        part = _dot(h, wd_ref[0, c0:c0 + cw, :])
        if i == 0:
            acc_ref[...] = part
        else:
            acc_ref[...] += part
    o_ref[0] = acc_ref[...].astype(BF16)


def _ffn(xe, cap, wg, wu, wd):
    e, _, d = xe.shape
    f = wg.shape[2]
    tm = min(FFN_TM, cap)
    chunks = tuple((c0, min(512, f - c0)) for c0 in range(0, f, 512))
    return pl.pallas_call(
        functools.partial(_ffn_kernel, chunks),
        out_shape=jax.ShapeDtypeStruct((e, cap, d), BF16),
        grid=(e, cap // tm),
        in_specs=[pl.BlockSpec((1, tm, d), lambda ei, m: (ei, m, 0)),
                  pl.BlockSpec((1, d, f), lambda ei, m: (ei, 0, 0)),
                  pl.BlockSpec((1, d, f), lambda ei, m: (ei, 0, 0)),
                  pl.BlockSpec((1, f, d), lambda ei, m: (ei, 0, 0))],
        out_specs=pl.BlockSpec((1, tm, d), lambda ei, m: (ei, m, 0)),
        scratch_shapes=[pltpu.VMEM((tm, d), F32)],
        compiler_params=_params(("arbitrary", "arbitrary")),
        name="expert_ffn",
    )(xe, wg, wu, wd)


def _uncompact_kernel(nsb, nbig, abig_ref, rel_ref, pos_ref, aff_ref, ye_ref, x1_ref, mod_ref, nf_ref,
                      out_ref, acc_ref):
    b = pl.program_id(0)
    e = pl.program_id(1)

    @pl.when(e == 0)
    def _():
        acc_ref[...] = jnp.zeros_like(acc_ref)

    rowi = lax.broadcasted_iota(jnp.int32, (WIN, SUB), 0)
    big = abig_ref[e * nbig + b]

    def body(s, carry):
        rel = pl.multiple_of(rel_ref[0, 0, s], 16)
        local = pos_ref[s, pl.ds(e, 1), :] - (big + rel)
        gate = aff_ref[s, pl.ds(e, 1), :]
        weights = jnp.where(rowi == local, gate, 0.0).astype(BF16)
        rows = pl.ds(s * SUB, SUB)
        acc_ref[rows, :] += _dot_tn(weights, ye_ref[pl.ds(rel, WIN), :])
        return carry

    for s in range(nsb):
        body(s, 0)

    @pl.when(e == pl.num_programs(1) - 1)
    def _():
        xo = x1_ref[...] + mod_ref[0][5:6] * acc_ref[...]
        ms = jnp.mean(xo * xo, axis=-1, keepdims=True)
        out_ref[...] = xo * lax.rsqrt(ms + EPS) * nf_ref[...]


def _uncompact(ye, pos3, aff3, abig, rel, x1, mod3, boff, seq_len, nf, tbig):
    t, d = x1.shape
    cap = ye.shape[1]
    nbig = t // tbig
    nsb = tbig // SUB
    wbig = tbig + 32
    grid_spec = pltpu.PrefetchScalarGridSpec(
        num_scalar_prefetch=1,
        grid=(nbig, N_EXPERTS),
        in_specs=[
            pl.BlockSpec((1, 1, nsb), lambda b, e, ab: (e * nbig + b, 0, 0), memory_space=pltpu.SMEM),
            pl.BlockSpec((nsb, N_EXPERTS, SUB), lambda b, e, ab: (b, 0, 0)),
            pl.BlockSpec((nsb, N_EXPERTS, SUB), lambda b, e, ab: (b, 0, 0)),
            pl.BlockSpec((pl.Element(wbig), pl.Element(d)),
                         lambda b, e, ab: (pl.multiple_of(e * cap + ab[e * nbig + b], 16), 0)),
            pl.BlockSpec((tbig, d), lambda b, e, ab: (b, 0)),
            pl.BlockSpec((1, 6, d), lambda b, e, ab: (boff + (b * tbig) // seq_len, 0, 0)),
            pl.BlockSpec((1, d), lambda b, e, ab: (0, 0)),
        ],
        out_specs=pl.BlockSpec((tbig, d), lambda b, e, ab: (b, 0)),
        scratch_shapes=[pltpu.VMEM((tbig, d), F32)],
    )
    return pl.pallas_call(
        functools.partial(_uncompact_kernel, nsb, nbig),
        out_shape=jax.ShapeDtypeStruct((t, d), F32),
        grid_spec=grid_spec,
        compiler_params=_params(("arbitrary", "arbitrary")),
        name="uncompact",
    )(abig, rel.reshape(N_EXPERTS * nbig, 1, nsb), pos3, aff3, ye.reshape(N_EXPERTS * cap, d), x1, mod3, nf)


def _uncompact_small_kernel(nsub, ab_ref, rel_ref, pos_ref, aff_ref, *refs):
    ye_refs = refs[:N_EXPERTS]
    x1_ref, mod_ref, nf_ref, out_ref = refs[N_EXPERTS:]
    p = pl.program_id(0)
    npair = nsub // GROUP
    rowi = lax.broadcasted_iota(jnp.int32, (SMALL_WIN, SUB), 0)
    ys = []
    for j in range(GROUP):
        pos = pos_ref[j]
        aff = aff_ref[j]
        weights, windows = [], []
        for e in range(N_EXPERTS):
            rel = pl.multiple_of(rel_ref[e * nsub + GROUP * p + j], 16)
            local = pos[e:e + 1] - (ab_ref[e * npair + p] + rel)
            weights.append(jnp.where(rowi == local, aff[e:e + 1], 0.0).astype(BF16))
            windows.append(ye_refs[e][pl.ds(rel, SMALL_WIN), :])
        ys.append(_dot_tn(jnp.concatenate(weights, axis=0), jnp.concatenate(windows, axis=0)))
    xo = x1_ref[...] + mod_ref[0][5:6] * jnp.concatenate(ys, axis=0)
    ms = jnp.mean(xo * xo, axis=-1, keepdims=True)
    out_ref[...] = xo * lax.rsqrt(ms + EPS) * nf_ref[...]


def _uncompact_small(ye, pos3, aff3, ab, rel, x1, mod3, boff, seq_len, nf):
    t, d = x1.shape
    cap = ye.shape[1]
    nsub = t // SUB
    npair = nsub // GROUP

    def window(e):
        return pl.BlockSpec((pl.Element(GROUP_WIN), pl.Element(d)),
                            lambda p, ab_ref, rel_ref: (pl.multiple_of(e * cap + ab_ref[e * npair + p], 16), 0))

    grid_spec = pltpu.PrefetchScalarGridSpec(
        num_scalar_prefetch=2,
        grid=(npair,),
        in_specs=[pl.BlockSpec((GROUP, N_EXPERTS, SUB), lambda p, ab_ref, rel_ref: (p, 0, 0)),
                  pl.BlockSpec((GROUP, N_EXPERTS, SUB), lambda p, ab_ref, rel_ref: (p, 0, 0))]
        + [window(e) for e in range(N_EXPERTS)]
        + [pl.BlockSpec((GROUP * SUB, d), lambda p, ab_ref, rel_ref: (p, 0)),
           pl.BlockSpec((1, 6, d), lambda p, ab_ref, rel_ref: (boff + (p * GROUP * SUB) // seq_len, 0, 0)),
           pl.BlockSpec((1, d), lambda p, ab_ref, rel_ref: (0, 0))],
        out_specs=pl.BlockSpec((GROUP * SUB, d), lambda p, ab_ref, rel_ref: (p, 0)),
    )
    ye2 = ye.reshape(N_EXPERTS * cap, d)
    return pl.pallas_call(
        functools.partial(_uncompact_small_kernel, nsub),
        out_shape=jax.ShapeDtypeStruct((t, d), F32),
        grid_spec=grid_spec,
        compiler_params=_params(("arbitrary",)),
        name="uncompact_small",
    )(ab, rel, pos3, aff3, *([ye2] * N_EXPERTS), x1, mod3, nf)


def _floor16(v):
    return (v // 16) * 16


def _trunk(x, mod3, boff, w, experts):
    b, l, d = x.shape
    t = b * l
    cap = EC_FACTOR * t // N_EXPERTS

    if experts[0].dtype == BF16:
        z = _in_proj(x, mod3, boff, w["norm1_w"], w["w_in"])
    else:
        z, *experts = _in_proj(x, mod3, boff, w["norm1_w"], w["w_in"], experts)
    o = _gla(z, w["wpre_f"], w["bpre_f"], False, None)
    o = _gla(z, w["wpre_b"], w["bpre_b"], True, o)
    x1, u2, aff3 = _mix_out(x, o, z, mod3, boff, w["conv_w"], w["conv_b"],
                            w["w_out_a"], w["w_out_b"], w["w_o"], w["norm2_w"], w["w_router_t"])
    pos3, base3 = _select(aff3, cap)

    base = base3[:, :, 0].T
    start = jnp.minimum(_floor16(base), cap - WIN)
    tbig_c = min(2048, t)
    tbig_u = min(1024, t)
    while tbig_u + 32 > cap:
        tbig_u //= 2
    nsb_u = tbig_u // SUB
    abig = jnp.minimum(_floor16(base[:, ::nsb_u]), cap - (tbig_u + 32))
    rel = start - jnp.repeat(abig, nsb_u, axis=1)

    base_next = jnp.concatenate([base[:, 1:], jnp.full((N_EXPERTS, 1), cap, jnp.int32)], axis=1)
    start_small = jnp.minimum(_floor16(base), cap - SMALL_WIN)
    ab_pair = jnp.minimum(_floor16(base[:, ::GROUP]), cap - GROUP_WIN)
    rel_small = start_small - jnp.repeat(ab_pair, GROUP, axis=1)
    fits_small = (jnp.all(base_next - _floor16(base) <= SMALL_WIN)
                  & jnp.all(rel_small <= GROUP_WIN - SMALL_WIN))

    u2f = u2.reshape(t, d)
    start_blk = _floor16(base[:, ::2])
    next_blk = base_next[:, 1::2]
    xe = lax.cond(
        jnp.all(next_blk - start_blk <= CWIN),
        lambda: _compact_small(u2f, pos3, start_blk.reshape(-1), _floor16(next_blk).reshape(-1), cap),
        lambda: _compact(u2f, pos3, start, cap, tbig_c))
    ye = _ffn(xe, cap, *experts)
    x1f = x1.reshape(t, d)
    out = lax.cond(
        fits_small,
        lambda: _uncompact_small(ye, pos3, aff3, ab_pair.reshape(-1), rel_small.reshape(-1), x1f, mod3, boff,
                                 l, w["norm_f_w"]),
        lambda: _uncompact(ye, pos3, aff3, abig.reshape(-1), rel, x1f, mod3, boff, l, w["norm_f_w"], tbig_u))
    return out.reshape(b, l, d), tuple(experts)


def _prepare_weights(w_in, w_alpha_f, b_alpha_f, w_alpha_b, b_alpha_b, gla_norm_w, conv_w, conv_b,
                     w_out_a, w_out_b, w_o, norm1_w, norm2_w, w_router, norm_f_w):
    d = D_MODEL
    wq, wk, wv, wr, wlf, wlb, wcb, wcc, wcx, wga, wgb = jnp.split(
        w_in, np.cumsum([512, 512, 1024, 1024, 16, 16, 1024, 1024, 1024, 1024])[:].tolist(), axis=1)
    pad = jnp.zeros((d, Z_WIDTH - Z_LR - 2 * GATE_RANK), w_in.dtype)
    wq = wq * (HEAD_K ** -0.5)
    wqkv = jnp.concatenate([wq.reshape(d, GLA_HEADS, HEAD_K), wk.reshape(d, GLA_HEADS, HEAD_K),
                            wv.reshape(d, GLA_HEADS, HEAD_V)], axis=2).reshape(d, GLA_HEADS * HEAD_W)
    w_in_p = jnp.concatenate([wqkv, wr, wcb, wcc, wcx, wga, wgb, wlf, wlb, pad], axis=1).astype(BF16)

    def pre(wa, ba, row0):
        wh = wa.reshape(GATE_RANK, GLA_HEADS, HEAD_K).transpose(1, 0, 2)
        full = jnp.zeros((GLA_HEADS, 128, HEAD_K), F32).at[:, row0:row0 + GATE_RANK].set(wh)
        return full.astype(BF16), ba.reshape(GLA_HEADS, 1, HEAD_K)

    wpre_f, bpre_f = pre(w_alpha_f, b_alpha_f, 0)
    wpre_b, bpre_b = pre(w_alpha_b, b_alpha_b, GATE_RANK)
    return dict(
        w_in=w_in_p, wpre_f=wpre_f, bpre_f=bpre_f, wpre_b=wpre_b, bpre_b=bpre_b,
        conv_w=conv_w, conv_b=conv_b.reshape(1, d),
        w_out_a=(gla_norm_w.reshape(d, 1) * w_out_a).astype(BF16),
        w_out_b=w_out_b.astype(BF16), w_o=w_o.astype(BF16),
        norm1_w=norm1_w, norm2_w=norm2_w.reshape(1, d), w_router_t=w_router.T.astype(BF16),
        norm_f_w=norm_f_w.reshape(1, d))


def kernel(x_prompt, x_sample, c_prompt, c_sample, w_ada, b_ada, norm1_w, w_in, w_alpha_f, b_alpha_f, w_alpha_b, b_alpha_b, gla_norm_w, conv_w, conv_b, w_out_a, w_out_b, w_o, norm2_w, w_router, w_gate, w_up, w_down, norm_f_w):
    d = D_MODEL
    bp, bs = c_prompt.shape[0], c_sample.shape[0]
    nrows = -(-(bp + bs) // 8) * 8
    c_all = jnp.concatenate([c_prompt, c_sample, jnp.zeros((nrows - bp - bs, d), F32)], axis=0)
    mod3 = _adaln(c_all, w_ada[0], b_ada[0]).reshape(nrows, 6, d)
    w = _prepare_weights(w_in[0], w_alpha_f[0], b_alpha_f[0], w_alpha_b[0], b_alpha_b[0], gla_norm_w[0],
                         conv_w[0], conv_b[0], w_out_a[0], w_out_b[0], w_o[0], norm1_w[0], norm2_w[0],
                         w_router[0], norm_f_w)
    y_prompt, experts = _trunk(x_prompt, mod3, 0, w, (w_gate[0], w_up[0], w_down[0]))
    y_sample, _ = _trunk(x_sample, mod3, bp, w, experts)
    return (y_prompt, y_sample)
```

```python
import functools

import numpy as np
import jax
import jax.numpy as jnp
from jax import lax
from jax.experimental import pallas as pl
from jax.experimental.pallas import tpu as pltpu

F32 = jnp.float32
BF16 = jnp.bfloat16

D_MODEL = 1024
GLA_HEADS = 4
HEAD_K = 128
HEAD_V = 256
GATE_RANK = 16
GATE_TAU = 16.0
CHUNK = 64
N_EXPERTS = 16
EC_FACTOR = 2
D_EXPERT = 2816
EPS = 1e-6
LOG2E = 1.4426950408889634

Z_R, Z_CB, Z_CC, Z_CX, Z_GA, Z_GB, Z_LR = 2048, 3072, 4096, 5120, 6144, 7168, 8192
HEAD_W = 2 * HEAD_K + HEAD_V
Z_WIDTH = 8448

UNIT = 4 * CHUNK
SUB = 128
WIN = SUB + 16
SMALL_WIN = 64
GROUP = 4
GROUP_WIN = SMALL_WIN + 48 * (GROUP - 1)
CWIN = 96
FFN_TM = 1024
VMEM_LIMIT = 56 * 1024 * 1024


def _dot(a, b):
    return jnp.dot(a, b, preferred_element_type=F32)


def _dot_nt(a, b):
    return lax.dot_general(a, b, (((1,), (1,)), ((), ())), preferred_element_type=F32)


def _sigmoid(x):
    return 0.5 * jnp.tanh(0.5 * x) + 0.5


def _dot_tn(a, b):
    return lax.dot_general(a, b, (((0,), (0,)), ((), ())), preferred_element_type=F32)


def _params(sem, limit=VMEM_LIMIT):
    return pltpu.CompilerParams(dimension_semantics=sem, vmem_limit_bytes=limit)


def _adaln_kernel(c_ref, w_ref, b_ref, o_ref):
    c = c_ref[...]
    s = c * jax.nn.sigmoid(c)
    o_ref[...] = jnp.dot(s, w_ref[...], preferred_element_type=F32,
                         precision=lax.Precision.HIGHEST) + b_ref[...]


def _adaln(c_all, w_ada, b_ada):
    nb, d = c_all.shape
    n = w_ada.shape[1]
    tn = 1536
    return pl.pallas_call(
        _adaln_kernel,
        out_shape=jax.ShapeDtypeStruct((nb, n), F32),
        grid=(n // tn,),
        in_specs=[pl.BlockSpec((nb, d), lambda j: (0, 0)),
                  pl.BlockSpec((d, tn), lambda j: (0, j)),
                  pl.BlockSpec((1, tn), lambda j: (0, j))],
        out_specs=pl.BlockSpec((nb, tn), lambda j: (0, j)),
        compiler_params=_params(("arbitrary",)),
        name="adaln",
    )(c_all, w_ada, b_ada.reshape(1, n))


CAST_SPLIT = 8


def _inproj_kernel(ncast, x_ref, mod_ref, n1_ref, w_ref, *refs):
    z_ref = refs[len(refs) // 2]
    x = x_ref[0]
    ms = jnp.mean(x * x, axis=-1, keepdims=True)
    m = mod_ref[0]
    u = (x * lax.rsqrt(ms + EPS)) * (n1_ref[...] * (1.0 + m[1:2])) + m[0:1]
    z_ref[0] = _dot(u.astype(BF16), w_ref[...]).astype(BF16)

    if ncast:
        lin = ((pl.program_id(0) * pl.num_programs(1) + pl.program_id(1)) * pl.num_programs(2)
               + pl.program_id(2))

        @pl.when(lin < ncast)
        def _():
            for src, dst in zip(refs[:3], refs[4:]):
                dst[...] = src[...].astype(BF16)


def _in_proj(x, mod3, boff, norm1_w, w_in_p, expert_weights=None):
    b, l, d = x.shape
    tm = min(1024, l)
    tn = 2816
    nt, nn = l // tm, Z_WIDTH // tn
    in_specs = [pl.BlockSpec((1, tm, d), lambda n, bi, i: (bi, i, 0)),
                pl.BlockSpec((1, 6, d), lambda n, bi, i: (bi + boff, 0, 0)),
                pl.BlockSpec((1, d), lambda n, bi, i: (0, 0)),
                pl.BlockSpec((d, tn), lambda n, bi, i: (0, n))]
    out_specs = [pl.BlockSpec((1, tm, tn), lambda n, bi, i: (bi, i, n))]
    out_shape = [jax.ShapeDtypeStruct((b, l, Z_WIDTH), BF16)]
    args = [x, mod3, norm1_w.reshape(1, d), w_in_p]
    ncast = 0
    if expert_weights is not None and N_EXPERTS * CAST_SPLIT > b * nt * nn:
        return (_in_proj(x, mod3, boff, norm1_w, w_in_p),) + tuple(wt.astype(BF16) for wt in expert_weights)
    if expert_weights is not None:
        ncast = N_EXPERTS * CAST_SPLIT

        def slab(n, bi, i):
            j = jnp.minimum((n * b + bi) * nt + i, ncast - 1)
            return (j // CAST_SPLIT, j % CAST_SPLIT, 0)

        for wt in expert_weights:
            blk = (1, wt.shape[1] // CAST_SPLIT, wt.shape[2])
            in_specs.append(pl.BlockSpec(blk, slab))
            out_specs.append(pl.BlockSpec(blk, slab))
            out_shape.append(jax.ShapeDtypeStruct(wt.shape, BF16))
            args.append(wt)
    res = pl.pallas_call(
        functools.partial(_inproj_kernel, ncast),
        out_shape=tuple(out_shape),
        grid=(nn, b, nt),
        in_specs=in_specs,
        out_specs=tuple(out_specs),
        compiler_params=_params(("arbitrary", "arbitrary", "arbitrary")),
        name="in_proj",
    )(*args)
    return res if expert_weights is not None else res[0]


def _gla_constants(reverse):
    r = np.arange(UNIT)[:, None]
    s = np.arange(UNIT)[None, :]
    cr, cs = r // CHUNK, s // CHUNK
    if reverse:
        tri = (cr == cs) & (s >= r)
        tr, ts = 3 - cr, 3 - cs
    else:
        tri = (cr == cs) & (s <= r)
        tr, ts = cr, cs
    code = np.zeros((UNIT, UNIT), np.int32)
    code[tri] = 1
    return jnp.asarray(tri, BF16), jnp.asarray(code)


def _gla_kernel(reverse, nunits, has_prev, *refs):
    if has_prev:
        qkv_ref, lr_ref, wp_ref, bp_ref, tri_ref, code_ref, prev_ref, o_ref, s_scr = refs
    else:
        qkv_ref, lr_ref, wp_ref, bp_ref, tri_ref, code_ref, o_ref, s_scr = refs
        prev_ref = None

    def v_of(rows):
        return qkv_ref[0, rows, 2 * HEAD_K:HEAD_W]

    @pl.when(pl.program_id(2) == 0)
    def _():
        s_scr[...] = jnp.zeros_like(s_scr)

    order = (3, 2, 1, 0) if reverse else (0, 1, 2, 3)

    def to_mem(by_step):
        out = [None] * 4
        for t in range(4):
            out[order[t]] = by_step[t]
        return out

    def scale_chunks(arr, logf):
        parts = []
        for c in range(4):
            part = arr[c * CHUNK:(c + 1) * CHUNK]
            if logf[c] is not None:
                part = part * jnp.exp2(logf[c])
            parts.append(part)
        return jnp.concatenate(parts, axis=0).astype(BF16)

    def rows_of(ui):
        um = (nunits - 1 - ui) if reverse else ui
        return pl.ds(um * UNIT, UNIT)

    def decays(ui):
        pre = _dot(lr_ref[0, rows_of(ui), :], wp_ref[0]) + bp_ref[0]
        soft = jnp.log2(1.0 + jnp.exp2(jnp.abs(pre) * (-LOG2E)))
        la = (jnp.minimum(pre, 0.0) * LOG2E - soft) * (1.0 / GATE_TAU)
        hi = la.astype(BF16)
        lo = (la - hi.astype(F32)).astype(BF16)
        bb = _dot(tri_ref[...], jnp.concatenate([hi, lo], axis=1))
        return bb[:, :HEAD_K] + bb[:, HEAD_K:]

    def operands(ui, b):
        rows = rows_of(ui)
        edge = 0 if reverse else CHUNK - 1
        tot = [b[c * CHUNK + edge:c * CHUNK + edge + 1] for c in range(4)]
        tot_full = jnp.concatenate([jnp.broadcast_to(t, (CHUNK, HEAD_K)) for t in tot], axis=0)
        q = qkv_ref[0, rows, 0:HEAD_K].astype(F32)
        k = qkv_ref[0, rows, HEAD_K:2 * HEAD_K].astype(F32)
        q_t = q * jnp.exp2(b)
        k_t = (k * jnp.exp2(-b)).astype(BF16)
        k_e = k * jnp.exp2(tot_full - b)
        q_tb = q_t.astype(BF16)
        k_eb = k_e.astype(BF16)
        bt = [tot[o] for o in order]

        zero = jnp.zeros((CHUNK, HEAD_K), BF16)
        e1 = jnp.exp2(bt[1])
        e2 = jnp.exp2(bt[2])
        qrows, krows = [None] * 4, [None] * 4
        for t in range(4):
            c = order[t]
            sl = slice(c * CHUNK, (c + 1) * CHUNK)
            qc, kc = q_tb[sl], k_eb[sl]
            q3 = qc if t == 2 else ((q_t[sl] * e2).astype(BF16) if t == 3 else zero)
            k3 = (k_e[sl] * e1).astype(BF16) if t == 0 else (kc if t == 1 else zero)
            qrows[c] = jnp.concatenate([qc if t == 1 else zero, qc if t == 3 else zero, q3], axis=1)
            krows[c] = jnp.concatenate([kc if t == 0 else zero, kc if t == 2 else zero, k3], axis=1)

        gq = to_mem([None, bt[0], bt[0] + bt[1], bt[0] + bt[1] + bt[2]])
        gk = to_mem([bt[1] + bt[2] + bt[3], bt[2] + bt[3], bt[3], None])
        dec = jnp.exp2(bt[0] + bt[1] + bt[2] + bt[3])
        return dict(q_t=q_tb, k_t=k_t, q_cat=jnp.concatenate(qrows, axis=0), k_cat=jnp.concatenate(krows, axis=0),
                    qs=scale_chunks(q_t, gq), ks=scale_chunks(k_e, gk), dec=dec)

    def scores(p):
        same = _dot_nt(p["q_t"], p["k_t"])
        cross = _dot_nt(p["q_cat"], p["k_cat"])
        return jnp.where(code_ref[...] == 1, same, cross).astype(BF16)

    def finish(ui, p, intra):
        rows = rows_of(ui)
        st = s_scr[...]
        o = intra + _dot_nt(p["qs"], st.astype(BF16))
        s_scr[...] = st * p["dec"] + _dot_tn(v_of(rows), p["ks"])
        if prev_ref is not None:
            o = o + prev_ref[0, rows, :]
        o_ref[0, rows, :] = o

    stages = 5
    bs, ps, atts, intra = {}, {}, {}, {}
    for wave in range(nunits + stages - 1):
        for k in range(stages):
            ui = wave - k
            if not 0 <= ui < nunits:
                continue
            if k == 0:
                bs[ui] = decays(ui)
            elif k == 1:
                ps[ui] = operands(ui, bs.pop(ui))
            elif k == 2:
                atts[ui] = scores(ps[ui])
            elif k == 3:
                intra[ui] = _dot(atts.pop(ui), v_of(rows_of(ui)))
            else:
                finish(ui, ps.pop(ui), intra.pop(ui))


def _gla(z, wpre, bpre, reverse, prev):
    b, l, _ = z.shape
    lb = min(4096, l)
    nb = l // lb
    nunits = lb // UNIT
    tri, code = _gla_constants(reverse)

    def seq(i):
        return (nb - 1 - i) if reverse else i

    in_specs = [
        pl.BlockSpec((1, lb, HEAD_W), lambda bi, h, i: (bi, seq(i), h)),
        pl.BlockSpec((1, lb, 128), lambda bi, h, i: (bi, seq(i), Z_LR // 128)),
        pl.BlockSpec((1, 128, HEAD_K), lambda bi, h, i: (h, 0, 0)),
        pl.BlockSpec((1, 1, HEAD_K), lambda bi, h, i: (h, 0, 0)),
        pl.BlockSpec((UNIT, UNIT), lambda bi, h, i: (0, 0)),
        pl.BlockSpec((UNIT, UNIT), lambda bi, h, i: (0, 0)),
    ]
    args = [z, z, wpre, bpre, tri, code]
    if prev is not None:
        in_specs.append(pl.BlockSpec((1, lb, HEAD_V), lambda bi, h, i: (bi, seq(i), h)))
        args.append(prev)
    return pl.pallas_call(
        functools.partial(_gla_kernel, reverse, nunits, prev is not None),
        out_shape=jax.ShapeDtypeStruct((b, l, GLA_HEADS * HEAD_V), F32),
        grid=(b, GLA_HEADS, nb),
        in_specs=in_specs,
        out_specs=pl.BlockSpec((1, lb, HEAD_V), lambda bi, h, i: (bi, seq(i), h)),
        scratch_shapes=[pltpu.VMEM((HEAD_V, HEAD_K), F32)],
        compiler_params=_params(("arbitrary", "arbitrary", "arbitrary")),
        name="gla_bwd" if reverse else "gla_fwd",
    )(*args)


def _mix_kernel(tm, x_ref, o_ref, r_ref, cb_ref, cc_ref, cx_ref, ccp_ref, cxp_ref, ccn_ref, cxn_ref,
                ga_ref, gb_ref, mod_ref, cw_ref, cbias_ref, woa_ref, wob_ref, wo_ref,
                n2_ref, wrt_ref, x1_ref, u2_ref, aff_ref):
    i = pl.program_id(1)
    last = pl.num_programs(1) - 1

    o = o_ref[0]
    parts = []
    for h in range(GLA_HEADS):
        oh = o[:, h * HEAD_V:(h + 1) * HEAD_V]
        ms = jnp.mean(oh * oh, axis=-1, keepdims=True)
        parts.append(oh * lax.rsqrt(ms + EPS))
    on = jnp.concatenate(parts, axis=1)
    r = r_ref[0]
    branch_a = _dot(on.astype(BF16) * (r * _sigmoid(r)), woa_ref[...])

    xc = cc_ref[0].astype(F32) * cx_ref[0].astype(F32)
    prev_row = ccp_ref[0, 15:16, :].astype(F32) * cxp_ref[0, 15:16, :].astype(F32)
    next_row = ccn_ref[0, 0:1, :].astype(F32) * cxn_ref[0, 0:1, :].astype(F32)
    prev_row = jnp.where(i > 0, prev_row, 0.0)
    next_row = jnp.where(i < last, next_row, 0.0)
    rowid = lax.broadcasted_iota(jnp.int32, (tm, 1), 0)
    xm1 = jnp.where(rowid == 0, prev_row, pltpu.roll(xc, 1, axis=0))
    xp1 = jnp.where(rowid == tm - 1, next_row, pltpu.roll(xc, tm - 1, axis=0))
    cw = cw_ref[...]
    conv = xm1 * cw[0:1] + xc * cw[1:2] + xp1 * cw[2:3] + cbias_ref[...]
    branch_b = _dot((cb_ref[0].astype(F32) * conv).astype(BF16), wob_ref[...])

    merged = (_sigmoid(ga_ref[0]) * branch_a.astype(BF16)
              + _sigmoid(gb_ref[0]) * branch_b.astype(BF16))
    mix = _dot(merged, wo_ref[...])

    m = mod_ref[0]
    x1 = x_ref[0] + m[2:3] * mix
    x1_ref[0] = x1
    ms = jnp.mean(x1 * x1, axis=-1, keepdims=True)
    u2 = (x1 * lax.rsqrt(ms + EPS)) * (n2_ref[...] * (1.0 + m[4:5])) + m[3:4]
    u2b = u2.astype(BF16)
    u2_ref[0] = u2b

    logits = _dot_nt(wrt_ref[...], u2b)
    ex = jnp.exp(logits - jnp.max(logits, axis=0, keepdims=True))
    aff = ex / jnp.sum(ex, axis=0, keepdims=True)
    for j in range(tm // SUB):
        aff_ref[j] = aff[:, j * SUB:(j + 1) * SUB]


def _mix_out(x, o, z, mod3, boff, conv_w, conv_b, woa, wob, wo, n2, wrt):
    b, l, d = x.shape
    tm = min(512, l)
    nt = l // tm
    hb = tm // 16
    nhalo = l // 16

    def col(c):
        return lambda bi, i: (bi, i, c // d)

    in_specs = [
        pl.BlockSpec((1, tm, d), lambda bi, i: (bi, i, 0)),
        pl.BlockSpec((1, tm, d), lambda bi, i: (bi, i, 0)),
        pl.BlockSpec((1, tm, d), col(Z_R)),
        pl.BlockSpec((1, tm, d), col(Z_CB)),
        pl.BlockSpec((1, tm, d), col(Z_CC)),
        pl.BlockSpec((1, tm, d), col(Z_CX)),
        pl.BlockSpec((1, 16, d), lambda bi, i: (bi, jnp.maximum(i * hb - 1, 0), Z_CC // d)),
        pl.BlockSpec((1, 16, d), lambda bi, i: (bi, jnp.maximum(i * hb - 1, 0), Z_CX // d)),
        pl.BlockSpec((1, 16, d), lambda bi, i: (bi, jnp.minimum((i + 1) * hb, nhalo - 1), Z_CC // d)),
        pl.BlockSpec((1, 16, d), lambda bi, i: (bi, jnp.minimum((i + 1) * hb, nhalo - 1), Z_CX // d)),
        pl.BlockSpec((1, tm, d), col(Z_GA)),
        pl.BlockSpec((1, tm, d), col(Z_GB)),
        pl.BlockSpec((1, 6, d), lambda bi, i: (bi + boff, 0, 0)),
        pl.BlockSpec((3, d), lambda bi, i: (0, 0)),
        pl.BlockSpec((1, d), lambda bi, i: (0, 0)),
        pl.BlockSpec((d, d), lambda bi, i: (0, 0)),
        pl.BlockSpec((d, d), lambda bi, i: (0, 0)),
        pl.BlockSpec((d, d), lambda bi, i: (0, 0)),
        pl.BlockSpec((1, d), lambda bi, i: (0, 0)),
        pl.BlockSpec((N_EXPERTS, d), lambda bi, i: (0, 0)),
    ]
    out_shape = (jax.ShapeDtypeStruct((b, l, d), F32),
                 jax.ShapeDtypeStruct((b, l, d), BF16),
                 jax.ShapeDtypeStruct((b * l // SUB, N_EXPERTS, SUB), F32))
    out_specs = (pl.BlockSpec((1, tm, d), lambda bi, i: (bi, i, 0)),
                 pl.BlockSpec((1, tm, d), lambda bi, i: (bi, i, 0)),
                 pl.BlockSpec((tm // SUB, N_EXPERTS, SUB), lambda bi, i: (bi * nt + i, 0, 0)))
    return pl.pallas_call(
        functools.partial(_mix_kernel, tm),
        out_shape=out_shape,
        grid=(b, nt),
        in_specs=in_specs,
        out_specs=out_specs,
        compiler_params=_params(("arbitrary", "arbitrary")),
        name="mix_out",
    )(x, o, z, z, z, z, z, z, z, z, z, z, mod3, conv_w, conv_b, woa, wob, wo, n2, wrt)


def _select_kernel(cap, nsb, aff_ref, u_ref, pos_ref, base_ref):
    capf = float(cap)

    def bits(j):
        return pltpu.bitcast(aff_ref[j], jnp.int32)

    def count(pred):
        def body(j, acc):
            return acc + jnp.where(pred(bits(j)), 1.0, 0.0)
        acc = lax.fori_loop(0, nsb, body, jnp.zeros((N_EXPERTS, SUB), F32), unroll=8)
        return jnp.sum(acc, axis=1, keepdims=True)

    def bit_body(t, thr):
        cand = thr | jnp.left_shift(jnp.int32(1), 30 - t)
        return jnp.where(count(lambda v: v >= cand) >= capf, cand, thr)

    thr = lax.fori_loop(0, 31, bit_body, jnp.zeros((N_EXPERTS, SUB), jnp.int32))
    need = capf - count(lambda v: v > thr)

    def blk(j, carry):
        ceq, csel = carry
        v = bits(j)
        gtf = jnp.where(v > thr, 1.0, 0.0)
        eqf = jnp.where(v == thr, 1.0, 0.0)
        r1 = _dot(eqf.astype(BF16), u_ref[...])
        eq_rank = r1[:, :SUB] - eqf + ceq
        sel = gtf + jnp.where(eq_rank < need, eqf, 0.0)
        r2 = _dot(sel.astype(BF16), u_ref[...])
        pos_ref[j] = jnp.where(sel > 0.5, r2[:, :SUB] + csel - 1.0, -1.0).astype(jnp.int32)
        base_ref[j] = csel.astype(jnp.int32)
        return ceq + r1[:, SUB:], csel + r2[:, SUB:]

    zero = jnp.zeros((N_EXPERTS, SUB), F32)
    lax.fori_loop(0, nsb, blk, (zero, zero), unroll=8)


def _select(aff3, cap):
    nsb = aff3.shape[0]
    tri = np.triu(np.ones((SUB, SUB), np.float32))
    umat = jnp.asarray(np.concatenate([tri, np.ones((SUB, SUB), np.float32)], axis=1), BF16)
    shp = jax.ShapeDtypeStruct((nsb, N_EXPERTS, SUB), jnp.int32)
    return pl.pallas_call(
        functools.partial(_select_kernel, cap, nsb),
        out_shape=(shp, shp),
        compiler_params=pltpu.CompilerParams(vmem_limit_bytes=VMEM_LIMIT),
        name="select",
    )(aff3, umat)


def _compact_kernel(nsb, start_ref, pos_ref, x_ref, xe_ref):
    e = pl.program_id(0)

    @pl.when(pl.program_id(1) == 0)
    def _():
        xe_ref[...] = jnp.zeros_like(xe_ref)

    rowi = lax.broadcasted_iota(jnp.int32, (WIN, SUB), 0)

    def body(s, carry):
        st = pl.multiple_of(start_ref[0, 0, s], 16)
        local = pos_ref[s, pl.ds(e, 1), :] - st
        onehot = jnp.where(rowi == local, 1.0, 0.0).astype(BF16)
        xs = x_ref[pl.ds(s * SUB, SUB), :]
        rows = _dot(onehot, xs).astype(BF16)
        xe_ref[0, pl.ds(st, WIN), :] = xe_ref[0, pl.ds(st, WIN), :] + rows
        return carry

    for s in range(nsb):
        body(s, 0)


def _compact(u2, pos3, start, cap, tbig):
    t, d = u2.shape
    nbig = t // tbig
    nsb = tbig // SUB
    return pl.pallas_call(
        functools.partial(_compact_kernel, nsb),
        out_shape=jax.ShapeDtypeStruct((N_EXPERTS, cap + CWIN, d), BF16),
        grid=(N_EXPERTS, nbig),
        in_specs=[pl.BlockSpec((1, 1, nsb), lambda e, b: (e * nbig + b, 0, 0), memory_space=pltpu.SMEM),
                  pl.BlockSpec((nsb, N_EXPERTS, SUB), lambda e, b: (b, 0, 0)),
                  pl.BlockSpec((tbig, d), lambda e, b: (b, 0))],
        out_specs=pl.BlockSpec((1, cap + CWIN, d), lambda e, b: (e, 0, 0)),
        compiler_params=_params(("arbitrary", "arbitrary")),
        name="compact",
    )(start.reshape(N_EXPERTS * nbig, 1, nsb), pos3, u2)


def _compact_small_kernel(nblk, cap, start_ref, next_ref, pos_ref, x_ref, xe_ref, strip_buf, tail_buf, zero_buf,
                          sems, zsem):
    p = pl.program_id(0)
    last = pl.num_programs(0) - 1
    slot = lax.rem(p, 2)

    @pl.when(p == 0)
    def _():
        tail_buf[...] = jnp.zeros_like(tail_buf)
        zero_buf[...] = jnp.zeros_like(zero_buf)

    def strip_copy(sl, e, row0):
        return pltpu.make_async_copy(strip_buf.at[sl, e], xe_ref.at[e, pl.ds(row0, CWIN), :], sems.at[sl, e])

    pos = jnp.concatenate([pos_ref[0], pos_ref[1]], axis=1)
    rowi = lax.broadcasted_iota(jnp.int32, (CWIN, 2 * SUB), 0)
    starts = [pl.multiple_of(start_ref[e * nblk + p], 16) for e in range(N_EXPERTS)]
    onehot = jnp.concatenate(
        [jnp.where(rowi == pos[e:e + 1] - starts[e], 1.0, 0.0).astype(BF16) for e in range(N_EXPERTS)], axis=0)
    rows = _dot(onehot, x_ref[...]).astype(BF16)
    for e in range(N_EXPERTS):
        strip_buf[slot, e, 0:16, :] = rows[e * CWIN:e * CWIN + 16] + tail_buf[e]
        strip_buf[slot, e, 16:CWIN, :] = rows[e * CWIN + 16:(e + 1) * CWIN]
        t0 = next_ref[e * nblk + p] - starts[e]
        t0c = pl.multiple_of(jnp.minimum(t0, CWIN - 16), 16)
        tail_buf[e] = jnp.where(t0 < CWIN, strip_buf[slot, e, pl.ds(t0c, 16), :], jnp.zeros((16, 1), BF16))

    for e in range(N_EXPERTS):
        @pl.when(p > 0)
        def _():
            strip_copy(1 - slot, e, starts[e]).wait()
        strip_copy(slot, e, starts[e]).start()

    @pl.when(p == last)
    def _():
        for e in range(N_EXPERTS):
            strip_copy(slot, e, starts[e]).wait()
            pad = pltpu.make_async_copy(zero_buf, xe_ref.at[e, pl.ds(cap, CWIN), :], zsem)
            pad.start()
            pad.wait()


def _compact_small(u2, pos3, start, start_next, cap):
    t, d = u2.shape
    nblk = t // (2 * SUB)
    grid_spec = pltpu.PrefetchScalarGridSpec(
        num_scalar_prefetch=2,
        grid=(nblk,),
        in_specs=[pl.BlockSpec((2, N_EXPERTS, SUB), lambda p, a, b: (p, 0, 0)),
                  pl.BlockSpec((2 * SUB, d), lambda p, a, b: (p, 0))],
        out_specs=pl.BlockSpec(memory_space=pl.ANY),
        scratch_shapes=[pltpu.VMEM((2, N_EXPERTS, CWIN, d), BF16),
                        pltpu.VMEM((N_EXPERTS, 16, d), BF16),
                        pltpu.VMEM((CWIN, d), BF16),
                        pltpu.SemaphoreType.DMA((2, N_EXPERTS)),
                        pltpu.SemaphoreType.DMA(())],
    )
    return pl.pallas_call(
        functools.partial(_compact_small_kernel, nblk, cap),
        out_shape=jax.ShapeDtypeStruct((N_EXPERTS, cap + CWIN, d), BF16),
        grid_spec=grid_spec,
        compiler_params=_params(("arbitrary",)),
        name="compact_small",
    )(start, start_next, pos3, u2)


def _ffn_kernel(chunks, x_ref, wg_ref, wu_ref, wd_ref, o_ref, acc_ref):
    xs = x_ref[0]
    for i, (c0, cw) in enumerate(chunks):
        g = _dot(xs, wg_ref[0, :, c0:c0 + cw])
        u = _dot(xs, wu_ref[0, :, c0:c0 + cw])
        h = ((g * jax.nn.sigmoid(g)) * u).astype(BF16)
        part = _dot(h, wd_ref[0, c0:c0 + cw, :])
        if i == 0:
            acc_ref[...] = part
        else:
            acc_ref[...] += part
    o_ref[0] = acc_ref[...].astype(BF16)


def _ffn(xe, cap, wg, wu, wd):
    e, _, d = xe.shape
    f = wg.shape[2]
    tm = min(FFN_TM, cap)
    chunks = tuple((c0, min(512, f - c0)) for c0 in range(0, f, 512))
    return pl.pallas_call(
        functools.partial(_ffn_kernel, chunks),
        out_shape=jax.ShapeDtypeStruct((e, cap, d), BF16),
        grid=(e, cap // tm),
        in_specs=[pl.BlockSpec((1, tm, d), lambda ei, m: (ei, m, 0)),
                  pl.BlockSpec((1, d, f), lambda ei, m: (ei, 0, 0)),
                  pl.BlockSpec((1, d, f), lambda ei, m: (ei, 0, 0)),
                  pl.BlockSpec((1, f, d), lambda ei, m: (ei, 0, 0))],
        out_specs=pl.BlockSpec((1, tm, d), lambda ei, m: (ei, m, 0)),
        scratch_shapes=[pltpu.VMEM((tm, d), F32)],
        compiler_params=_params(("arbitrary", "arbitrary")),
        name="expert_ffn",
    )(xe, wg, wu, wd)


def _uncompact_kernel(nsb, nbig, abig_ref, rel_ref, pos_ref, aff_ref, ye_ref, x1_ref, mod_ref, nf_ref,
                      out_ref, acc_ref):
    b = pl.program_id(0)
    e = pl.program_id(1)

    @pl.when(e == 0)
    def _():
        acc_ref[...] = jnp.zeros_like(acc_ref)

    rowi = lax.broadcasted_iota(jnp.int32, (WIN, SUB), 0)
    big = abig_ref[e * nbig + b]

    def body(s, carry):
        rel = pl.multiple_of(rel_ref[0, 0, s], 16)
        local = pos_ref[s, pl.ds(e, 1), :] - (big + rel)
        gate = aff_ref[s, pl.ds(e, 1), :]
        weights = jnp.where(rowi == local, gate, 0.0).astype(BF16)
        rows = pl.ds(s * SUB, SUB)
        acc_ref[rows, :] += _dot_tn(weights, ye_ref[pl.ds(rel, WIN), :])
        return carry

    for s in range(nsb):
        body(s, 0)

    @pl.when(e == pl.num_programs(1) - 1)
    def _():
        xo = x1_ref[...] + mod_ref[0][5:6] * acc_ref[...]
        ms = jnp.mean(xo * xo, axis=-1, keepdims=True)
        out_ref[...] = xo * lax.rsqrt(ms + EPS) * nf_ref[...]


def _uncompact(ye, pos3, aff3, abig, rel, x1, mod3, boff, seq_len, nf, tbig):
    t, d = x1.shape
    cap = ye.shape[1]
    nbig = t // tbig
    nsb = tbig // SUB
    wbig = tbig + 32
    grid_spec = pltpu.PrefetchScalarGridSpec(
        num_scalar_prefetch=1,
        grid=(nbig, N_EXPERTS),
        in_specs=[
            pl.BlockSpec((1, 1, nsb), lambda b, e, ab: (e * nbig + b, 0, 0), memory_space=pltpu.SMEM),
            pl.BlockSpec((nsb, N_EXPERTS, SUB), lambda b, e, ab: (b, 0, 0)),
            pl.BlockSpec((nsb, N_EXPERTS, SUB), lambda b, e, ab: (b, 0, 0)),
            pl.BlockSpec((pl.Element(wbig), pl.Element(d)),
                         lambda b, e, ab: (pl.multiple_of(e * cap + ab[e * nbig + b], 16), 0)),
            pl.BlockSpec((tbig, d), lambda b, e, ab: (b, 0)),
            pl.BlockSpec((1, 6, d), lambda b, e, ab: (boff + (b * tbig) // seq_len, 0, 0)),
            pl.BlockSpec((1, d), lambda b, e, ab: (0, 0)),
        ],
        out_specs=pl.BlockSpec((tbig, d), lambda b, e, ab: (b, 0)),
        scratch_shapes=[pltpu.VMEM((tbig, d), F32)],
    )
    return pl.pallas_call(
        functools.partial(_uncompact_kernel, nsb, nbig),
        out_shape=jax.ShapeDtypeStruct((t, d), F32),
        grid_spec=grid_spec,
        compiler_params=_params(("arbitrary", "arbitrary")),
        name="uncompact",
    )(abig, rel.reshape(N_EXPERTS * nbig, 1, nsb), pos3, aff3, ye.reshape(N_EXPERTS * cap, d), x1, mod3, nf)


def _uncompact_small_kernel(nsub, ab_ref, rel_ref, pos_ref, aff_ref, *refs):
    ye_refs = refs[:N_EXPERTS]
    x1_ref, mod_ref, nf_ref, out_ref = refs[N_EXPERTS:]
    p = pl.program_id(0)
    npair = nsub // GROUP
    rowi = lax.broadcasted_iota(jnp.int32, (SMALL_WIN, SUB), 0)
    ys = []
    for j in range(GROUP):
        pos = pos_ref[j]
        aff = aff_ref[j]
        weights, windows = [], []
        for e in range(N_EXPERTS):
            rel = pl.multiple_of(rel_ref[e * nsub + GROUP * p + j], 16)
            local = pos[e:e + 1] - (ab_ref[e * npair + p] + rel)
            weights.append(jnp.where(rowi == local, aff[e:e + 1], 0.0).astype(BF16))
            windows.append(ye_refs[e][pl.ds(rel, SMALL_WIN), :])
        ys.append(_dot_tn(jnp.concatenate(weights, axis=0), jnp.concatenate(windows, axis=0)))
    xo = x1_ref[...] + mod_ref[0][5:6] * jnp.concatenate(ys, axis=0)
    ms = jnp.mean(xo * xo, axis=-1, keepdims=True)
    out_ref[...] = xo * lax.rsqrt(ms + EPS) * nf_ref[...]


def _uncompact_small(ye, pos3, aff3, ab, rel, x1, mod3, boff, seq_len, nf):
    t, d = x1.shape
    cap = ye.shape[1]
    nsub = t // SUB
    npair = nsub // GROUP

    def window(e):
        return pl.BlockSpec((pl.Element(GROUP_WIN), pl.Element(d)),
                            lambda p, ab_ref, rel_ref: (pl.multiple_of(e * cap + ab_ref[e * npair + p], 16), 0))

    grid_spec = pltpu.PrefetchScalarGridSpec(
        num_scalar_prefetch=2,
        grid=(npair,),
        in_specs=[pl.BlockSpec((GROUP, N_EXPERTS, SUB), lambda p, ab_ref, rel_ref: (p, 0, 0)),
                  pl.BlockSpec((GROUP, N_EXPERTS, SUB), lambda p, ab_ref, rel_ref: (p, 0, 0))]
        + [window(e) for e in range(N_EXPERTS)]
        + [pl.BlockSpec((GROUP * SUB, d), lambda p, ab_ref, rel_ref: (p, 0)),
           pl.BlockSpec((1, 6, d), lambda p, ab_ref, rel_ref: (boff + (p * GROUP * SUB) // seq_len, 0, 0)),
           pl.BlockSpec((1, d), lambda p, ab_ref, rel_ref: (0, 0))],
        out_specs=pl.BlockSpec((GROUP * SUB, d), lambda p, ab_ref, rel_ref: (p, 0)),
    )
    ye2 = ye.reshape(N_EXPERTS * cap, d)
    return pl.pallas_call(
        functools.partial(_uncompact_small_kernel, nsub),
        out_shape=jax.ShapeDtypeStruct((t, d), F32),
        grid_spec=grid_spec,
        compiler_params=_params(("arbitrary",)),
        name="uncompact_small",
    )(ab, rel, pos3, aff3, *([ye2] * N_EXPERTS), x1, mod3, nf)


def _floor16(v):
    return (v // 16) * 16


def _trunk(x, mod3, boff, w, experts):
    b, l, d = x.shape
    t = b * l
    cap = EC_FACTOR * t // N_EXPERTS

    if experts[0].dtype == BF16:
        z = _in_proj(x, mod3, boff, w["norm1_w"], w["w_in"])
    else:
        z, *experts = _in_proj(x, mod3, boff, w["norm1_w"], w["w_in"], experts)
    o = _gla(z, w["wpre_f"], w["bpre_f"], False, None)
    o = _gla(z, w["wpre_b"], w["bpre_b"], True, o)
    x1, u2, aff3 = _mix_out(x, o, z, mod3, boff, w["conv_w"], w["conv_b"],
                            w["w_out_a"], w["w_out_b"], w["w_o"], w["norm2_w"], w["w_router_t"])
    pos3, base3 = _select(aff3, cap)

    base = base3[:, :, 0].T
    start = jnp.minimum(_floor16(base), cap - WIN)
    tbig_c = min(2048, t)
    tbig_u = min(1024, t)
    while tbig_u + 32 > cap:
        tbig_u //= 2
    nsb_u = tbig_u // SUB
    abig = jnp.minimum(_floor16(base[:, ::nsb_u]), cap - (tbig_u + 32))
    rel = start - jnp.repeat(abig, nsb_u, axis=1)

    base_next = jnp.concatenate([base[:, 1:], jnp.full((N_EXPERTS, 1), cap, jnp.int32)], axis=1)
    start_small = jnp.minimum(_floor16(base), cap - SMALL_WIN)
    ab_pair = jnp.minimum(_floor16(base[:, ::GROUP]), cap - GROUP_WIN)
    rel_small = start_small - jnp.repeat(ab_pair, GROUP, axis=1)
    fits_small = (jnp.all(base_next - _floor16(base) <= SMALL_WIN)
                  & jnp.all(rel_small <= GROUP_WIN - SMALL_WIN))

    u2f = u2.reshape(t, d)
    start_blk = _floor16(base[:, ::2])
    next_blk = base_next[:, 1::2]
    xe = lax.cond(
        jnp.all(next_blk - start_blk <= CWIN),
        lambda: _compact_small(u2f, pos3, start_blk.reshape(-1), _floor16(next_blk).reshape(-1), cap),
        lambda: _compact(u2f, pos3, start, cap, tbig_c))
    ye = _ffn(xe, cap, *experts)
    x1f = x1.reshape(t, d)
    out = lax.cond(
        fits_small,
        lambda: _uncompact_small(ye, pos3, aff3, ab_pair.reshape(-1), rel_small.reshape(-1), x1f, mod3, boff,
                                 l, w["norm_f_w"]),
        lambda: _uncompact(ye, pos3, aff3, abig.reshape(-1), rel, x1f, mod3, boff, l, w["norm_f_w"], tbig_u))
    return out.reshape(b, l, d), tuple(experts)


def _prepare_weights(w_in, w_alpha_f, b_alpha_f, w_alpha_b, b_alpha_b, gla_norm_w, conv_w, conv_b,
                     w_out_a, w_out_b, w_o, norm1_w, norm2_w, w_router, norm_f_w):
    d = D_MODEL
    wq, wk, wv, wr, wlf, wlb, wcb, wcc, wcx, wga, wgb = jnp.split(
        w_in, np.cumsum([512, 512, 1024, 1024, 16, 16, 1024, 1024, 1024, 1024])[:].tolist(), axis=1)
    pad = jnp.zeros((d, Z_WIDTH - Z_LR - 2 * GATE_RANK), w_in.dtype)
    wq = wq * (HEAD_K ** -0.5)
    wqkv = jnp.concatenate([wq.reshape(d, GLA_HEADS, HEAD_K), wk.reshape(d, GLA_HEADS, HEAD_K),
                            wv.reshape(d, GLA_HEADS, HEAD_V)], axis=2).reshape(d, GLA_HEADS * HEAD_W)
    w_in_p = jnp.concatenate([wqkv, wr, wcb, wcc, wcx, wga, wgb, wlf, wlb, pad], axis=1).astype(BF16)

    def pre(wa, ba, row0):
        wh = wa.reshape(GATE_RANK, GLA_HEADS, HEAD_K).transpose(1, 0, 2)
        full = jnp.zeros((GLA_HEADS, 128, HEAD_K), F32).at[:, row0:row0 + GATE_RANK].set(wh)
        return full.astype(BF16), ba.reshape(GLA_HEADS, 1, HEAD_K)

    wpre_f, bpre_f = pre(w_alpha_f, b_alpha_f, 0)
    wpre_b, bpre_b = pre(w_alpha_b, b_alpha_b, GATE_RANK)
    return dict(
        w_in=w_in_p, wpre_f=wpre_f, bpre_f=bpre_f, wpre_b=wpre_b, bpre_b=bpre_b,
        conv_w=conv_w, conv_b=conv_b.reshape(1, d),
        w_out_a=(gla_norm_w.reshape(d, 1) * w_out_a).astype(BF16),
        w_out_b=w_out_b.astype(BF16), w_o=w_o.astype(BF16),
        norm1_w=norm1_w, norm2_w=norm2_w.reshape(1, d), w_router_t=w_router.T.astype(BF16),
        norm_f_w=norm_f_w.reshape(1, d))


def kernel(x_prompt, x_sample, c_prompt, c_sample, w_ada, b_ada, norm1_w, w_in, w_alpha_f, b_alpha_f, w_alpha_b, b_alpha_b, gla_norm_w, conv_w, conv_b, w_out_a, w_out_b, w_o, norm2_w, w_router, w_gate, w_up, w_down, norm_f_w):
    d = D_MODEL
    bp, bs = c_prompt.shape[0], c_sample.shape[0]
    nrows = -(-(bp + bs) // 8) * 8
    c_all = jnp.concatenate([c_prompt, c_sample, jnp.zeros((nrows - bp - bs, d), F32)], axis=0)
    mod3 = _adaln(c_all, w_ada[0], b_ada[0]).reshape(nrows, 6, d)
    w = _prepare_weights(w_in[0], w_alpha_f[0], b_alpha_f[0], w_alpha_b[0], b_alpha_b[0], gla_norm_w[0],
                         conv_w[0], conv_b[0], w_out_a[0], w_out_b[0], w_o[0], norm1_w[0], norm2_w[0],
                         w_router[0], norm_f_w)
    y_prompt, experts = _trunk(x_prompt, mod3, 0, w, (w_gate[0], w_up[0], w_down[0]))
    y_sample, _ = _trunk(x_sample, mod3, bp, w, experts)
    return (y_prompt, y_sample)
```

```python
import functools

import numpy as np
import jax
import jax.numpy as jnp
from jax import lax
from jax.experimental import pallas as pl
from jax.experimental.pallas import tpu as pltpu

F32 = jnp.float32
BF16 = jnp.bfloat16

D_MODEL = 1024
GLA_HEADS = 4
HEAD_K = 128
HEAD_V = 256
GATE_RANK = 16
GATE_TAU = 16.0
CHUNK = 64
N_EXPERTS = 16
EC_FACTOR = 2
D_EXPERT = 2816
EPS = 1e-6
LOG2E = 1.4426950408889634

Z_R, Z_CB, Z_CC, Z_CX, Z_GA, Z_GB, Z_LR = 2048, 3072, 4096, 5120, 6144, 7168, 8192
HEAD_W = 2 * HEAD_K + HEAD_V
Z_WIDTH = 8448

UNIT = 4 * CHUNK
SUB = 128
WIN = SUB + 16
SMALL_WIN = 64
GROUP = 4
GROUP_WIN = SMALL_WIN + 48 * (GROUP - 1)
CWIN = 96
FFN_TM = 1024
V7X_VMEM_BYTES = 64 * 1024 * 1024
VMEM_LIMIT = V7X_VMEM_BYTES - 8 * 1024 * 1024


def _dot(a, b):
    return jnp.dot(a, b, preferred_element_type=F32)


def _dot_nt(a, b):
    return lax.dot_general(a, b, (((1,), (1,)), ((), ())), preferred_element_type=F32)


def _sigmoid(x):
    return 0.5 * jnp.tanh(0.5 * x) + 0.5


def _dot_tn(a, b):
    return lax.dot_general(a, b, (((0,), (0,)), ((), ())), preferred_element_type=F32)


def _params(sem, limit=VMEM_LIMIT):
    return pltpu.CompilerParams(dimension_semantics=sem, vmem_limit_bytes=limit)


def _adaln_kernel(c_ref, w_ref, b_ref, o_ref):
    c = c_ref[...]
    s = c * jax.nn.sigmoid(c)
    o_ref[...] = jnp.dot(s, w_ref[...], preferred_element_type=F32,
                         precision=lax.Precision.HIGHEST) + b_ref[...]


def _adaln(c_all, w_ada, b_ada):
    nb, d = c_all.shape
    n = w_ada.shape[1]
    tn = 1536
    return pl.pallas_call(
        _adaln_kernel,
        out_shape=jax.ShapeDtypeStruct((nb, n), F32),
        grid=(n // tn,),
        in_specs=[pl.BlockSpec((nb, d), lambda j: (0, 0)),
                  pl.BlockSpec((d, tn), lambda j: (0, j)),
                  pl.BlockSpec((1, tn), lambda j: (0, j))],
        out_specs=pl.BlockSpec((nb, tn), lambda j: (0, j)),
        compiler_params=_params(("arbitrary",)),
        name="adaln",
    )(c_all, w_ada, b_ada.reshape(1, n))


CAST_SPLIT = 8


def _inproj_kernel(ncast, x_ref, mod_ref, n1_ref, w_ref, *refs):
    z_ref = refs[len(refs) // 2]
    x = x_ref[0]
    ms = jnp.mean(x * x, axis=-1, keepdims=True)
    m = mod_ref[0]
    u = (x * lax.rsqrt(ms + EPS)) * (n1_ref[...] * (1.0 + m[1:2])) + m[0:1]
    z_ref[0] = _dot(u.astype(BF16), w_ref[...]).astype(BF16)

    if ncast:
        lin = ((pl.program_id(0) * pl.num_programs(1) + pl.program_id(1)) * pl.num_programs(2)
               + pl.program_id(2))

        @pl.when(lin < ncast)
        def _():
            for src, dst in zip(refs[:3], refs[4:]):
                dst[...] = src[...].astype(BF16)


def _in_proj(x, mod3, boff, norm1_w, w_in_p, expert_weights=None):
    b, l, d = x.shape
    tm = min(1024, l)
    tn = 2816
    nt, nn = l // tm, Z_WIDTH // tn
    in_specs = [pl.BlockSpec((1, tm, d), lambda n, bi, i: (bi, i, 0)),
                pl.BlockSpec((1, 6, d), lambda n, bi, i: (bi + boff, 0, 0)),
                pl.BlockSpec((1, d), lambda n, bi, i: (0, 0)),
                pl.BlockSpec((d, tn), lambda n, bi, i: (0, n))]
    out_specs = [pl.BlockSpec((1, tm, tn), lambda n, bi, i: (bi, i, n))]
    out_shape = [jax.ShapeDtypeStruct((b, l, Z_WIDTH), BF16)]
    args = [x, mod3, norm1_w.reshape(1, d), w_in_p]
    ncast = 0
    if expert_weights is not None and N_EXPERTS * CAST_SPLIT > b * nt * nn:
        return (_in_proj(x, mod3, boff, norm1_w, w_in_p),) + tuple(wt.astype(BF16) for wt in expert_weights)
    if expert_weights is not None:
        ncast = N_EXPERTS * CAST_SPLIT

        def slab(n, bi, i):
            j = jnp.minimum((n * b + bi) * nt + i, ncast - 1)
            return (j // CAST_SPLIT, j % CAST_SPLIT, 0)

        for wt in expert_weights:
            blk = (1, wt.shape[1] // CAST_SPLIT, wt.shape[2])
            in_specs.append(pl.BlockSpec(blk, slab))
            out_specs.append(pl.BlockSpec(blk, slab))
            out_shape.append(jax.ShapeDtypeStruct(wt.shape, BF16))
            args.append(wt)
    res = pl.pallas_call(
        functools.partial(_inproj_kernel, ncast),
        out_shape=tuple(out_shape),
        grid=(nn, b, nt),
        in_specs=in_specs,
        out_specs=tuple(out_specs),
        compiler_params=_params(("arbitrary", "arbitrary", "arbitrary")),
        name="in_proj",
    )(*args)
    return res if expert_weights is not None else res[0]


def _gla_constants(reverse):
    r = np.arange(UNIT)[:, None]
    s = np.arange(UNIT)[None, :]
    cr, cs = r // CHUNK, s // CHUNK
    if reverse:
        tri = (cr == cs) & (s >= r)
        tr, ts = 3 - cr, 3 - cs
    else:
        tri = (cr == cs) & (s <= r)
        tr, ts = cr, cs
    code = np.zeros((UNIT, UNIT), np.int32)
    code[tri] = 1
    return jnp.asarray(tri, BF16), jnp.asarray(code)


def _gla_kernel(reverse, nunits, has_prev, *refs):
    if has_prev:
        qkv_ref, lr_ref, wp_ref, bp_ref, tri_ref, code_ref, prev_ref, o_ref, s_scr = refs
    else:
        qkv_ref, lr_ref, wp_ref, bp_ref, tri_ref, code_ref, o_ref, s_scr = refs
        prev_ref = None

    def v_of(rows):
        return qkv_ref[0, rows, 2 * HEAD_K:HEAD_W]

    @pl.when(pl.program_id(2) == 0)
    def _():
        s_scr[...] = jnp.zeros_like(s_scr)

    order = (3, 2, 1, 0) if reverse else (0, 1, 2, 3)

    def to_mem(by_step):
        out = [None] * 4
        for t in range(4):
            out[order[t]] = by_step[t]
        return out

    def scale_chunks(arr, logf):
        parts = []
        for c in range(4):
            part = arr[c * CHUNK:(c + 1) * CHUNK]
            if logf[c] is not None:
                part = part * jnp.exp2(logf[c])
            parts.append(part)
        return jnp.concatenate(parts, axis=0).astype(BF16)

    def rows_of(ui):
        um = (nunits - 1 - ui) if reverse else ui
        return pl.ds(um * UNIT, UNIT)

    def decays(ui):
        pre = _dot(lr_ref[0, rows_of(ui), :], wp_ref[0]) + bp_ref[0]
        soft = jnp.log2(1.0 + jnp.exp2(jnp.abs(pre) * (-LOG2E)))
        la = (jnp.minimum(pre, 0.0) * LOG2E - soft) * (1.0 / GATE_TAU)
        hi = la.astype(BF16)
        lo = (la - hi.astype(F32)).astype(BF16)
        bb = _dot(tri_ref[...], jnp.concatenate([hi, lo], axis=1))
        return bb[:, :HEAD_K] + bb[:, HEAD_K:]

    def operands(ui, b):
        rows = rows_of(ui)
        edge = 0 if reverse else CHUNK - 1
        tot = [b[c * CHUNK + edge:c * CHUNK + edge + 1] for c in range(4)]
        tot_full = jnp.concatenate([jnp.broadcast_to(t, (CHUNK, HEAD_K)) for t in tot], axis=0)
        q = qkv_ref[0, rows, 0:HEAD_K].astype(F32)
        k = qkv_ref[0, rows, HEAD_K:2 * HEAD_K].astype(F32)
        q_t = q * jnp.exp2(b)
        k_t = (k * jnp.exp2(-b)).astype(BF16)
        k_e = k * jnp.exp2(tot_full - b)
        q_tb = q_t.astype(BF16)
        k_eb = k_e.astype(BF16)
        bt = [tot[o] for o in order]

        zero = jnp.zeros((CHUNK, HEAD_K), BF16)
        e1 = jnp.exp2(bt[1])
        e2 = jnp.exp2(bt[2])
        qrows, krows = [None] * 4, [None] * 4
        for t in range(4):
            c = order[t]
            sl = slice(c * CHUNK, (c + 1) * CHUNK)
            qc, kc = q_tb[sl], k_eb[sl]
            q3 = qc if t == 2 else ((q_t[sl] * e2).astype(BF16) if t == 3 else zero)
            k3 = (k_e[sl] * e1).astype(BF16) if t == 0 else (kc if t == 1 else zero)
            qrows[c] = jnp.concatenate([qc if t == 1 else zero, qc if t == 3 else zero, q3], axis=1)
            krows[c] = jnp.concatenate([kc if t == 0 else zero, kc if t == 2 else zero, k3], axis=1)

        gq = to_mem([None, bt[0], bt[0] + bt[1], bt[0] + bt[1] + bt[2]])
        gk = to_mem([bt[1] + bt[2] + bt[3], bt[2] + bt[3], bt[3], None])
        dec = jnp.exp2(bt[0] + bt[1] + bt[2] + bt[3])
        return dict(q_t=q_tb, k_t=k_t, q_cat=jnp.concatenate(qrows, axis=0), k_cat=jnp.concatenate(krows, axis=0),
                    qs=scale_chunks(q_t, gq), ks=scale_chunks(k_e, gk), dec=dec)

    def scores(p):
        same = _dot_nt(p["q_t"], p["k_t"])
        cross = _dot_nt(p["q_cat"], p["k_cat"])
        return jnp.where(code_ref[...] == 1, same, cross).astype(BF16)

    def finish(ui, p, intra):
        rows = rows_of(ui)
        st = s_scr[...]
        o = intra + _dot_nt(p["qs"], st.astype(BF16))
        s_scr[...] = st * p["dec"] + _dot_tn(v_of(rows), p["ks"])
        if prev_ref is not None:
            o = o + prev_ref[0, rows, :]
        o_ref[0, rows, :] = o

    stages = 5
    bs, ps, atts, intra = {}, {}, {}, {}
    for wave in range(nunits + stages - 1):
        for k in range(stages):
            ui = wave - k
            if not 0 <= ui < nunits:
                continue
            if k == 0:
                bs[ui] = decays(ui)
            elif k == 1:
                ps[ui] = operands(ui, bs.pop(ui))
            elif k == 2:
                atts[ui] = scores(ps[ui])
            elif k == 3:
                intra[ui] = _dot(atts.pop(ui), v_of(rows_of(ui)))
            else:
                finish(ui, ps.pop(ui), intra.pop(ui))


def _gla(z, wpre, bpre, reverse, prev):
    b, l, _ = z.shape
    lb = min(4096, l)
    nb = l // lb
    nunits = lb // UNIT
    tri, code = _gla_constants(reverse)

    def seq(i):
        return (nb - 1 - i) if reverse else i

    in_specs = [
        pl.BlockSpec((1, lb, HEAD_W), lambda bi, h, i: (bi, seq(i), h)),
        pl.BlockSpec((1, lb, 128), lambda bi, h, i: (bi, seq(i), Z_LR // 128)),
        pl.BlockSpec((1, 128, HEAD_K), lambda bi, h, i: (h, 0, 0)),
        pl.BlockSpec((1, 1, HEAD_K), lambda bi, h, i: (h, 0, 0)),
        pl.BlockSpec((UNIT, UNIT), lambda bi, h, i: (0, 0)),
        pl.BlockSpec((UNIT, UNIT), lambda bi, h, i: (0, 0)),
    ]
    args = [z, z, wpre, bpre, tri, code]
    if prev is not None:
        in_specs.append(pl.BlockSpec((1, lb, HEAD_V), lambda bi, h, i: (bi, seq(i), h)))
        args.append(prev)
    return pl.pallas_call(
        functools.partial(_gla_kernel, reverse, nunits, prev is not None),
        out_shape=jax.ShapeDtypeStruct((b, l, GLA_HEADS * HEAD_V), F32),
        grid=(b, GLA_HEADS, nb),
        in_specs=in_specs,
        out_specs=pl.BlockSpec((1, lb, HEAD_V), lambda bi, h, i: (bi, seq(i), h)),
        scratch_shapes=[pltpu.VMEM((HEAD_V, HEAD_K), F32)],
        compiler_params=_params(("arbitrary", "arbitrary", "arbitrary")),
        name="gla_bwd" if reverse else "gla_fwd",
    )(*args)


def _mix_kernel(tm, x_ref, o_ref, r_ref, cb_ref, cc_ref, cx_ref, ccp_ref, cxp_ref, ccn_ref, cxn_ref,
                ga_ref, gb_ref, mod_ref, cw_ref, cbias_ref, woa_ref, wob_ref, wo_ref,
                n2_ref, wrt_ref, x1_ref, u2_ref, aff_ref):
    i = pl.program_id(1)
    last = pl.num_programs(1) - 1

    o = o_ref[0]
    parts = []
    for h in range(GLA_HEADS):
        oh = o[:, h * HEAD_V:(h + 1) * HEAD_V]
        ms = jnp.mean(oh * oh, axis=-1, keepdims=True)
        parts.append(oh * lax.rsqrt(ms + EPS))
    on = jnp.concatenate(parts, axis=1)
    r = r_ref[0]
    branch_a = _dot(on.astype(BF16) * (r * _sigmoid(r)), woa_ref[...])

    xc = (cc_ref[0] * cx_ref[0]).astype(F32)
    prev_row = ccp_ref[0, 15:16, :].astype(F32) * cxp_ref[0, 15:16, :].astype(F32)
    next_row = ccn_ref[0, 0:1, :].astype(F32) * cxn_ref[0, 0:1, :].astype(F32)
    prev_row = jnp.where(i > 0, prev_row, 0.0)
    next_row = jnp.where(i < last, next_row, 0.0)
    rowid = lax.broadcasted_iota(jnp.int32, (tm, 1), 0)
    xm1 = jnp.where(rowid == 0, prev_row, pltpu.roll(xc, 1, axis=0))
    xp1 = jnp.where(rowid == tm - 1, next_row, pltpu.roll(xc, tm - 1, axis=0))
    cw = cw_ref[...]
    conv = xm1 * cw[0:1] + xc * cw[1:2] + xp1 * cw[2:3] + cbias_ref[...]
    branch_b = _dot(cb_ref[0] * conv.astype(BF16), wob_ref[...])

    merged = (_sigmoid(ga_ref[0]) * branch_a.astype(BF16)
              + _sigmoid(gb_ref[0]) * branch_b.astype(BF16))
    mix = _dot(merged, wo_ref[...])

    m = mod_ref[0]
    x1 = x_ref[0] + m[2:3] * mix
    x1_ref[0] = x1
    ms = jnp.mean(x1 * x1, axis=-1, keepdims=True)
    u2 = (x1 * lax.rsqrt(ms + EPS)) * (n2_ref[...] * (1.0 + m[4:5])) + m[3:4]
    u2b = u2.astype(BF16)
    u2_ref[0] = u2b

    logits = _dot_nt(wrt_ref[...], u2b)
    ex = jnp.exp(logits - jnp.max(logits, axis=0, keepdims=True))
    aff = ex / jnp.sum(ex, axis=0, keepdims=True)
    for j in range(tm // SUB):
        aff_ref[j] = aff[:, j * SUB:(j + 1) * SUB]


def _mix_out(x, o, z, mod3, boff, conv_w, conv_b, woa, wob, wo, n2, wrt):
    b, l, d = x.shape
    tm = min(512, l)
    nt = l // tm
    hb = tm // 16
    nhalo = l // 16

    def col(c):
        return lambda bi, i: (bi, i, c // d)

    in_specs = [
        pl.BlockSpec((1, tm, d), lambda bi, i: (bi, i, 0)),
        pl.BlockSpec((1, tm, d), lambda bi, i: (bi, i, 0)),
        pl.BlockSpec((1, tm, d), col(Z_R)),
        pl.BlockSpec((1, tm, d), col(Z_CB)),
        pl.BlockSpec((1, tm, d), col(Z_CC)),
        pl.BlockSpec((1, tm, d), col(Z_CX)),
        pl.BlockSpec((1, 16, d), lambda bi, i: (bi, jnp.maximum(i * hb - 1, 0), Z_CC // d)),
        pl.BlockSpec((1, 16, d), lambda bi, i: (bi, jnp.maximum(i * hb - 1, 0), Z_CX // d)),
        pl.BlockSpec((1, 16, d), lambda bi, i: (bi, jnp.minimum((i + 1) * hb, nhalo - 1), Z_CC // d)),
        pl.BlockSpec((1, 16, d), lambda bi, i: (bi, jnp.minimum((i + 1) * hb, nhalo - 1), Z_CX // d)),
        pl.BlockSpec((1, tm, d), col(Z_GA)),
        pl.BlockSpec((1, tm, d), col(Z_GB)),
        pl.BlockSpec((1, 6, d), lambda bi, i: (bi + boff, 0, 0)),
        pl.BlockSpec((3, d), lambda bi, i: (0, 0)),
        pl.BlockSpec((1, d), lambda bi, i: (0, 0)),
        pl.BlockSpec((d, d), lambda bi, i: (0, 0)),
        pl.BlockSpec((d, d), lambda bi, i: (0, 0)),
        pl.BlockSpec((d, d), lambda bi, i: (0, 0)),
        pl.BlockSpec((1, d), lambda bi, i: (0, 0)),
        pl.BlockSpec((N_EXPERTS, d), lambda bi, i: (0, 0)),
    ]
    out_shape = (jax.ShapeDtypeStruct((b, l, d), F32),
                 jax.ShapeDtypeStruct((b, l, d), BF16),
                 jax.ShapeDtypeStruct((b * l // SUB, N_EXPERTS, SUB), F32))
    out_specs = (pl.BlockSpec((1, tm, d), lambda bi, i: (bi, i, 0)),
                 pl.BlockSpec((1, tm, d), lambda bi, i: (bi, i, 0)),
                 pl.BlockSpec((tm // SUB, N_EXPERTS, SUB), lambda bi, i: (bi * nt + i, 0, 0)))
    return pl.pallas_call(
        functools.partial(_mix_kernel, tm),
        out_shape=out_shape,
        grid=(b, nt),
        in_specs=in_specs,
        out_specs=out_specs,
        compiler_params=_params(("arbitrary", "arbitrary")),
        name="mix_out",
    )(x, o, z, z, z, z, z, z, z, z, z, z, mod3, conv_w, conv_b, woa, wob, wo, n2, wrt)


def _select_kernel(cap, nsb, aff_ref, u_ref, pos_ref, base_ref):
    capf = float(cap)

    def bits(j):
        return pltpu.bitcast(aff_ref[j], jnp.int32)

    def count(pred):
        def body(j, acc):
            return acc + jnp.where(pred(bits(j)), 1.0, 0.0)
        acc = lax.fori_loop(0, nsb, body, jnp.zeros((N_EXPERTS, SUB), F32), unroll=8)
        return jnp.sum(acc, axis=1, keepdims=True)

    def bit_body(t, thr):
        cand = thr | jnp.left_shift(jnp.int32(1), 30 - t)
        return jnp.where(count(lambda v: v >= cand) >= capf, cand, thr)

    thr = lax.fori_loop(0, 31, bit_body, jnp.zeros((N_EXPERTS, SUB), jnp.int32))
    need = capf - count(lambda v: v > thr)

    def blk(j, carry):
        ceq, csel = carry
        v = bits(j)
        gtf = jnp.where(v > thr, 1.0, 0.0)
        eqf = jnp.where(v == thr, 1.0, 0.0)
        r1 = _dot(eqf.astype(BF16), u_ref[...])
        eq_rank = r1[:, :SUB] - eqf + ceq
        sel = gtf + jnp.where(eq_rank < need, eqf, 0.0)
        r2 = _dot(sel.astype(BF16), u_ref[...])
        pos_ref[j] = jnp.where(sel > 0.5, r2[:, :SUB] + csel - 1.0, -1.0).astype(jnp.int32)
        base_ref[j] = csel.astype(jnp.int32)
        return ceq + r1[:, SUB:], csel + r2[:, SUB:]

    zero = jnp.zeros((N_EXPERTS, SUB), F32)
    lax.fori_loop(0, nsb, blk, (zero, zero), unroll=8)


def _select(aff3, cap):
    nsb = aff3.shape[0]
    tri = np.triu(np.ones((SUB, SUB), np.float32))
    umat = jnp.asarray(np.concatenate([tri, np.ones((SUB, SUB), np.float32)], axis=1), BF16)
    shp = jax.ShapeDtypeStruct((nsb, N_EXPERTS, SUB), jnp.int32)
    return pl.pallas_call(
        functools.partial(_select_kernel, cap, nsb),
        out_shape=(shp, shp),
        compiler_params=pltpu.CompilerParams(vmem_limit_bytes=VMEM_LIMIT),
        name="select",
    )(aff3, umat)


def _compact_kernel(nsb, start_ref, pos_ref, x_ref, xe_ref):
    e = pl.program_id(0)

    @pl.when(pl.program_id(1) == 0)
    def _():
        xe_ref[...] = jnp.zeros_like(xe_ref)

    rowi = lax.broadcasted_iota(jnp.int32, (WIN, SUB), 0)

    def body(s, carry):
        st = pl.multiple_of(start_ref[0, 0, s], 16)
        local = pos_ref[s, pl.ds(e, 1), :] - st
        onehot = jnp.where(rowi == local, 1.0, 0.0).astype(BF16)
        xs = x_ref[pl.ds(s * SUB, SUB), :]
        rows = _dot(onehot, xs).astype(BF16)
        xe_ref[0, pl.ds(st, WIN), :] = xe_ref[0, pl.ds(st, WIN), :] + rows
        return carry

    for s in range(nsb):
        body(s, 0)


def _compact(u2, pos3, start, cap, tbig):
    t, d = u2.shape
    nbig = t // tbig
    nsb = tbig // SUB
    return pl.pallas_call(
        functools.partial(_compact_kernel, nsb),
        out_shape=jax.ShapeDtypeStruct((N_EXPERTS, cap + CWIN, d), BF16),
        grid=(N_EXPERTS, nbig),
        in_specs=[pl.BlockSpec((1, 1, nsb), lambda e, b: (e * nbig + b, 0, 0), memory_space=pltpu.SMEM),
                  pl.BlockSpec((nsb, N_EXPERTS, SUB), lambda e, b: (b, 0, 0)),
                  pl.BlockSpec((tbig, d), lambda e, b: (b, 0))],
        out_specs=pl.BlockSpec((1, cap + CWIN, d), lambda e, b: (e, 0, 0)),
        compiler_params=_params(("arbitrary", "arbitrary")),
        name="compact",
    )(start.reshape(N_EXPERTS * nbig, 1, nsb), pos3, u2)


def _compact_small_kernel(nblk, cap, start_ref, next_ref, pos_ref, x_ref, xe_ref, strip_buf, tail_buf, zero_buf,
                          sems, zsem):
    p = pl.program_id(0)
    last = pl.num_programs(0) - 1
    slot = lax.rem(p, 2)

    @pl.when(p == 0)
    def _():
        tail_buf[...] = jnp.zeros_like(tail_buf)
        zero_buf[...] = jnp.zeros_like(zero_buf)

    def strip_copy(sl, e, row0):
        return pltpu.make_async_copy(strip_buf.at[sl, e], xe_ref.at[e, pl.ds(row0, CWIN), :], sems.at[sl, e])

    pos = jnp.concatenate([pos_ref[0], pos_ref[1]], axis=1)
    rowi = lax.broadcasted_iota(jnp.int32, (CWIN, 2 * SUB), 0)
    starts = [pl.multiple_of(start_ref[e * nblk + p], 16) for e in range(N_EXPERTS)]
    onehot = jnp.concatenate(
        [jnp.where(rowi == pos[e:e + 1] - starts[e], 1.0, 0.0).astype(BF16) for e in range(N_EXPERTS)], axis=0)
    rows = _dot(onehot, x_ref[...]).astype(BF16)
    for e in range(N_EXPERTS):
        strip_buf[slot, e, 0:16, :] = rows[e * CWIN:e * CWIN + 16] + tail_buf[e]
        strip_buf[slot, e, 16:CWIN, :] = rows[e * CWIN + 16:(e + 1) * CWIN]
        t0 = next_ref[e * nblk + p] - starts[e]
        t0c = pl.multiple_of(jnp.minimum(t0, CWIN - 16), 16)
        tail_buf[e] = jnp.where(t0 < CWIN, strip_buf[slot, e, pl.ds(t0c, 16), :], jnp.zeros((16, 1), BF16))

    for e in range(N_EXPERTS):
        @pl.when(p > 0)
        def _():
            strip_copy(1 - slot, e, starts[e]).wait()
        strip_copy(slot, e, starts[e]).start()

    @pl.when(p == last)
    def _():
        for e in range(N_EXPERTS):
            strip_copy(slot, e, starts[e]).wait()
            pad = pltpu.make_async_copy(zero_buf, xe_ref.at[e, pl.ds(cap, CWIN), :], zsem)
            pad.start()
            pad.wait()


def _compact_small(u2, pos3, start, start_next, cap):
    t, d = u2.shape
    nblk = t // (2 * SUB)
    grid_spec = pltpu.PrefetchScalarGridSpec(
        num_scalar_prefetch=2,
        grid=(nblk,),
        in_specs=[pl.BlockSpec((2, N_EXPERTS, SUB), lambda p, a, b: (p, 0, 0)),
                  pl.BlockSpec((2 * SUB, d), lambda p, a, b: (p, 0))],
        out_specs=pl.BlockSpec(memory_space=pl.ANY),
        scratch_shapes=[pltpu.VMEM((2, N_EXPERTS, CWIN, d), BF16),
                        pltpu.VMEM((N_EXPERTS, 16, d), BF16),
                        pltpu.VMEM((CWIN, d), BF16),
                        pltpu.SemaphoreType.DMA((2, N_EXPERTS)),
                        pltpu.SemaphoreType.DMA(())],
    )
    return pl.pallas_call(
        functools.partial(_compact_small_kernel, nblk, cap),
        out_shape=jax.ShapeDtypeStruct((N_EXPERTS, cap + CWIN, d), BF16),
        grid_spec=grid_spec,
        compiler_params=_params(("arbitrary",)),
        name="compact_small",
    )(start, start_next, pos3, u2)


def _ffn_kernel(chunks, x_ref, wg_ref, wu_ref, wd_ref, o_ref, acc_ref):
    xs = x_ref[0]
    for i, (c0, cw) in enumerate(chunks):
        g = _dot(xs, wg_ref[0, :, c0:c0 + cw])
        u = _dot(xs, wu_ref[0, :, c0:c0 + cw])
        h = ((g * jax.nn.sigmoid(g)) * u).astype(BF16)
        part = _dot(h, wd_ref[0, c0:c0 + cw, :])
        if i == 0:
            acc_ref[...] = part
        else:
            acc_ref[...] += part
    o_ref[0] = acc_ref[...].astype(BF16)


def _ffn(xe, cap, wg, wu, wd):
    e, _, d = xe.shape
    f = wg.shape[2]
    tm = min(FFN_TM, cap)
    chunks = tuple((c0, min(512, f - c0)) for c0 in range(0, f, 512))
    return pl.pallas_call(
        functools.partial(_ffn_kernel, chunks),
        out_shape=jax.ShapeDtypeStruct((e, cap, d), BF16),
        grid=(e, cap // tm),
        in_specs=[pl.BlockSpec((1, tm, d), lambda ei, m: (ei, m, 0)),
                  pl.BlockSpec((1, d, f), lambda ei, m: (ei, 0, 0)),
                  pl.BlockSpec((1, d, f), lambda ei, m: (ei, 0, 0)),
                  pl.BlockSpec((1, f, d), lambda ei, m: (ei, 0, 0))],
        out_specs=pl.BlockSpec((1, tm, d), lambda ei, m: (ei, m, 0)),
        scratch_shapes=[pltpu.VMEM((tm, d), F32)],
        compiler_params=_params(("arbitrary", "arbitrary")),
        name="expert_ffn",
    )(xe, wg, wu, wd)


def _uncompact_kernel(nsb, nbig, abig_ref, rel_ref, pos_ref, aff_ref, ye_ref, x1_ref, mod_ref, nf_ref,
                      out_ref, acc_ref):
    b = pl.program_id(0)
    e = pl.program_id(1)

    @pl.when(e == 0)
    def _():
        acc_ref[...] = jnp.zeros_like(acc_ref)

    rowi = lax.broadcasted_iota(jnp.int32, (WIN, SUB), 0)
    big = abig_ref[e * nbig + b]

    def body(s, carry):
        rel = pl.multiple_of(rel_ref[0, 0, s], 16)
        local = pos_ref[s, pl.ds(e, 1), :] - (big + rel)
        gate = aff_ref[s, pl.ds(e, 1), :]
        weights = jnp.where(rowi == local, gate, 0.0).astype(BF16)
        rows = pl.ds(s * SUB, SUB)
        acc_ref[rows, :] += _dot_tn(weights, ye_ref[pl.ds(rel, WIN), :])
        return carry

    for s in range(nsb):
        body(s, 0)

    @pl.when(e == pl.num_programs(1) - 1)
    def _():
        xo = x1_ref[...] + mod_ref[0][5:6] * acc_ref[...]
        ms = jnp.mean(xo * xo, axis=-1, keepdims=True)
        out_ref[...] = xo * lax.rsqrt(ms + EPS) * nf_ref[...]


def _uncompact(ye, pos3, aff3, abig, rel, x1, mod3, boff, seq_len, nf, tbig):
    t, d = x1.shape
    cap = ye.shape[1]
    nbig = t // tbig
    nsb = tbig // SUB
    wbig = tbig + 32
    grid_spec = pltpu.PrefetchScalarGridSpec(
        num_scalar_prefetch=1,
        grid=(nbig, N_EXPERTS),
        in_specs=[
            pl.BlockSpec((1, 1, nsb), lambda b, e, ab: (e * nbig + b, 0, 0), memory_space=pltpu.SMEM),
            pl.BlockSpec((nsb, N_EXPERTS, SUB), lambda b, e, ab: (b, 0, 0)),
            pl.BlockSpec((nsb, N_EXPERTS, SUB), lambda b, e, ab: (b, 0, 0)),
            pl.BlockSpec((pl.Element(wbig), pl.Element(d)),
                         lambda b, e, ab: (pl.multiple_of(e * cap + ab[e * nbig + b], 16), 0)),
            pl.BlockSpec((tbig, d), lambda b, e, ab: (b, 0)),
            pl.BlockSpec((1, 6, d), lambda b, e, ab: (boff + (b * tbig) // seq_len, 0, 0)),
            pl.BlockSpec((1, d), lambda b, e, ab: (0, 0)),
        ],
        out_specs=pl.BlockSpec((tbig, d), lambda b, e, ab: (b, 0)),
        scratch_shapes=[pltpu.VMEM((tbig, d), F32)],
    )
    return pl.pallas_call(
        functools.partial(_uncompact_kernel, nsb, nbig),
        out_shape=jax.ShapeDtypeStruct((t, d), F32),
        grid_spec=grid_spec,
        compiler_params=_params(("arbitrary", "arbitrary")),
        name="uncompact",
    )(abig, rel.reshape(N_EXPERTS * nbig, 1, nsb), pos3, aff3, ye.reshape(N_EXPERTS * cap, d), x1, mod3, nf)


def _uncompact_small_kernel(nsub, ab_ref, rel_ref, pos_ref, aff_ref, *refs):
    ye_refs = refs[:N_EXPERTS]
    x1_ref, mod_ref, nf_ref, out_ref = refs[N_EXPERTS:]
    p = pl.program_id(0)
    npair = nsub // GROUP
    rowi = lax.broadcasted_iota(jnp.int32, (SMALL_WIN, SUB), 0)
    ys = []
    for j in range(GROUP):
        pos = pos_ref[j]
        aff = aff_ref[j]
        weights, windows = [], []
        for e in range(N_EXPERTS):
            rel = pl.multiple_of(rel_ref[e * nsub + GROUP * p + j], 16)
            local = pos[e:e + 1] - (ab_ref[e * npair + p] + rel)
            weights.append(jnp.where(rowi == local, aff[e:e + 1], 0.0).astype(BF16))
            windows.append(ye_refs[e][pl.ds(rel, SMALL_WIN), :])
        ys.append(_dot_tn(jnp.concatenate(weights, axis=0), jnp.concatenate(windows, axis=0)))
    xo = x1_ref[...] + mod_ref[0][5:6] * jnp.concatenate(ys, axis=0)
    ms = jnp.mean(xo * xo, axis=-1, keepdims=True)
    out_ref[...] = xo * lax.rsqrt(ms + EPS) * nf_ref[...]


def _uncompact_small(ye, pos3, aff3, ab, rel, x1, mod3, boff, seq_len, nf):
    t, d = x1.shape
    cap = ye.shape[1]
    nsub = t // SUB
    npair = nsub // GROUP

    def window(e):
        return pl.BlockSpec((pl.Element(GROUP_WIN), pl.Element(d)),
                            lambda p, ab_ref, rel_ref: (pl.multiple_of(e * cap + ab_ref[e * npair + p], 16), 0))

    grid_spec = pltpu.PrefetchScalarGridSpec(
        num_scalar_prefetch=2,
        grid=(npair,),
        in_specs=[pl.BlockSpec((GROUP, N_EXPERTS, SUB), lambda p, ab_ref, rel_ref: (p, 0, 0)),
                  pl.BlockSpec((GROUP, N_EXPERTS, SUB), lambda p, ab_ref, rel_ref: (p, 0, 0))]
        + [window(e) for e in range(N_EXPERTS)]
        + [pl.BlockSpec((GROUP * SUB, d), lambda p, ab_ref, rel_ref: (p, 0)),
           pl.BlockSpec((1, 6, d), lambda p, ab_ref, rel_ref: (boff + (p * GROUP * SUB) // seq_len, 0, 0)),
           pl.BlockSpec((1, d), lambda p, ab_ref, rel_ref: (0, 0))],
        out_specs=pl.BlockSpec((GROUP * SUB, d), lambda p, ab_ref, rel_ref: (p, 0)),
    )
    ye2 = ye.reshape(N_EXPERTS * cap, d)
    return pl.pallas_call(
        functools.partial(_uncompact_small_kernel, nsub),
        out_shape=jax.ShapeDtypeStruct((t, d), F32),
        grid_spec=grid_spec,
        compiler_params=_params(("arbitrary",)),
        name="uncompact_small",
    )(ab, rel, pos3, aff3, *([ye2] * N_EXPERTS), x1, mod3, nf)


def _floor16(v):
    return (v // 16) * 16


def _trunk(x, mod3, boff, w, experts):
    b, l, d = x.shape
    t = b * l
    cap = EC_FACTOR * t // N_EXPERTS

    if experts[0].dtype == BF16:
        z = _in_proj(x, mod3, boff, w["norm1_w"], w["w_in"])
    else:
        z, *experts = _in_proj(x, mod3, boff, w["norm1_w"], w["w_in"], experts)
    o = _gla(z, w["wpre_f"], w["bpre_f"], False, None)
    o = _gla(z, w["wpre_b"], w["bpre_b"], True, o)
    x1, u2, aff3 = _mix_out(x, o, z, mod3, boff, w["conv_w"], w["conv_b"],
                            w["w_out_a"], w["w_out_b"], w["w_o"], w["norm2_w"], w["w_router_t"])
    pos3, base3 = _select(aff3, cap)

    base = base3[:, :, 0].T
    start = jnp.minimum(_floor16(base), cap - WIN)
    tbig_c = min(2048, t)
    tbig_u = min(1024, t)
    while tbig_u + 32 > cap:
        tbig_u //= 2
    nsb_u = tbig_u // SUB
    abig = jnp.minimum(_floor16(base[:, ::nsb_u]), cap - (tbig_u + 32))
    rel = start - jnp.repeat(abig, nsb_u, axis=1)

    base_next = jnp.concatenate([base[:, 1:], jnp.full((N_EXPERTS, 1), cap, jnp.int32)], axis=1)
    start_small = jnp.minimum(_floor16(base), cap - SMALL_WIN)
    ab_pair = jnp.minimum(_floor16(base[:, ::GROUP]), cap - GROUP_WIN)
    rel_small = start_small - jnp.repeat(ab_pair, GROUP, axis=1)
    fits_small = (jnp.all(base_next - _floor16(base) <= SMALL_WIN)
                  & jnp.all(rel_small <= GROUP_WIN - SMALL_WIN))

    u2f = u2.reshape(t, d)
    start_blk = _floor16(base[:, ::2])
    next_blk = base_next[:, 1::2]
    xe = lax.cond(
        jnp.all(next_blk - start_blk <= CWIN),
        lambda: _compact_small(u2f, pos3, start_blk.reshape(-1), _floor16(next_blk).reshape(-1), cap),
        lambda: _compact(u2f, pos3, start, cap, tbig_c))
    ye = _ffn(xe, cap, *experts)
    x1f = x1.reshape(t, d)
    out = lax.cond(
        fits_small,
        lambda: _uncompact_small(ye, pos3, aff3, ab_pair.reshape(-1), rel_small.reshape(-1), x1f, mod3, boff,
                                 l, w["norm_f_w"]),
        lambda: _uncompact(ye, pos3, aff3, abig.reshape(-1), rel, x1f, mod3, boff, l, w["norm_f_w"], tbig_u))
    return out.reshape(b, l, d), tuple(experts)


def _prepare_weights(w_in, w_alpha_f, b_alpha_f, w_alpha_b, b_alpha_b, gla_norm_w, conv_w, conv_b,
                     w_out_a, w_out_b, w_o, norm1_w, norm2_w, w_router, norm_f_w):
    d = D_MODEL
    wq, wk, wv, wr, wlf, wlb, wcb, wcc, wcx, wga, wgb = jnp.split(
        w_in, np.cumsum([512, 512, 1024, 1024, 16, 16, 1024, 1024, 1024, 1024])[:].tolist(), axis=1)
    pad = jnp.zeros((d, Z_WIDTH - Z_LR - 2 * GATE_RANK), w_in.dtype)
    wq = wq * (HEAD_K ** -0.5)
    wqkv = jnp.concatenate([wq.reshape(d, GLA_HEADS, HEAD_K), wk.reshape(d, GLA_HEADS, HEAD_K),
                            wv.reshape(d, GLA_HEADS, HEAD_V)], axis=2).reshape(d, GLA_HEADS * HEAD_W)
    w_in_p = jnp.concatenate([wqkv, wr, wcb, wcc, wcx, wga, wgb, wlf, wlb, pad], axis=1).astype(BF16)

    def pre(wa, ba, row0):
        wh = wa.reshape(GATE_RANK, GLA_HEADS, HEAD_K).transpose(1, 0, 2)
        full = jnp.zeros((GLA_HEADS, 128, HEAD_K), F32).at[:, row0:row0 + GATE_RANK].set(wh)
        return full.astype(BF16), ba.reshape(GLA_HEADS, 1, HEAD_K)

    wpre_f, bpre_f = pre(w_alpha_f, b_alpha_f, 0)
    wpre_b, bpre_b = pre(w_alpha_b, b_alpha_b, GATE_RANK)
    return dict(
        w_in=w_in_p, wpre_f=wpre_f, bpre_f=bpre_f, wpre_b=wpre_b, bpre_b=bpre_b,
        conv_w=conv_w, conv_b=conv_b.reshape(1, d),
        w_out_a=(gla_norm_w.reshape(d, 1) * w_out_a).astype(BF16),
        w_out_b=w_out_b.astype(BF16), w_o=w_o.astype(BF16),
        norm1_w=norm1_w, norm2_w=norm2_w.reshape(1, d), w_router_t=w_router.T.astype(BF16),
        norm_f_w=norm_f_w.reshape(1, d))


def kernel(x_prompt, x_sample, c_prompt, c_sample, w_ada, b_ada, norm1_w, w_in, w_alpha_f, b_alpha_f, w_alpha_b, b_alpha_b, gla_norm_w, conv_w, conv_b, w_out_a, w_out_b, w_o, norm2_w, w_router, w_gate, w_up, w_down, norm_f_w):
    d = D_MODEL
    bp, bs = c_prompt.shape[0], c_sample.shape[0]
    nrows = -(-(bp + bs) // 8) * 8
    c_all = jnp.concatenate([c_prompt, c_sample, jnp.zeros((nrows - bp - bs, d), F32)], axis=0)
    mod3 = _adaln(c_all, w_ada[0], b_ada[0]).reshape(nrows, 6, d)
    w = _prepare_weights(w_in[0], w_alpha_f[0], b_alpha_f[0], w_alpha_b[0], b_alpha_b[0], gla_norm_w[0],
                         conv_w[0], conv_b[0], w_out_a[0], w_out_b[0], w_o[0], norm1_w[0], norm2_w[0],
                         w_router[0], norm_f_w)
    y_prompt, experts = _trunk(x_prompt, mod3, 0, w, (w_gate[0], w_up[0], w_down[0]))
    y_sample, _ = _trunk(x_sample, mod3, bp, w, experts)
    return (y_prompt, y_sample)
```

```python
import functools

import numpy as np
import jax
import jax.numpy as jnp
from jax import lax
from jax.experimental import pallas as pl
from jax.experimental.pallas import tpu as pltpu

F32 = jnp.float32
BF16 = jnp.bfloat16

D_MODEL = 1024
GLA_HEADS = 4
HEAD_K = 128
HEAD_V = 256
GATE_RANK = 16
GATE_TAU = 16.0
CHUNK = 64
N_EXPERTS = 16
EC_FACTOR = 2
D_EXPERT = 2816
EPS = 1e-6
LOG2E = 1.4426950408889634

Z_R, Z_CB, Z_CC, Z_CX, Z_GA, Z_GB, Z_LR = 2048, 3072, 4096, 5120, 6144, 7168, 8192
HEAD_W = 2 * HEAD_K + HEAD_V
Z_WIDTH = 8448

UNIT = 4 * CHUNK
SUB = 128
WIN = SUB + 16
SMALL_WIN = 64
GROUP = 4
GROUP_WIN = SMALL_WIN + 48 * (GROUP - 1)
CWIN = 96
FFN_TM = 1024
FFN_CHUNK = 768
V7X_VMEM_BYTES = 64 * 1024 * 1024
VMEM_LIMIT = V7X_VMEM_BYTES - 8 * 1024 * 1024


def _dot(a, b):
    return jnp.dot(a, b, preferred_element_type=F32)


def _dot_nt(a, b):
    return lax.dot_general(a, b, (((1,), (1,)), ((), ())), preferred_element_type=F32)


def _sigmoid(x):
    return 0.5 * jnp.tanh(0.5 * x) + 0.5


def _dot_tn(a, b):
    return lax.dot_general(a, b, (((0,), (0,)), ((), ())), preferred_element_type=F32)


def _params(sem, limit=VMEM_LIMIT):
    return pltpu.CompilerParams(dimension_semantics=sem, vmem_limit_bytes=limit)


def _adaln_kernel(c_ref, w_ref, b_ref, o_ref):
    c = c_ref[...]
    s = c * jax.nn.sigmoid(c)
    o_ref[...] = jnp.dot(s, w_ref[...], preferred_element_type=F32,
                         precision=lax.Precision.HIGHEST) + b_ref[...]


def _adaln(c_all, w_ada, b_ada):
    nb, d = c_all.shape
    n = w_ada.shape[1]
    tn = 1536
    return pl.pallas_call(
        _adaln_kernel,
        out_shape=jax.ShapeDtypeStruct((nb, n), F32),
        grid=(n // tn,),
        in_specs=[pl.BlockSpec((nb, d), lambda j: (0, 0)),
                  pl.BlockSpec((d, tn), lambda j: (0, j)),
                  pl.BlockSpec((1, tn), lambda j: (0, j))],
        out_specs=pl.BlockSpec((nb, tn), lambda j: (0, j)),
        compiler_params=_params(("arbitrary",)),
        name="adaln",
    )(c_all, w_ada, b_ada.reshape(1, n))


CAST_SPLIT = 8


def _inproj_kernel(ncast, x_ref, mod_ref, n1_ref, w_ref, *refs):
    z_ref = refs[len(refs) // 2]
    x = x_ref[0]
    ms = jnp.mean(x * x, axis=-1, keepdims=True)
    m = mod_ref[0]
    u = (x * lax.rsqrt(ms + EPS)) * (n1_ref[...] * (1.0 + m[1:2])) + m[0:1]
    z_ref[0] = _dot(u.astype(BF16), w_ref[...]).astype(BF16)

    if ncast:
        lin = ((pl.program_id(0) * pl.num_programs(1) + pl.program_id(1)) * pl.num_programs(2)
               + pl.program_id(2))

        @pl.when(lin < ncast)
        def _():
            for src, dst in zip(refs[:3], refs[4:]):
                dst[...] = src[...].astype(BF16)


def _in_proj(x, mod3, boff, norm1_w, w_in_p, expert_weights=None):
    b, l, d = x.shape
    tm = min(1024, l)
    tn = 2816
    nt, nn = l // tm, Z_WIDTH // tn
    in_specs = [pl.BlockSpec((1, tm, d), lambda n, bi, i: (bi, i, 0)),
                pl.BlockSpec((1, 6, d), lambda n, bi, i: (bi + boff, 0, 0)),
                pl.BlockSpec((1, d), lambda n, bi, i: (0, 0)),
                pl.BlockSpec((d, tn), lambda n, bi, i: (0, n))]
    out_specs = [pl.BlockSpec((1, tm, tn), lambda n, bi, i: (bi, i, n))]
    out_shape = [jax.ShapeDtypeStruct((b, l, Z_WIDTH), BF16)]
    args = [x, mod3, norm1_w.reshape(1, d), w_in_p]
    ncast = 0
    if expert_weights is not None and N_EXPERTS * CAST_SPLIT > b * nt * nn:
        return (_in_proj(x, mod3, boff, norm1_w, w_in_p),) + tuple(wt.astype(BF16) for wt in expert_weights)
    if expert_weights is not None:
        ncast = N_EXPERTS * CAST_SPLIT

        def slab(n, bi, i):
            j = jnp.minimum((n * b + bi) * nt + i, ncast - 1)
            return (j // CAST_SPLIT, j % CAST_SPLIT, 0)

        for wt in expert_weights:
            blk = (1, wt.shape[1] // CAST_SPLIT, wt.shape[2])
            in_specs.append(pl.BlockSpec(blk, slab))
            out_specs.append(pl.BlockSpec(blk, slab))
            out_shape.append(jax.ShapeDtypeStruct(wt.shape, BF16))
            args.append(wt)
    res = pl.pallas_call(
        functools.partial(_inproj_kernel, ncast),
        out_shape=tuple(out_shape),
        grid=(nn, b, nt),
        in_specs=in_specs,
        out_specs=tuple(out_specs),
        compiler_params=_params(("arbitrary", "arbitrary", "arbitrary")),
        name="in_proj",
    )(*args)
    return res if expert_weights is not None else res[0]


def _gla_constants(reverse):
    r = np.arange(UNIT)[:, None]
    s = np.arange(UNIT)[None, :]
    cr, cs = r // CHUNK, s // CHUNK
    if reverse:
        tri = (cr == cs) & (s >= r)
        tr, ts = 3 - cr, 3 - cs
    else:
        tri = (cr == cs) & (s <= r)
        tr, ts = cr, cs
    code = np.zeros((UNIT, UNIT), np.int32)
    code[tri] = 1
    return jnp.asarray(tri, BF16), jnp.asarray(code)


def _gla_kernel(reverse, nunits, has_prev, *refs):
    if has_prev:
        qkv_ref, lr_ref, wp_ref, bp_ref, tri_ref, code_ref, prev_ref, o_ref, s_scr = refs
    else:
        qkv_ref, lr_ref, wp_ref, bp_ref, tri_ref, code_ref, o_ref, s_scr = refs
        prev_ref = None

    def v_of(rows):
        return qkv_ref[0, rows, 2 * HEAD_K:HEAD_W]

    @pl.when(pl.program_id(2) == 0)
    def _():
        s_scr[...] = jnp.zeros_like(s_scr)

    order = (3, 2, 1, 0) if reverse else (0, 1, 2, 3)

    def to_mem(by_step):
        out = [None] * 4
        for t in range(4):
            out[order[t]] = by_step[t]
        return out

    def scale_chunks(arr, logf):
        parts = []
        for c in range(4):
            part = arr[c * CHUNK:(c + 1) * CHUNK]
            if logf[c] is not None:
                part = part * jnp.exp2(logf[c])
            parts.append(part)
        return jnp.concatenate(parts, axis=0).astype(BF16)

    def rows_of(ui):
        um = (nunits - 1 - ui) if reverse else ui
        return pl.ds(um * UNIT, UNIT)

    def decays(ui):
        pre = _dot(lr_ref[0, rows_of(ui), :], wp_ref[0]) + bp_ref[0]
        soft = jnp.log2(1.0 + jnp.exp2(jnp.abs(pre) * (-LOG2E)))
        la = (jnp.minimum(pre, 0.0) * LOG2E - soft) * (1.0 / GATE_TAU)
        hi = la.astype(BF16)
        lo = (la - hi.astype(F32)).astype(BF16)
        bb = _dot(tri_ref[...], jnp.concatenate([hi, lo], axis=1))
        return bb[:, :HEAD_K] + bb[:, HEAD_K:]

    def operands(ui, b):
        rows = rows_of(ui)
        edge = 0 if reverse else CHUNK - 1
        tot = [b[c * CHUNK + edge:c * CHUNK + edge + 1] for c in range(4)]
        tot_full = jnp.concatenate([jnp.broadcast_to(t, (CHUNK, HEAD_K)) for t in tot], axis=0)
        q = qkv_ref[0, rows, 0:HEAD_K].astype(F32)
        k = qkv_ref[0, rows, HEAD_K:2 * HEAD_K].astype(F32)
        q_t = q * jnp.exp2(b)
        k_t = (k * jnp.exp2(-b)).astype(BF16)
        k_e = k * jnp.exp2(tot_full - b)
        q_tb = q_t.astype(BF16)
        k_eb = k_e.astype(BF16)
        bt = [tot[o] for o in order]

        zero = jnp.zeros((CHUNK, HEAD_K), BF16)
        e1 = jnp.exp2(bt[1])
        e2 = jnp.exp2(bt[2])
        qrows, krows = [None] * 4, [None] * 4
        for t in range(4):
            c = order[t]
            sl = slice(c * CHUNK, (c + 1) * CHUNK)
            qc, kc = q_tb[sl], k_eb[sl]
            q3 = qc if t == 2 else ((q_t[sl] * e2).astype(BF16) if t == 3 else zero)
            k3 = (k_e[sl] * e1).astype(BF16) if t == 0 else (kc if t == 1 else zero)
            qrows[c] = jnp.concatenate([qc if t == 1 else zero, qc if t == 3 else zero, q3], axis=1)
            krows[c] = jnp.concatenate([kc if t == 0 else zero, kc if t == 2 else zero, k3], axis=1)

        gq = to_mem([None, bt[0], bt[0] + bt[1], bt[0] + bt[1] + bt[2]])
        gk = to_mem([bt[1] + bt[2] + bt[3], bt[2] + bt[3], bt[3], None])
        dec = jnp.exp2(bt[0] + bt[1] + bt[2] + bt[3])
        return dict(q_t=q_tb, k_t=k_t, q_cat=jnp.concatenate(qrows, axis=0), k_cat=jnp.concatenate(krows, axis=0),
                    qs=scale_chunks(q_t, gq), ks=scale_chunks(k_e, gk), dec=dec)

    def scores(p):
        same = _dot_nt(p["q_t"], p["k_t"])
        cross = _dot_nt(p["q_cat"], p["k_cat"])
        return jnp.where(code_ref[...] == 1, same, cross).astype(BF16)

    def finish(ui, p, intra):
        rows = rows_of(ui)
        st = s_scr[...]
        o = intra + _dot_nt(p["qs"], st.astype(BF16))
        s_scr[...] = st * p["dec"] + _dot_tn(v_of(rows), p["ks"])
        if prev_ref is not None:
            o = o + prev_ref[0, rows, :]
        o_ref[0, rows, :] = o

    stages = 5
    bs, ps, atts, intra = {}, {}, {}, {}
    for wave in range(nunits + stages - 1):
        for k in range(stages):
            ui = wave - k
            if not 0 <= ui < nunits:
                continue
            if k == 0:
                bs[ui] = decays(ui)
            elif k == 1:
                ps[ui] = operands(ui, bs.pop(ui))
            elif k == 2:
                atts[ui] = scores(ps[ui])
            elif k == 3:
                intra[ui] = _dot(atts.pop(ui), v_of(rows_of(ui)))
            else:
                finish(ui, ps.pop(ui), intra.pop(ui))


def _gla(z, wpre, bpre, reverse, prev):
    b, l, _ = z.shape
    lb = min(4096, l)
    nb = l // lb
    nunits = lb // UNIT
    tri, code = _gla_constants(reverse)

    def seq(i):
        return (nb - 1 - i) if reverse else i

    in_specs = [
        pl.BlockSpec((1, lb, HEAD_W), lambda bi, h, i: (bi, seq(i), h)),
        pl.BlockSpec((1, lb, 128), lambda bi, h, i: (bi, seq(i), Z_LR // 128)),
        pl.BlockSpec((1, 128, HEAD_K), lambda bi, h, i: (h, 0, 0)),
        pl.BlockSpec((1, 1, HEAD_K), lambda bi, h, i: (h, 0, 0)),
        pl.BlockSpec((UNIT, UNIT), lambda bi, h, i: (0, 0)),
        pl.BlockSpec((UNIT, UNIT), lambda bi, h, i: (0, 0)),
    ]
    args = [z, z, wpre, bpre, tri, code]
    if prev is not None:
        in_specs.append(pl.BlockSpec((1, lb, HEAD_V), lambda bi, h, i: (bi, seq(i), h)))
        args.append(prev)
    return pl.pallas_call(
        functools.partial(_gla_kernel, reverse, nunits, prev is not None),
        out_shape=jax.ShapeDtypeStruct((b, l, GLA_HEADS * HEAD_V), F32),
        grid=(b, GLA_HEADS, nb),
        in_specs=in_specs,
        out_specs=pl.BlockSpec((1, lb, HEAD_V), lambda bi, h, i: (bi, seq(i), h)),
        scratch_shapes=[pltpu.VMEM((HEAD_V, HEAD_K), F32)],
        compiler_params=_params(("arbitrary", "arbitrary", "arbitrary")),
        name="gla_bwd" if reverse else "gla_fwd",
    )(*args)


def _mix_kernel(tm, x_ref, o_ref, r_ref, cb_ref, cc_ref, cx_ref, ccp_ref, cxp_ref, ccn_ref, cxn_ref,
                ga_ref, gb_ref, mod_ref, cw_ref, cbias_ref, woa_ref, wob_ref, wo_ref,
                n2_ref, wrt_ref, x1_ref, u2_ref, aff_ref):
    i = pl.program_id(1)
    last = pl.num_programs(1) - 1

    o = o_ref[0]
    parts = []
    for h in range(GLA_HEADS):
        oh = o[:, h * HEAD_V:(h + 1) * HEAD_V]
        ms = jnp.mean(oh * oh, axis=-1, keepdims=True)
        parts.append(oh * lax.rsqrt(ms + EPS))
    on = jnp.concatenate(parts, axis=1)
    r = r_ref[0]
    branch_a = _dot(on.astype(BF16) * (r * _sigmoid(r)), woa_ref[...])

    xc = (cc_ref[0] * cx_ref[0]).astype(F32)
    prev_row = ccp_ref[0, 15:16, :].astype(F32) * cxp_ref[0, 15:16, :].astype(F32)
    next_row = ccn_ref[0, 0:1, :].astype(F32) * cxn_ref[0, 0:1, :].astype(F32)
    prev_row = jnp.where(i > 0, prev_row, 0.0)
    next_row = jnp.where(i < last, next_row, 0.0)
    rowid = lax.broadcasted_iota(jnp.int32, (tm, 1), 0)
    xm1 = jnp.where(rowid == 0, prev_row, pltpu.roll(xc, 1, axis=0))
    xp1 = jnp.where(rowid == tm - 1, next_row, pltpu.roll(xc, tm - 1, axis=0))
    cw = cw_ref[...]
    conv = xm1 * cw[0:1] + xc * cw[1:2] + xp1 * cw[2:3] + cbias_ref[...]
    branch_b = _dot(cb_ref[0] * conv.astype(BF16), wob_ref[...])

    merged = (_sigmoid(ga_ref[0]) * branch_a.astype(BF16)
              + _sigmoid(gb_ref[0]) * branch_b.astype(BF16))
    mix = _dot(merged, wo_ref[...])

    m = mod_ref[0]
    x1 = x_ref[0] + m[2:3] * mix
    x1_ref[0] = x1
    ms = jnp.mean(x1 * x1, axis=-1, keepdims=True)
    u2 = (x1 * lax.rsqrt(ms + EPS)) * (n2_ref[...] * (1.0 + m[4:5])) + m[3:4]
    u2b = u2.astype(BF16)
    u2_ref[0] = u2b

    logits = _dot_nt(wrt_ref[...], u2b)
    ex = jnp.exp(logits - jnp.max(logits, axis=0, keepdims=True))
    aff = ex / jnp.sum(ex, axis=0, keepdims=True)
    for j in range(tm // SUB):
        aff_ref[j] = aff[:, j * SUB:(j + 1) * SUB]


def _mix_out(x, o, z, mod3, boff, conv_w, conv_b, woa, wob, wo, n2, wrt):
    b, l, d = x.shape
    tm = min(512, l)
    nt = l // tm
    hb = tm // 16
    nhalo = l // 16

    def col(c):
        return lambda bi, i: (bi, i, c // d)

    in_specs = [
        pl.BlockSpec((1, tm, d), lambda bi, i: (bi, i, 0)),
        pl.BlockSpec((1, tm, d), lambda bi, i: (bi, i, 0)),
        pl.BlockSpec((1, tm, d), col(Z_R)),
        pl.BlockSpec((1, tm, d), col(Z_CB)),
        pl.BlockSpec((1, tm, d), col(Z_CC)),
        pl.BlockSpec((1, tm, d), col(Z_CX)),
        pl.BlockSpec((1, 16, d), lambda bi, i: (bi, jnp.maximum(i * hb - 1, 0), Z_CC // d)),
        pl.BlockSpec((1, 16, d), lambda bi, i: (bi, jnp.maximum(i * hb - 1, 0), Z_CX // d)),
        pl.BlockSpec((1, 16, d), lambda bi, i: (bi, jnp.minimum((i + 1) * hb, nhalo - 1), Z_CC // d)),
        pl.BlockSpec((1, 16, d), lambda bi, i: (bi, jnp.minimum((i + 1) * hb, nhalo - 1), Z_CX // d)),
        pl.BlockSpec((1, tm, d), col(Z_GA)),
        pl.BlockSpec((1, tm, d), col(Z_GB)),
        pl.BlockSpec((1, 6, d), lambda bi, i: (bi + boff, 0, 0)),
        pl.BlockSpec((3, d), lambda bi, i: (0, 0)),
        pl.BlockSpec((1, d), lambda bi, i: (0, 0)),
        pl.BlockSpec((d, d), lambda bi, i: (0, 0)),
        pl.BlockSpec((d, d), lambda bi, i: (0, 0)),
        pl.BlockSpec((d, d), lambda bi, i: (0, 0)),
        pl.BlockSpec((1, d), lambda bi, i: (0, 0)),
        pl.BlockSpec((N_EXPERTS, d), lambda bi, i: (0, 0)),
    ]
    out_shape = (jax.ShapeDtypeStruct((b, l, d), F32),
                 jax.ShapeDtypeStruct((b, l, d), BF16),
                 jax.ShapeDtypeStruct((b * l // SUB, N_EXPERTS, SUB), F32))
    out_specs = (pl.BlockSpec((1, tm, d), lambda bi, i: (bi, i, 0)),
                 pl.BlockSpec((1, tm, d), lambda bi, i: (bi, i, 0)),
                 pl.BlockSpec((tm // SUB, N_EXPERTS, SUB), lambda bi, i: (bi * nt + i, 0, 0)))
    return pl.pallas_call(
        functools.partial(_mix_kernel, tm),
        out_shape=out_shape,
        grid=(b, nt),
        in_specs=in_specs,
        out_specs=out_specs,
        compiler_params=_params(("arbitrary", "arbitrary")),
        name="mix_out",
    )(x, o, z, z, z, z, z, z, z, z, z, z, mod3, conv_w, conv_b, woa, wob, wo, n2, wrt)


def _select_kernel(cap, nsb, aff_ref, u_ref, pos_ref, base_ref):
    capf = float(cap)

    def bits(j):
        return pltpu.bitcast(aff_ref[j], jnp.int32)

    def count(pred):
        def body(j, acc):
            return acc + jnp.where(pred(bits(j)), 1.0, 0.0)
        acc = lax.fori_loop(0, nsb, body, jnp.zeros((N_EXPERTS, SUB), F32), unroll=8)
        return jnp.sum(acc, axis=1, keepdims=True)

    def bit_body(t, thr):
        cand = thr | jnp.left_shift(jnp.int32(1), 30 - t)
        return jnp.where(count(lambda v: v >= cand) >= capf, cand, thr)

    thr = lax.fori_loop(0, 31, bit_body, jnp.zeros((N_EXPERTS, SUB), jnp.int32))
    need = capf - count(lambda v: v > thr)

    def blk(j, carry):
        ceq, csel = carry
        v = bits(j)
        gtf = jnp.where(v > thr, 1.0, 0.0)
        eqf = jnp.where(v == thr, 1.0, 0.0)
        r1 = _dot(eqf.astype(BF16), u_ref[...])
        eq_rank = r1[:, :SUB] - eqf + ceq
        sel = gtf + jnp.where(eq_rank < need, eqf, 0.0)
        r2 = _dot(sel.astype(BF16), u_ref[...])
        pos_ref[j] = jnp.where(sel > 0.5, r2[:, :SUB] + csel - 1.0, -1.0).astype(jnp.int32)
        base_ref[j] = csel.astype(jnp.int32)
        return ceq + r1[:, SUB:], csel + r2[:, SUB:]

    zero = jnp.zeros((N_EXPERTS, SUB), F32)
    lax.fori_loop(0, nsb, blk, (zero, zero), unroll=8)


def _select(aff3, cap):
    nsb = aff3.shape[0]
    tri = np.triu(np.ones((SUB, SUB), np.float32))
    umat = jnp.asarray(np.concatenate([tri, np.ones((SUB, SUB), np.float32)], axis=1), BF16)
    shp = jax.ShapeDtypeStruct((nsb, N_EXPERTS, SUB), jnp.int32)
    return pl.pallas_call(
        functools.partial(_select_kernel, cap, nsb),
        out_shape=(shp, shp),
        compiler_params=pltpu.CompilerParams(vmem_limit_bytes=VMEM_LIMIT),
        name="select",
    )(aff3, umat)


def _compact_kernel(nsb, start_ref, pos_ref, x_ref, xe_ref):
    e = pl.program_id(0)

    @pl.when(pl.program_id(1) == 0)
    def _():
        xe_ref[...] = jnp.zeros_like(xe_ref)

    rowi = lax.broadcasted_iota(jnp.int32, (WIN, SUB), 0)

    def body(s, carry):
        st = pl.multiple_of(start_ref[0, 0, s], 16)
        local = pos_ref[s, pl.ds(e, 1), :] - st
        onehot = jnp.where(rowi == local, 1.0, 0.0).astype(BF16)
        xs = x_ref[pl.ds(s * SUB, SUB), :]
        rows = _dot(onehot, xs).astype(BF16)
        xe_ref[0, pl.ds(st, WIN), :] = xe_ref[0, pl.ds(st, WIN), :] + rows
        return carry

    for s in range(nsb):
        body(s, 0)


def _compact(u2, pos3, start, cap, tbig):
    t, d = u2.shape
    nbig = t // tbig
    nsb = tbig // SUB
    return pl.pallas_call(
        functools.partial(_compact_kernel, nsb),
        out_shape=jax.ShapeDtypeStruct((N_EXPERTS, cap + CWIN, d), BF16),
        grid=(N_EXPERTS, nbig),
        in_specs=[pl.BlockSpec((1, 1, nsb), lambda e, b: (e * nbig + b, 0, 0), memory_space=pltpu.SMEM),
                  pl.BlockSpec((nsb, N_EXPERTS, SUB), lambda e, b: (b, 0, 0)),
                  pl.BlockSpec((tbig, d), lambda e, b: (b, 0))],
        out_specs=pl.BlockSpec((1, cap + CWIN, d), lambda e, b: (e, 0, 0)),
        compiler_params=_params(("arbitrary", "arbitrary")),
        name="compact",
    )(start.reshape(N_EXPERTS * nbig, 1, nsb), pos3, u2)


def _compact_small_kernel(nblk, cap, start_ref, next_ref, pos_ref, x_ref, xe_ref, strip_buf, tail_buf, zero_buf,
                          sems, zsem):
    p = pl.program_id(0)
    last = pl.num_programs(0) - 1
    slot = lax.rem(p, 2)

    @pl.when(p == 0)
    def _():
        tail_buf[...] = jnp.zeros_like(tail_buf)
        zero_buf[...] = jnp.zeros_like(zero_buf)

    def strip_copy(sl, e, row0):
        return pltpu.make_async_copy(strip_buf.at[sl, e], xe_ref.at[e, pl.ds(row0, CWIN), :], sems.at[sl, e])

    pos = jnp.concatenate([pos_ref[0], pos_ref[1]], axis=1)
    rowi = lax.broadcasted_iota(jnp.int32, (CWIN, 2 * SUB), 0)
    starts = [pl.multiple_of(start_ref[e * nblk + p], 16) for e in range(N_EXPERTS)]
    onehot = jnp.concatenate(
        [jnp.where(rowi == pos[e:e + 1] - starts[e], 1.0, 0.0).astype(BF16) for e in range(N_EXPERTS)], axis=0)
    rows = _dot(onehot, x_ref[...]).astype(BF16)
    for e in range(N_EXPERTS):
        strip_buf[slot, e, 0:16, :] = rows[e * CWIN:e * CWIN + 16] + tail_buf[e]
        strip_buf[slot, e, 16:CWIN, :] = rows[e * CWIN + 16:(e + 1) * CWIN]
        t0 = next_ref[e * nblk + p] - starts[e]
        t0c = pl.multiple_of(jnp.minimum(t0, CWIN - 16), 16)
        tail_buf[e] = jnp.where(t0 < CWIN, strip_buf[slot, e, pl.ds(t0c, 16), :], jnp.zeros((16, 1), BF16))

    for e in range(N_EXPERTS):
        @pl.when(p > 0)
        def _():
            strip_copy(1 - slot, e, starts[e]).wait()
        strip_copy(slot, e, starts[e]).start()

    @pl.when(p == last)
    def _():
        for e in range(N_EXPERTS):
            strip_copy(slot, e, starts[e]).wait()
            pad = pltpu.make_async_copy(zero_buf, xe_ref.at[e, pl.ds(cap, CWIN), :], zsem)
            pad.start()
            pad.wait()


def _compact_small(u2, pos3, start, start_next, cap):
    t, d = u2.shape
    nblk = t // (2 * SUB)
    grid_spec = pltpu.PrefetchScalarGridSpec(
        num_scalar_prefetch=2,
        grid=(nblk,),
        in_specs=[pl.BlockSpec((2, N_EXPERTS, SUB), lambda p, a, b: (p, 0, 0)),
                  pl.BlockSpec((2 * SUB, d), lambda p, a, b: (p, 0))],
        out_specs=pl.BlockSpec(memory_space=pl.ANY),
        scratch_shapes=[pltpu.VMEM((2, N_EXPERTS, CWIN, d), BF16),
                        pltpu.VMEM((N_EXPERTS, 16, d), BF16),
                        pltpu.VMEM((CWIN, d), BF16),
                        pltpu.SemaphoreType.DMA((2, N_EXPERTS)),
                        pltpu.SemaphoreType.DMA(())],
    )
    return pl.pallas_call(
        functools.partial(_compact_small_kernel, nblk, cap),
        out_shape=jax.ShapeDtypeStruct((N_EXPERTS, cap + CWIN, d), BF16),
        grid_spec=grid_spec,
        compiler_params=_params(("arbitrary",)),
        name="compact_small",
    )(start, start_next, pos3, u2)


def _ffn_kernel(chunks, x_ref, wg_ref, wu_ref, wd_ref, o_ref, acc_ref):
    xs = x_ref[0]
    for i, (c0, cw) in enumerate(chunks):
        g = _dot(xs, wg_ref[0, :, c0:c0 + cw])
        u = _dot(xs, wu_ref[0, :, c0:c0 + cw])
        h = ((g * jax.nn.sigmoid(g)) * u).astype(BF16)
        part = _dot(h, wd_ref[0, c0:c0 + cw, :])
        if i == 0:
            acc_ref[...] = part
        else:
            acc_ref[...] += part
    o_ref[0] = acc_ref[...].astype(BF16)


def _ffn(xe, cap, wg, wu, wd):
    e, _, d = xe.shape
    f = wg.shape[2]
    tm = min(FFN_TM, cap)
    chunks = tuple((c0, min(FFN_CHUNK, f - c0)) for c0 in range(0, f, FFN_CHUNK))
    return pl.pallas_call(
        functools.partial(_ffn_kernel, chunks),
        out_shape=jax.ShapeDtypeStruct((e, cap, d), BF16),
        grid=(e, cap // tm),
        in_specs=[pl.BlockSpec((1, tm, d), lambda ei, m: (ei, m, 0)),
                  pl.BlockSpec((1, d, f), lambda ei, m: (ei, 0, 0)),
                  pl.BlockSpec((1, d, f), lambda ei, m: (ei, 0, 0)),
                  pl.BlockSpec((1, f, d), lambda ei, m: (ei, 0, 0))],
        out_specs=pl.BlockSpec((1, tm, d), lambda ei, m: (ei, m, 0)),
        scratch_shapes=[pltpu.VMEM((tm, d), F32)],
        compiler_params=_params(("arbitrary", "arbitrary")),
        name="expert_ffn",
    )(xe, wg, wu, wd)


def _uncompact_kernel(nsb, nbig, abig_ref, rel_ref, pos_ref, aff_ref, ye_ref, x1_ref, mod_ref, nf_ref,
                      out_ref, acc_ref):
    b = pl.program_id(0)
    e = pl.program_id(1)

    @pl.when(e == 0)
    def _():
        acc_ref[...] = jnp.zeros_like(acc_ref)

    rowi = lax.broadcasted_iota(jnp.int32, (WIN, SUB), 0)
    big = abig_ref[e * nbig + b]

    def body(s, carry):
        rel = pl.multiple_of(rel_ref[0, 0, s], 16)
        local = pos_ref[s, pl.ds(e, 1), :] - (big + rel)
        gate = aff_ref[s, pl.ds(e, 1), :]
        weights = jnp.where(rowi == local, gate, 0.0).astype(BF16)
        rows = pl.ds(s * SUB, SUB)
        acc_ref[rows, :] += _dot_tn(weights, ye_ref[pl.ds(rel, WIN), :])
        return carry

    for s in range(nsb):
        body(s, 0)

    @pl.when(e == pl.num_programs(1) - 1)
    def _():
        xo = x1_ref[...] + mod_ref[0][5:6] * acc_ref[...]
        ms = jnp.mean(xo * xo, axis=-1, keepdims=True)
        out_ref[...] = xo * lax.rsqrt(ms + EPS) * nf_ref[...]


def _uncompact(ye, pos3, aff3, abig, rel, x1, mod3, boff, seq_len, nf, tbig):
    t, d = x1.shape
    cap = ye.shape[1]
    nbig = t // tbig
    nsb = tbig // SUB
    wbig = tbig + 32
    grid_spec = pltpu.PrefetchScalarGridSpec(
        num_scalar_prefetch=1,
        grid=(nbig, N_EXPERTS),
        in_specs=[
            pl.BlockSpec((1, 1, nsb), lambda b, e, ab: (e * nbig + b, 0, 0), memory_space=pltpu.SMEM),
            pl.BlockSpec((nsb, N_EXPERTS, SUB), lambda b, e, ab: (b, 0, 0)),
            pl.BlockSpec((nsb, N_EXPERTS, SUB), lambda b, e, ab: (b, 0, 0)),
            pl.BlockSpec((pl.Element(wbig), pl.Element(d)),
                         lambda b, e, ab: (pl.multiple_of(e * cap + ab[e * nbig + b], 16), 0)),
            pl.BlockSpec((tbig, d), lambda b, e, ab: (b, 0)),
            pl.BlockSpec((1, 6, d), lambda b, e, ab: (boff + (b * tbig) // seq_len, 0, 0)),
            pl.BlockSpec((1, d), lambda b, e, ab: (0, 0)),
        ],
        out_specs=pl.BlockSpec((tbig, d), lambda b, e, ab: (b, 0)),
        scratch_shapes=[pltpu.VMEM((tbig, d), F32)],
    )
    return pl.pallas_call(
        functools.partial(_uncompact_kernel, nsb, nbig),
        out_shape=jax.ShapeDtypeStruct((t, d), F32),
        grid_spec=grid_spec,
        compiler_params=_params(("arbitrary", "arbitrary")),
        name="uncompact",
    )(abig, rel.reshape(N_EXPERTS * nbig, 1, nsb), pos3, aff3, ye.reshape(N_EXPERTS * cap, d), x1, mod3, nf)


def _uncompact_small_kernel(nsub, ab_ref, rel_ref, pos_ref, aff_ref, *refs):
    ye_refs = refs[:N_EXPERTS]
    x1_ref, mod_ref, nf_ref, out_ref = refs[N_EXPERTS:]
    p = pl.program_id(0)
    npair = nsub // GROUP
    rowi = lax.broadcasted_iota(jnp.int32, (SMALL_WIN, SUB), 0)
    ys = []
    for j in range(GROUP):
        pos = pos_ref[j]
        aff = aff_ref[j]
        weights, windows = [], []
        for e in range(N_EXPERTS):
            rel = pl.multiple_of(rel_ref[e * nsub + GROUP * p + j], 16)
            local = pos[e:e + 1] - (ab_ref[e * npair + p] + rel)
            weights.append(jnp.where(rowi == local, aff[e:e + 1], 0.0).astype(BF16))
            windows.append(ye_refs[e][pl.ds(rel, SMALL_WIN), :])
        ys.append(_dot_tn(jnp.concatenate(weights, axis=0), jnp.concatenate(windows, axis=0)))
    xo = x1_ref[...] + mod_ref[0][5:6] * jnp.concatenate(ys, axis=0)
    ms = jnp.mean(xo * xo, axis=-1, keepdims=True)
    out_ref[...] = xo * lax.rsqrt(ms + EPS) * nf_ref[...]


def _uncompact_small(ye, pos3, aff3, ab, rel, x1, mod3, boff, seq_len, nf):
    t, d = x1.shape
    cap = ye.shape[1]
    nsub = t // SUB
    npair = nsub // GROUP

    def window(e):
        return pl.BlockSpec((pl.Element(GROUP_WIN), pl.Element(d)),
                            lambda p, ab_ref, rel_ref: (pl.multiple_of(e * cap + ab_ref[e * npair + p], 16), 0))

    grid_spec = pltpu.PrefetchScalarGridSpec(
        num_scalar_prefetch=2,
        grid=(npair,),
        in_specs=[pl.BlockSpec((GROUP, N_EXPERTS, SUB), lambda p, ab_ref, rel_ref: (p, 0, 0)),
                  pl.BlockSpec((GROUP, N_EXPERTS, SUB), lambda p, ab_ref, rel_ref: (p, 0, 0))]
        + [window(e) for e in range(N_EXPERTS)]
        + [pl.BlockSpec((GROUP * SUB, d), lambda p, ab_ref, rel_ref: (p, 0)),
           pl.BlockSpec((1, 6, d), lambda p, ab_ref, rel_ref: (boff + (p * GROUP * SUB) // seq_len, 0, 0)),
           pl.BlockSpec((1, d), lambda p, ab_ref, rel_ref: (0, 0))],
        out_specs=pl.BlockSpec((GROUP * SUB, d), lambda p, ab_ref, rel_ref: (p, 0)),
    )
    ye2 = ye.reshape(N_EXPERTS * cap, d)
    return pl.pallas_call(
        functools.partial(_uncompact_small_kernel, nsub),
        out_shape=jax.ShapeDtypeStruct((t, d), F32),
        grid_spec=grid_spec,
        compiler_params=_params(("arbitrary",)),
        name="uncompact_small",
    )(ab, rel, pos3, aff3, *([ye2] * N_EXPERTS), x1, mod3, nf)


def _floor16(v):
    return (v // 16) * 16


def _trunk(x, mod3, boff, w, experts):
    b, l, d = x.shape
    t = b * l
    cap = EC_FACTOR * t // N_EXPERTS

    if experts[0].dtype == BF16:
        z = _in_proj(x, mod3, boff, w["norm1_w"], w["w_in"])
    else:
        z, *experts = _in_proj(x, mod3, boff, w["norm1_w"], w["w_in"], experts)
    o = _gla(z, w["wpre_f"], w["bpre_f"], False, None)
    o = _gla(z, w["wpre_b"], w["bpre_b"], True, o)
    x1, u2, aff3 = _mix_out(x, o, z, mod3, boff, w["conv_w"], w["conv_b"],
                            w["w_out_a"], w["w_out_b"], w["w_o"], w["norm2_w"], w["w_router_t"])
    pos3, base3 = _select(aff3, cap)

    base = base3[:, :, 0].T
    start = jnp.minimum(_floor16(base), cap - WIN)
    tbig_c = min(2048, t)
    tbig_u = min(1024, t)
    while tbig_u + 32 > cap:
        tbig_u //= 2
    nsb_u = tbig_u // SUB
    abig = jnp.minimum(_floor16(base[:, ::nsb_u]), cap - (tbig_u + 32))
    rel = start - jnp.repeat(abig, nsb_u, axis=1)

    base_next = jnp.concatenate([base[:, 1:], jnp.full((N_EXPERTS, 1), cap, jnp.int32)], axis=1)
    start_small = jnp.minimum(_floor16(base), cap - SMALL_WIN)
    ab_pair = jnp.minimum(_floor16(base[:, ::GROUP]), cap - GROUP_WIN)
    rel_small = start_small - jnp.repeat(ab_pair, GROUP, axis=1)
    fits_small = (jnp.all(base_next - _floor16(base) <= SMALL_WIN)
                  & jnp.all(rel_small <= GROUP_WIN - SMALL_WIN))

    u2f = u2.reshape(t, d)
    start_blk = _floor16(base[:, ::2])
    next_blk = base_next[:, 1::2]
    xe = lax.cond(
        jnp.all(next_blk - start_blk <= CWIN),
        lambda: _compact_small(u2f, pos3, start_blk.reshape(-1), _floor16(next_blk).reshape(-1), cap),
        lambda: _compact(u2f, pos3, start, cap, tbig_c))
    ye = _ffn(xe, cap, *experts)
    x1f = x1.reshape(t, d)
    out = lax.cond(
        fits_small,
        lambda: _uncompact_small(ye, pos3, aff3, ab_pair.reshape(-1), rel_small.reshape(-1), x1f, mod3, boff,
                                 l, w["norm_f_w"]),
        lambda: _uncompact(ye, pos3, aff3, abig.reshape(-1), rel, x1f, mod3, boff, l, w["norm_f_w"], tbig_u))
    return out.reshape(b, l, d), tuple(experts)


def _prepare_weights(w_in, w_alpha_f, b_alpha_f, w_alpha_b, b_alpha_b, gla_norm_w, conv_w, conv_b,
                     w_out_a, w_out_b, w_o, norm1_w, norm2_w, w_router, norm_f_w):
    d = D_MODEL
    wq, wk, wv, wr, wlf, wlb, wcb, wcc, wcx, wga, wgb = jnp.split(
        w_in, np.cumsum([512, 512, 1024, 1024, 16, 16, 1024, 1024, 1024, 1024])[:].tolist(), axis=1)
    pad = jnp.zeros((d, Z_WIDTH - Z_LR - 2 * GATE_RANK), w_in.dtype)
    wq = wq * (HEAD_K ** -0.5)
    wqkv = jnp.concatenate([wq.reshape(d, GLA_HEADS, HEAD_K), wk.reshape(d, GLA_HEADS, HEAD_K),
                            wv.reshape(d, GLA_HEADS, HEAD_V)], axis=2).reshape(d, GLA_HEADS * HEAD_W)
    w_in_p = jnp.concatenate([wqkv, wr, wcb, wcc, wcx, wga, wgb, wlf, wlb, pad], axis=1).astype(BF16)

    def pre(wa, ba, row0):
        wh = wa.reshape(GATE_RANK, GLA_HEADS, HEAD_K).transpose(1, 0, 2)
        full = jnp.zeros((GLA_HEADS, 128, HEAD_K), F32).at[:, row0:row0 + GATE_RANK].set(wh)
        return full.astype(BF16), ba.reshape(GLA_HEADS, 1, HEAD_K)

    wpre_f, bpre_f = pre(w_alpha_f, b_alpha_f, 0)
    wpre_b, bpre_b = pre(w_alpha_b, b_alpha_b, GATE_RANK)
    return dict(
        w_in=w_in_p, wpre_f=wpre_f, bpre_f=bpre_f, wpre_b=wpre_b, bpre_b=bpre_b,
        conv_w=conv_w, conv_b=conv_b.reshape(1, d),
        w_out_a=(gla_norm_w.reshape(d, 1) * w_out_a).astype(BF16),
        w_out_b=w_out_b.astype(BF16), w_o=w_o.astype(BF16),
        norm1_w=norm1_w, norm2_w=norm2_w.reshape(1, d), w_router_t=w_router.T.astype(BF16),
        norm_f_w=norm_f_w.reshape(1, d))


def kernel(x_prompt, x_sample, c_prompt, c_sample, w_ada, b_ada, norm1_w, w_in, w_alpha_f, b_alpha_f, w_alpha_b, b_alpha_b, gla_norm_w, conv_w, conv_b, w_out_a, w_out_b, w_o, norm2_w, w_router, w_gate, w_up, w_down, norm_f_w):
    d = D_MODEL
    bp, bs = c_prompt.shape[0], c_sample.shape[0]
    nrows = -(-(bp + bs) // 8) * 8
    c_all = jnp.concatenate([c_prompt, c_sample, jnp.zeros((nrows - bp - bs, d), F32)], axis=0)
    mod3 = _adaln(c_all, w_ada[0], b_ada[0]).reshape(nrows, 6, d)
    w = _prepare_weights(w_in[0], w_alpha_f[0], b_alpha_f[0], w_alpha_b[0], b_alpha_b[0], gla_norm_w[0],
                         conv_w[0], conv_b[0], w_out_a[0], w_out_b[0], w_o[0], norm1_w[0], norm2_w[0],
                         w_router[0], norm_f_w)
    y_prompt, experts = _trunk(x_prompt, mod3, 0, w, (w_gate[0], w_up[0], w_down[0]))
    y_sample, _ = _trunk(x_sample, mod3, bp, w, experts)
    return (y_prompt, y_sample)
```

```python
import functools

import numpy as np
import jax
import jax.numpy as jnp
from jax import lax
from jax.experimental import pallas as pl
from jax.experimental.pallas import tpu as pltpu

F32 = jnp.float32
BF16 = jnp.bfloat16

D_MODEL = 1024
GLA_HEADS = 4
HEAD_K = 128
HEAD_V = 256
GATE_RANK = 16
GATE_TAU = 16.0
CHUNK = 64
N_EXPERTS = 16
EC_FACTOR = 2
D_EXPERT = 2816
EPS = 1e-6
LOG2E = 1.4426950408889634

Z_R, Z_CB, Z_CC, Z_CX, Z_GA, Z_GB, Z_LR = 2048, 3072, 4096, 5120, 6144, 7168, 8192
HEAD_W = 2 * HEAD_K + HEAD_V
Z_WIDTH = 8448

UNIT = 4 * CHUNK
SUB = 128
WIN = SUB + 16
SMALL_WIN = 64
GROUP = 4
GROUP_WIN = SMALL_WIN + 48 * (GROUP - 1)
CWIN = 96
FFN_TM = 1024
FFN_CHUNK = 768
V7X_VMEM_BYTES = 64 * 1024 * 1024
VMEM_LIMIT = V7X_VMEM_BYTES - 8 * 1024 * 1024


def _dot(a, b):
    return jnp.dot(a, b, preferred_element_type=F32)


def _dot_nt(a, b):
    return lax.dot_general(a, b, (((1,), (1,)), ((), ())), preferred_element_type=F32)


def _sigmoid(x):
    return 0.5 * jnp.tanh(0.5 * x) + 0.5


def _dot_tn(a, b):
    return lax.dot_general(a, b, (((0,), (0,)), ((), ())), preferred_element_type=F32)


def _params(sem, limit=VMEM_LIMIT):
    return pltpu.CompilerParams(dimension_semantics=sem, vmem_limit_bytes=limit)


def _adaln_kernel(c_ref, w_ref, b_ref, o_ref):
    c = c_ref[...]
    s = c * jax.nn.sigmoid(c)
    o_ref[...] = jnp.dot(s, w_ref[...], preferred_element_type=F32,
                         precision=lax.Precision.HIGHEST) + b_ref[...]


def _adaln(c_all, w_ada, b_ada):
    nb, d = c_all.shape
    n = w_ada.shape[1]
    tn = 1536
    return pl.pallas_call(
        _adaln_kernel,
        out_shape=jax.ShapeDtypeStruct((nb, n), F32),
        grid=(n // tn,),
        in_specs=[pl.BlockSpec((nb, d), lambda j: (0, 0)),
                  pl.BlockSpec((d, tn), lambda j: (0, j)),
                  pl.BlockSpec((1, tn), lambda j: (0, j))],
        out_specs=pl.BlockSpec((nb, tn), lambda j: (0, j)),
        compiler_params=_params(("arbitrary",)),
        name="adaln",
    )(c_all, w_ada, b_ada.reshape(1, n))


CAST_SPLIT = 8


def _inproj_kernel(ncast, x_ref, mod_ref, n1_ref, w_ref, *refs):
    z_ref = refs[len(refs) // 2]
    x = x_ref[0]
    ms = jnp.mean(x * x, axis=-1, keepdims=True)
    m = mod_ref[0]
    u = (x * lax.rsqrt(ms + EPS)) * (n1_ref[...] * (1.0 + m[1:2])) + m[0:1]
    z_ref[0] = _dot(u.astype(BF16), w_ref[...]).astype(BF16)

    if ncast:
        lin = ((pl.program_id(0) * pl.num_programs(1) + pl.program_id(1)) * pl.num_programs(2)
               + pl.program_id(2))

        @pl.when(lin < ncast)
        def _():
            for src, dst in zip(refs[:3], refs[4:]):
                dst[...] = src[...].astype(BF16)


def _in_proj(x, mod3, boff, norm1_w, w_in_p, expert_weights=None):
    b, l, d = x.shape
    tm = min(1024, l)
    tn = 2816
    nt, nn = l // tm, Z_WIDTH // tn
    in_specs = [pl.BlockSpec((1, tm, d), lambda n, bi, i: (bi, i, 0)),
                pl.BlockSpec((1, 6, d), lambda n, bi, i: (bi + boff, 0, 0)),
                pl.BlockSpec((1, d), lambda n, bi, i: (0, 0)),
                pl.BlockSpec((d, tn), lambda n, bi, i: (0, n))]
    out_specs = [pl.BlockSpec((1, tm, tn), lambda n, bi, i: (bi, i, n))]
    out_shape = [jax.ShapeDtypeStruct((b, l, Z_WIDTH), BF16)]
    args = [x, mod3, norm1_w.reshape(1, d), w_in_p]
    ncast = 0
    if expert_weights is not None and N_EXPERTS * CAST_SPLIT > b * nt * nn:
        return (_in_proj(x, mod3, boff, norm1_w, w_in_p),) + tuple(wt.astype(BF16) for wt in expert_weights)
    if expert_weights is not None:
        ncast = N_EXPERTS * CAST_SPLIT

        def slab(n, bi, i):
            j = jnp.minimum((n * b + bi) * nt + i, ncast - 1)
            return (j // CAST_SPLIT, j % CAST_SPLIT, 0)

        for wt in expert_weights:
            blk = (1, wt.shape[1] // CAST_SPLIT, wt.shape[2])
            in_specs.append(pl.BlockSpec(blk, slab))
            out_specs.append(pl.BlockSpec(blk, slab))
            out_shape.append(jax.ShapeDtypeStruct(wt.shape, BF16))
            args.append(wt)
    res = pl.pallas_call(
        functools.partial(_inproj_kernel, ncast),
        out_shape=tuple(out_shape),
        grid=(nn, b, nt),
        in_specs=in_specs,
        out_specs=tuple(out_specs),
        compiler_params=_params(("arbitrary", "arbitrary", "arbitrary")),
        name="in_proj",
    )(*args)
    return res if expert_weights is not None else res[0]


def _gla_constants(reverse):
    r = np.arange(UNIT)[:, None]
    s = np.arange(UNIT)[None, :]
    cr, cs = r // CHUNK, s // CHUNK
    if reverse:
        tri = (cr == cs) & (s >= r)
        tr, ts = 3 - cr, 3 - cs
    else:
        tri = (cr == cs) & (s <= r)
        tr, ts = cr, cs
    code = np.zeros((UNIT, UNIT), np.int32)
    code[tri] = 1
    return jnp.asarray(tri, BF16), jnp.asarray(code)


def _gla_kernel(reverse, nunits, has_prev, *refs):
    if has_prev:
        qkv_ref, lr_ref, wp_ref, bp_ref, tri_ref, code_ref, prev_ref, o_ref, s_scr = refs
    else:
        qkv_ref, lr_ref, wp_ref, bp_ref, tri_ref, code_ref, o_ref, s_scr = refs
        prev_ref = None

    def v_of(rows):
        return qkv_ref[0, rows, 2 * HEAD_K:HEAD_W]

    @pl.when(pl.program_id(2) == 0)
    def _():
        s_scr[...] = jnp.zeros_like(s_scr)

    order = (3, 2, 1, 0) if reverse else (0, 1, 2, 3)

    def to_mem(by_step):
        out = [None] * 4
        for t in range(4):
            out[order[t]] = by_step[t]
        return out

    def scale_chunks(arr, logf):
        parts = []
        for c in range(4):
            part = arr[c * CHUNK:(c + 1) * CHUNK]
            if logf[c] is not None:
                part = part * jnp.exp2(logf[c])
            parts.append(part)
        return jnp.concatenate(parts, axis=0).astype(BF16)

    def rows_of(ui):
        um = (nunits - 1 - ui) if reverse else ui
        return pl.ds(um * UNIT, UNIT)

    def decays(ui):
        pre = _dot(lr_ref[0, rows_of(ui), :], wp_ref[0]) + bp_ref[0]
        soft = jnp.log2(1.0 + jnp.exp2(jnp.abs(pre) * (-LOG2E)))
        la = (jnp.minimum(pre, 0.0) * LOG2E - soft) * (1.0 / GATE_TAU)
        hi = la.astype(BF16)
        lo = (la - hi.astype(F32)).astype(BF16)
        bb = _dot(tri_ref[...], jnp.concatenate([hi, lo], axis=1))
        return bb[:, :HEAD_K] + bb[:, HEAD_K:]

    def operands(ui, b):
        rows = rows_of(ui)
        edge = 0 if reverse else CHUNK - 1
        tot = [b[c * CHUNK + edge:c * CHUNK + edge + 1] for c in range(4)]
        tot_full = jnp.concatenate([jnp.broadcast_to(t, (CHUNK, HEAD_K)) for t in tot], axis=0)
        q = qkv_ref[0, rows, 0:HEAD_K].astype(F32)
        k = qkv_ref[0, rows, HEAD_K:2 * HEAD_K].astype(F32)
        q_t = q * jnp.exp2(b)
        k_t = (k * jnp.exp2(-b)).astype(BF16)
        k_e = k * jnp.exp2(tot_full - b)
        q_tb = q_t.astype(BF16)
        k_eb = k_e.astype(BF16)
        bt = [tot[o] for o in order]

        zero = jnp.zeros((CHUNK, HEAD_K), BF16)
        e1 = jnp.exp2(bt[1])
        e2 = jnp.exp2(bt[2])
        qrows, krows = [None] * 4, [None] * 4
        for t in range(4):
            c = order[t]
            sl = slice(c * CHUNK, (c + 1) * CHUNK)
            qc, kc = q_tb[sl], k_eb[sl]
            q3 = qc if t == 2 else ((q_t[sl] * e2).astype(BF16) if t == 3 else zero)
            k3 = (k_e[sl] * e1).astype(BF16) if t == 0 else (kc if t == 1 else zero)
            qrows[c] = jnp.concatenate([qc if t == 1 else zero, qc if t == 3 else zero, q3], axis=1)
            krows[c] = jnp.concatenate([kc if t == 0 else zero, kc if t == 2 else zero, k3], axis=1)

        gq = to_mem([None, bt[0], bt[0] + bt[1], bt[0] + bt[1] + bt[2]])
        gk = to_mem([bt[1] + bt[2] + bt[3], bt[2] + bt[3], bt[3], None])
        dec = jnp.exp2(bt[0] + bt[1] + bt[2] + bt[3])
        return dict(q_t=q_tb, k_t=k_t, q_cat=jnp.concatenate(qrows, axis=0), k_cat=jnp.concatenate(krows, axis=0),
                    qs=scale_chunks(q_t, gq), ks=scale_chunks(k_e, gk), dec=dec)

    def scores(p):
        same = _dot_nt(p["q_t"], p["k_t"])
        cross = _dot_nt(p["q_cat"], p["k_cat"])
        return jnp.where(code_ref[...] == 1, same, cross).astype(BF16)

    def finish(ui, p, intra):
        rows = rows_of(ui)
        st = s_scr[...]
        o = intra + _dot(p["qs"], st.astype(BF16))
        dec_col = jnp.transpose(jnp.broadcast_to(p["dec"], (8, HEAD_K)))[:, 0:1]
        s_scr[...] = st * dec_col + _dot_tn(p["ks"], v_of(rows))
        if prev_ref is not None:
            o = o + prev_ref[0, rows, :]
        o_ref[0, rows, :] = o

    stages = 5
    bs, ps, atts, intra = {}, {}, {}, {}
    for wave in range(nunits + stages - 1):
        for k in range(stages):
            ui = wave - k
            if not 0 <= ui < nunits:
                continue
            if k == 0:
                bs[ui] = decays(ui)
            elif k == 1:
                ps[ui] = operands(ui, bs.pop(ui))
            elif k == 2:
                atts[ui] = scores(ps[ui])
            elif k == 3:
                intra[ui] = _dot(atts.pop(ui), v_of(rows_of(ui)))
            else:
                finish(ui, ps.pop(ui), intra.pop(ui))


def _gla(z, wpre, bpre, reverse, prev):
    b, l, _ = z.shape
    lb = min(4096, l)
    nb = l // lb
    nunits = lb // UNIT
    tri, code = _gla_constants(reverse)

    def seq(i):
        return (nb - 1 - i) if reverse else i

    in_specs = [
        pl.BlockSpec((1, lb, HEAD_W), lambda bi, h, i: (bi, seq(i), h)),
        pl.BlockSpec((1, lb, 128), lambda bi, h, i: (bi, seq(i), Z_LR // 128)),
        pl.BlockSpec((1, 128, HEAD_K), lambda bi, h, i: (h, 0, 0)),
        pl.BlockSpec((1, 1, HEAD_K), lambda bi, h, i: (h, 0, 0)),
        pl.BlockSpec((UNIT, UNIT), lambda bi, h, i: (0, 0)),
        pl.BlockSpec((UNIT, UNIT), lambda bi, h, i: (0, 0)),
    ]
    args = [z, z, wpre, bpre, tri, code]
    if prev is not None:
        in_specs.append(pl.BlockSpec((1, lb, HEAD_V), lambda bi, h, i: (bi, seq(i), h)))
        args.append(prev)
    return pl.pallas_call(
        functools.partial(_gla_kernel, reverse, nunits, prev is not None),
        out_shape=jax.ShapeDtypeStruct((b, l, GLA_HEADS * HEAD_V), F32),
        grid=(b, GLA_HEADS, nb),
        in_specs=in_specs,
        out_specs=pl.BlockSpec((1, lb, HEAD_V), lambda bi, h, i: (bi, seq(i), h)),
        scratch_shapes=[pltpu.VMEM((HEAD_K, HEAD_V), F32)],
        compiler_params=_params(("arbitrary", "arbitrary", "arbitrary")),
        name="gla_bwd" if reverse else "gla_fwd",
    )(*args)


def _mix_kernel(tm, x_ref, o_ref, r_ref, cb_ref, cc_ref, cx_ref, ccp_ref, cxp_ref, ccn_ref, cxn_ref,
                ga_ref, gb_ref, mod_ref, cw_ref, cbias_ref, woa_ref, wob_ref, wo_ref,
                n2_ref, wrt_ref, x1_ref, u2_ref, aff_ref):
    i = pl.program_id(1)
    last = pl.num_programs(1) - 1

    o = o_ref[0]
    parts = []
    for h in range(GLA_HEADS):
        oh = o[:, h * HEAD_V:(h + 1) * HEAD_V]
        ms = jnp.mean(oh * oh, axis=-1, keepdims=True)
        parts.append(oh * lax.rsqrt(ms + EPS))
    on = jnp.concatenate(parts, axis=1)
    r = r_ref[0]
    branch_a = _dot(on.astype(BF16) * (r * _sigmoid(r)), woa_ref[...])

    xc = (cc_ref[0] * cx_ref[0]).astype(F32)
    prev_row = ccp_ref[0, 15:16, :].astype(F32) * cxp_ref[0, 15:16, :].astype(F32)
    next_row = ccn_ref[0, 0:1, :].astype(F32) * cxn_ref[0, 0:1, :].astype(F32)
    prev_row = jnp.where(i > 0, prev_row, 0.0)
    next_row = jnp.where(i < last, next_row, 0.0)
    rowid = lax.broadcasted_iota(jnp.int32, (tm, 1), 0)
    xm1 = jnp.where(rowid == 0, prev_row, pltpu.roll(xc, 1, axis=0))
    xp1 = jnp.where(rowid == tm - 1, next_row, pltpu.roll(xc, tm - 1, axis=0))
    cw = cw_ref[...]
    conv = xm1 * cw[0:1] + xc * cw[1:2] + xp1 * cw[2:3] + cbias_ref[...]
    branch_b = _dot(cb_ref[0] * conv.astype(BF16), wob_ref[...])

    merged = (_sigmoid(ga_ref[0]) * branch_a.astype(BF16)
              + _sigmoid(gb_ref[0]) * branch_b.astype(BF16))
    mix = _dot(merged, wo_ref[...])

    m = mod_ref[0]
    x1 = x_ref[0] + m[2:3] * mix
    x1_ref[0] = x1
    ms = jnp.mean(x1 * x1, axis=-1, keepdims=True)
    u2 = (x1 * lax.rsqrt(ms + EPS)) * (n2_ref[...] * (1.0 + m[4:5])) + m[3:4]
    u2b = u2.astype(BF16)
    u2_ref[0] = u2b

    logits = _dot_nt(wrt_ref[...], u2b)
    ex = jnp.exp(logits - jnp.max(logits, axis=0, keepdims=True))
    aff = ex / jnp.sum(ex, axis=0, keepdims=True)
    for j in range(tm // SUB):
        aff_ref[j] = aff[:, j * SUB:(j + 1) * SUB]


def _mix_out(x, o, z, mod3, boff, conv_w, conv_b, woa, wob, wo, n2, wrt):
    b, l, d = x.shape
    tm = min(512, l)
    nt = l // tm
    hb = tm // 16
    nhalo = l // 16

    def col(c):
        return lambda bi, i: (bi, i, c // d)

    in_specs = [
        pl.BlockSpec((1, tm, d), lambda bi, i: (bi, i, 0)),
        pl.BlockSpec((1, tm, d), lambda bi, i: (bi, i, 0)),
        pl.BlockSpec((1, tm, d), col(Z_R)),
        pl.BlockSpec((1, tm, d), col(Z_CB)),
        pl.BlockSpec((1, tm, d), col(Z_CC)),
        pl.BlockSpec((1, tm, d), col(Z_CX)),
        pl.BlockSpec((1, 16, d), lambda bi, i: (bi, jnp.maximum(i * hb - 1, 0), Z_CC // d)),
        pl.BlockSpec((1, 16, d), lambda bi, i: (bi, jnp.maximum(i * hb - 1, 0), Z_CX // d)),
        pl.BlockSpec((1, 16, d), lambda bi, i: (bi, jnp.minimum((i + 1) * hb, nhalo - 1), Z_CC // d)),
        pl.BlockSpec((1, 16, d), lambda bi, i: (bi, jnp.minimum((i + 1) * hb, nhalo - 1), Z_CX // d)),
        pl.BlockSpec((1, tm, d), col(Z_GA)),
        pl.BlockSpec((1, tm, d), col(Z_GB)),
        pl.BlockSpec((1, 6, d), lambda bi, i: (bi + boff, 0, 0)),
        pl.BlockSpec((3, d), lambda bi, i: (0, 0)),
        pl.BlockSpec((1, d), lambda bi, i: (0, 0)),
        pl.BlockSpec((d, d), lambda bi, i: (0, 0)),
        pl.BlockSpec((d, d), lambda bi, i: (0, 0)),
        pl.BlockSpec((d, d), lambda bi, i: (0, 0)),
        pl.BlockSpec((1, d), lambda bi, i: (0, 0)),
        pl.BlockSpec((N_EXPERTS, d), lambda bi, i: (0, 0)),
    ]
    out_shape = (jax.ShapeDtypeStruct((b, l, d), F32),
                 jax.ShapeDtypeStruct((b, l, d), BF16),
                 jax.ShapeDtypeStruct((b * l // SUB, N_EXPERTS, SUB), F32))
    out_specs = (pl.BlockSpec((1, tm, d), lambda bi, i: (bi, i, 0)),
                 pl.BlockSpec((1, tm, d), lambda bi, i: (bi, i, 0)),
                 pl.BlockSpec((tm // SUB, N_EXPERTS, SUB), lambda bi, i: (bi * nt + i, 0, 0)))
    return pl.pallas_call(
        functools.partial(_mix_kernel, tm),
        out_shape=out_shape,
        grid=(b, nt),
        in_specs=in_specs,
        out_specs=out_specs,
        compiler_params=_params(("arbitrary", "arbitrary")),
        name="mix_out",
    )(x, o, z, z, z, z, z, z, z, z, z, z, mod3, conv_w, conv_b, woa, wob, wo, n2, wrt)


def _select_kernel(cap, nsb, aff_ref, u_ref, pos_ref, base_ref):
    capf = float(cap)

    def bits(j):
        return pltpu.bitcast(aff_ref[j], jnp.int32)

    def count(pred):
        def body(j, acc):
            return acc + jnp.where(pred(bits(j)), 1.0, 0.0)
        acc = lax.fori_loop(0, nsb, body, jnp.zeros((N_EXPERTS, SUB), F32), unroll=8)
        return jnp.sum(acc, axis=1, keepdims=True)

    def bit_body(t, thr):
        cand = thr | jnp.left_shift(jnp.int32(1), 30 - t)
        return jnp.where(count(lambda v: v >= cand) >= capf, cand, thr)

    thr = lax.fori_loop(0, 31, bit_body, jnp.zeros((N_EXPERTS, SUB), jnp.int32))
    need = capf - count(lambda v: v > thr)

    def blk(j, carry):
        ceq, csel = carry
        v = bits(j)
        gtf = jnp.where(v > thr, 1.0, 0.0)
        eqf = jnp.where(v == thr, 1.0, 0.0)
        r1 = _dot(eqf.astype(BF16), u_ref[...])
        eq_rank = r1[:, :SUB] - eqf + ceq
        sel = gtf + jnp.where(eq_rank < need, eqf, 0.0)
        r2 = _dot(sel.astype(BF16), u_ref[...])
        pos_ref[j] = jnp.where(sel > 0.5, r2[:, :SUB] + csel - 1.0, -1.0).astype(jnp.int32)
        base_ref[j] = csel.astype(jnp.int32)
        return ceq + r1[:, SUB:], csel + r2[:, SUB:]

    zero = jnp.zeros((N_EXPERTS, SUB), F32)
    lax.fori_loop(0, nsb, blk, (zero, zero), unroll=8)


def _select(aff3, cap):
    nsb = aff3.shape[0]
    tri = np.triu(np.ones((SUB, SUB), np.float32))
    umat = jnp.asarray(np.concatenate([tri, np.ones((SUB, SUB), np.float32)], axis=1), BF16)
    shp = jax.ShapeDtypeStruct((nsb, N_EXPERTS, SUB), jnp.int32)
    return pl.pallas_call(
        functools.partial(_select_kernel, cap, nsb),
        out_shape=(shp, shp),
        compiler_params=pltpu.CompilerParams(vmem_limit_bytes=VMEM_LIMIT),
        name="select",
    )(aff3, umat)


def _compact_kernel(nsb, start_ref, pos_ref, x_ref, xe_ref):
    e = pl.program_id(0)

    @pl.when(pl.program_id(1) == 0)
    def _():
        xe_ref[...] = jnp.zeros_like(xe_ref)

    rowi = lax.broadcasted_iota(jnp.int32, (WIN, SUB), 0)

    def body(s, carry):
        st = pl.multiple_of(start_ref[0, 0, s], 16)
        local = pos_ref[s, pl.ds(e, 1), :] - st
        onehot = jnp.where(rowi == local, 1.0, 0.0).astype(BF16)
        xs = x_ref[pl.ds(s * SUB, SUB), :]
        rows = _dot(onehot, xs).astype(BF16)
        xe_ref[0, pl.ds(st, WIN), :] = xe_ref[0, pl.ds(st, WIN), :] + rows
        return carry

    for s in range(nsb):
        body(s, 0)


def _compact(u2, pos3, start, cap, tbig):
    t, d = u2.shape
    nbig = t // tbig
    nsb = tbig // SUB
    return pl.pallas_call(
        functools.partial(_compact_kernel, nsb),
        out_shape=jax.ShapeDtypeStruct((N_EXPERTS, cap + CWIN, d), BF16),
        grid=(N_EXPERTS, nbig),
        in_specs=[pl.BlockSpec((1, 1, nsb), lambda e, b: (e * nbig + b, 0, 0), memory_space=pltpu.SMEM),
                  pl.BlockSpec((nsb, N_EXPERTS, SUB), lambda e, b: (b, 0, 0)),
                  pl.BlockSpec((tbig, d), lambda e, b: (b, 0))],
        out_specs=pl.BlockSpec((1, cap + CWIN, d), lambda e, b: (e, 0, 0)),
        compiler_params=_params(("arbitrary", "arbitrary")),
        name="compact",
    )(start.reshape(N_EXPERTS * nbig, 1, nsb), pos3, u2)


def _compact_small_kernel(nblk, cap, start_ref, next_ref, pos_ref, x_ref, xe_ref, strip_buf, tail_buf, zero_buf,
                          sems, zsem):
    p = pl.program_id(0)
    last = pl.num_programs(0) - 1
    slot = lax.rem(p, 2)

    @pl.when(p == 0)
    def _():
        tail_buf[...] = jnp.zeros_like(tail_buf)
        zero_buf[...] = jnp.zeros_like(zero_buf)

    def strip_copy(sl, e, row0):
        return pltpu.make_async_copy(strip_buf.at[sl, e], xe_ref.at[e, pl.ds(row0, CWIN), :], sems.at[sl, e])

    pos = jnp.concatenate([pos_ref[0], pos_ref[1]], axis=1)
    rowi = lax.broadcasted_iota(jnp.int32, (CWIN, 2 * SUB), 0)
    starts = [pl.multiple_of(start_ref[e * nblk + p], 16) for e in range(N_EXPERTS)]
    onehot = jnp.concatenate(
        [jnp.where(rowi == pos[e:e + 1] - starts[e], 1.0, 0.0).astype(BF16) for e in range(N_EXPERTS)], axis=0)
    rows = _dot(onehot, x_ref[...]).astype(BF16)
    for e in range(N_EXPERTS):
        strip_buf[slot, e, 0:16, :] = rows[e * CWIN:e * CWIN + 16] + tail_buf[e]
        strip_buf[slot, e, 16:CWIN, :] = rows[e * CWIN + 16:(e + 1) * CWIN]
        t0 = next_ref[e * nblk + p] - starts[e]
        t0c = pl.multiple_of(jnp.minimum(t0, CWIN - 16), 16)
        tail_buf[e] = jnp.where(t0 < CWIN, strip_buf[slot, e, pl.ds(t0c, 16), :], jnp.zeros((16, 1), BF16))

    for e in range(N_EXPERTS):
        @pl.when(p > 0)
        def _():
            strip_copy(1 - slot, e, starts[e]).wait()
        strip_copy(slot, e, starts[e]).start()

    @pl.when(p == last)
    def _():
        for e in range(N_EXPERTS):
            strip_copy(slot, e, starts[e]).wait()
            pad = pltpu.make_async_copy(zero_buf, xe_ref.at[e, pl.ds(cap, CWIN), :], zsem)
            pad.start()
            pad.wait()


def _compact_small(u2, pos3, start, start_next, cap):
    t, d = u2.shape
    nblk = t // (2 * SUB)
    grid_spec = pltpu.PrefetchScalarGridSpec(
        num_scalar_prefetch=2,
        grid=(nblk,),
        in_specs=[pl.BlockSpec((2, N_EXPERTS, SUB), lambda p, a, b: (p, 0, 0)),
                  pl.BlockSpec((2 * SUB, d), lambda p, a, b: (p, 0))],
        out_specs=pl.BlockSpec(memory_space=pl.ANY),
        scratch_shapes=[pltpu.VMEM((2, N_EXPERTS, CWIN, d), BF16),
                        pltpu.VMEM((N_EXPERTS, 16, d), BF16),
                        pltpu.VMEM((CWIN, d), BF16),
                        pltpu.SemaphoreType.DMA((2, N_EXPERTS)),
                        pltpu.SemaphoreType.DMA(())],
    )
    return pl.pallas_call(
        functools.partial(_compact_small_kernel, nblk, cap),
        out_shape=jax.ShapeDtypeStruct((N_EXPERTS, cap + CWIN, d), BF16),
        grid_spec=grid_spec,
        compiler_params=_params(("arbitrary",)),
        name="compact_small",
    )(start, start_next, pos3, u2)


def _ffn_kernel(chunks, x_ref, wg_ref, wu_ref, wd_ref, o_ref, acc_ref):
    xs = x_ref[0]
    for i, (c0, cw) in enumerate(chunks):
        g = _dot(xs, wg_ref[0, :, c0:c0 + cw])
        u = _dot(xs, wu_ref[0, :, c0:c0 + cw])
        h = ((g * jax.nn.sigmoid(g)) * u).astype(BF16)
        part = _dot(h, wd_ref[0, c0:c0 + cw, :])
        if i == 0:
            acc_ref[...] = part
        else:
            acc_ref[...] += part
    o_ref[0] = acc_ref[...].astype(BF16)


def _ffn(xe, cap, wg, wu, wd):
    e, _, d = xe.shape
    f = wg.shape[2]
    tm = min(FFN_TM, cap)
    chunks = tuple((c0, min(FFN_CHUNK, f - c0)) for c0 in range(0, f, FFN_CHUNK))
    return pl.pallas_call(
        functools.partial(_ffn_kernel, chunks),
        out_shape=jax.ShapeDtypeStruct((e, cap, d), BF16),
        grid=(e, cap // tm),
        in_specs=[pl.BlockSpec((1, tm, d), lambda ei, m: (ei, m, 0)),
                  pl.BlockSpec((1, d, f), lambda ei, m: (ei, 0, 0)),
                  pl.BlockSpec((1, d, f), lambda ei, m: (ei, 0, 0)),
                  pl.BlockSpec((1, f, d), lambda ei, m: (ei, 0, 0))],
        out_specs=pl.BlockSpec((1, tm, d), lambda ei, m: (ei, m, 0)),
        scratch_shapes=[pltpu.VMEM((tm, d), F32)],
        compiler_params=_params(("arbitrary", "arbitrary")),
        name="expert_ffn",
    )(xe, wg, wu, wd)


def _uncompact_kernel(nsb, nbig, abig_ref, rel_ref, pos_ref, aff_ref, ye_ref, x1_ref, mod_ref, nf_ref,
                      out_ref, acc_ref):
    b = pl.program_id(0)
    e = pl.program_id(1)

    @pl.when(e == 0)
    def _():
        acc_ref[...] = jnp.zeros_like(acc_ref)

    rowi = lax.broadcasted_iota(jnp.int32, (WIN, SUB), 0)
    big = abig_ref[e * nbig + b]

    def body(s, carry):
        rel = pl.multiple_of(rel_ref[0, 0, s], 16)
        local = pos_ref[s, pl.ds(e, 1), :] - (big + rel)
        gate = aff_ref[s, pl.ds(e, 1), :]
        weights = jnp.where(rowi == local, gate, 0.0).astype(BF16)
        rows = pl.ds(s * SUB, SUB)
        acc_ref[rows, :] += _dot_tn(weights, ye_ref[pl.ds(rel, WIN), :])
        return carry

    for s in range(nsb):
        body(s, 0)

    @pl.when(e == pl.num_programs(1) - 1)
    def _():
        xo = x1_ref[...] + mod_ref[0][5:6] * acc_ref[...]
        ms = jnp.mean(xo * xo, axis=-1, keepdims=True)
        out_ref[...] = xo * lax.rsqrt(ms + EPS) * nf_ref[...]


def _uncompact(ye, pos3, aff3, abig, rel, x1, mod3, boff, seq_len, nf, tbig):
    t, d = x1.shape
    cap = ye.shape[1]
    nbig = t // tbig
    nsb = tbig // SUB
    wbig = tbig + 32
    grid_spec = pltpu.PrefetchScalarGridSpec(
        num_scalar_prefetch=1,
        grid=(nbig, N_EXPERTS),
        in_specs=[
            pl.BlockSpec((1, 1, nsb), lambda b, e, ab: (e * nbig + b, 0, 0), memory_space=pltpu.SMEM),
            pl.BlockSpec((nsb, N_EXPERTS, SUB), lambda b, e, ab: (b, 0, 0)),
            pl.BlockSpec((nsb, N_EXPERTS, SUB), lambda b, e, ab: (b, 0, 0)),
            pl.BlockSpec((pl.Element(wbig), pl.Element(d)),
                         lambda b, e, ab: (pl.multiple_of(e * cap + ab[e * nbig + b], 16), 0)),
            pl.BlockSpec((tbig, d), lambda b, e, ab: (b, 0)),
            pl.BlockSpec((1, 6, d), lambda b, e, ab: (boff + (b * tbig) // seq_len, 0, 0)),
            pl.BlockSpec((1, d), lambda b, e, ab: (0, 0)),
        ],
        out_specs=pl.BlockSpec((tbig, d), lambda b, e, ab: (b, 0)),
        scratch_shapes=[pltpu.VMEM((tbig, d), F32)],
    )
    return pl.pallas_call(
        functools.partial(_uncompact_kernel, nsb, nbig),
        out_shape=jax.ShapeDtypeStruct((t, d), F32),
        grid_spec=grid_spec,
        compiler_params=_params(("arbitrary", "arbitrary")),
        name="uncompact",
    )(abig, rel.reshape(N_EXPERTS * nbig, 1, nsb), pos3, aff3, ye.reshape(N_EXPERTS * cap, d), x1, mod3, nf)


def _uncompact_small_kernel(nsub, ab_ref, rel_ref, pos_ref, aff_ref, *refs):
    ye_refs = refs[:N_EXPERTS]
    x1_ref, mod_ref, nf_ref, out_ref = refs[N_EXPERTS:]
    p = pl.program_id(0)
    npair = nsub // GROUP
    rowi = lax.broadcasted_iota(jnp.int32, (SMALL_WIN, SUB), 0)
    ys = []
    for j in range(GROUP):
        pos = pos_ref[j]
        aff = aff_ref[j]
        weights, windows = [], []
        for e in range(N_EXPERTS):
            rel = pl.multiple_of(rel_ref[e * nsub + GROUP * p + j], 16)
            local = pos[e:e + 1] - (ab_ref[e * npair + p] + rel)
            weights.append(jnp.where(rowi == local, aff[e:e + 1], 0.0).astype(BF16))
            windows.append(ye_refs[e][pl.ds(rel, SMALL_WIN), :])
        ys.append(_dot_tn(jnp.concatenate(weights, axis=0), jnp.concatenate(windows, axis=0)))
    xo = x1_ref[...] + mod_ref[0][5:6] * jnp.concatenate(ys, axis=0)
    ms = jnp.mean(xo * xo, axis=-1, keepdims=True)
    out_ref[...] = xo * lax.rsqrt(ms + EPS) * nf_ref[...]


def _uncompact_small(ye, pos3, aff3, ab, rel, x1, mod3, boff, seq_len, nf):
    t, d = x1.shape
    cap = ye.shape[1]
    nsub = t // SUB
    npair = nsub // GROUP

    def window(e):
        return pl.BlockSpec((pl.Element(GROUP_WIN), pl.Element(d)),
                            lambda p, ab_ref, rel_ref: (pl.multiple_of(e * cap + ab_ref[e * npair + p], 16), 0))

    grid_spec = pltpu.PrefetchScalarGridSpec(
        num_scalar_prefetch=2,
        grid=(npair,),
        in_specs=[pl.BlockSpec((GROUP, N_EXPERTS, SUB), lambda p, ab_ref, rel_ref: (p, 0, 0)),
                  pl.BlockSpec((GROUP, N_EXPERTS, SUB), lambda p, ab_ref, rel_ref: (p, 0, 0))]
        + [window(e) for e in range(N_EXPERTS)]
        + [pl.BlockSpec((GROUP * SUB, d), lambda p, ab_ref, rel_ref: (p, 0)),
           pl.BlockSpec((1, 6, d), lambda p, ab_ref, rel_ref: (boff + (p * GROUP * SUB) // seq_len, 0, 0)),
           pl.BlockSpec((1, d), lambda p, ab_ref, rel_ref: (0, 0))],
        out_specs=pl.BlockSpec((GROUP * SUB, d), lambda p, ab_ref, rel_ref: (p, 0)),
    )
    ye2 = ye.reshape(N_EXPERTS * cap, d)
    return pl.pallas_call(
        functools.partial(_uncompact_small_kernel, nsub),
        out_shape=jax.ShapeDtypeStruct((t, d), F32),
        grid_spec=grid_spec,
        compiler_params=_params(("arbitrary",)),
        name="uncompact_small",
    )(ab, rel, pos3, aff3, *([ye2] * N_EXPERTS), x1, mod3, nf)


def _floor16(v):
    return (v // 16) * 16


def _trunk(x, mod3, boff, w, experts):
    b, l, d = x.shape
    t = b * l
    cap = EC_FACTOR * t // N_EXPERTS

    if experts[0].dtype == BF16:
        z = _in_proj(x, mod3, boff, w["norm1_w"], w["w_in"])
    else:
        z, *experts = _in_proj(x, mod3, boff, w["norm1_w"], w["w_in"], experts)
    o = _gla(z, w["wpre_f"], w["bpre_f"], False, None)
    o = _gla(z, w["wpre_b"], w["bpre_b"], True, o)
    x1, u2, aff3 = _mix_out(x, o, z, mod3, boff, w["conv_w"], w["conv_b"],
                            w["w_out_a"], w["w_out_b"], w["w_o"], w["norm2_w"], w["w_router_t"])
    pos3, base3 = _select(aff3, cap)

    base = base3[:, :, 0].T
    start = jnp.minimum(_floor16(base), cap - WIN)
    tbig_c = min(2048, t)
    tbig_u = min(1024, t)
    while tbig_u + 32 > cap:
        tbig_u //= 2
    nsb_u = tbig_u // SUB
    abig = jnp.minimum(_floor16(base[:, ::nsb_u]), cap - (tbig_u + 32))
    rel = start - jnp.repeat(abig, nsb_u, axis=1)

    base_next = jnp.concatenate([base[:, 1:], jnp.full((N_EXPERTS, 1), cap, jnp.int32)], axis=1)
    start_small = jnp.minimum(_floor16(base), cap - SMALL_WIN)
    ab_pair = jnp.minimum(_floor16(base[:, ::GROUP]), cap - GROUP_WIN)
    rel_small = start_small - jnp.repeat(ab_pair, GROUP, axis=1)
    fits_small = (jnp.all(base_next - _floor16(base) <= SMALL_WIN)
                  & jnp.all(rel_small <= GROUP_WIN - SMALL_WIN))

    u2f = u2.reshape(t, d)
    start_blk = _floor16(base[:, ::2])
    next_blk = base_next[:, 1::2]
    xe = lax.cond(
        jnp.all(next_blk - start_blk <= CWIN),
        lambda: _compact_small(u2f, pos3, start_blk.reshape(-1), _floor16(next_blk).reshape(-1), cap),
        lambda: _compact(u2f, pos3, start, cap, tbig_c))
    ye = _ffn(xe, cap, *experts)
    x1f = x1.reshape(t, d)
    out = lax.cond(
        fits_small,
        lambda: _uncompact_small(ye, pos3, aff3, ab_pair.reshape(-1), rel_small.reshape(-1), x1f, mod3, boff,
                                 l, w["norm_f_w"]),
        lambda: _uncompact(ye, pos3, aff3, abig.reshape(-1), rel, x1f, mod3, boff, l, w["norm_f_w"], tbig_u))
    return out.reshape(b, l, d), tuple(experts)


def _prepare_weights(w_in, w_alpha_f, b_alpha_f, w_alpha_b, b_alpha_b, gla_norm_w, conv_w, conv_b,
                     w_out_a, w_out_b, w_o, norm1_w, norm2_w, w_router, norm_f_w):
    d = D_MODEL
    wq, wk, wv, wr, wlf, wlb, wcb, wcc, wcx, wga, wgb = jnp.split(
        w_in, np.cumsum([512, 512, 1024, 1024, 16, 16, 1024, 1024, 1024, 1024])[:].tolist(), axis=1)
    pad = jnp.zeros((d, Z_WIDTH - Z_LR - 2 * GATE_RANK), w_in.dtype)
    wq = wq * (HEAD_K ** -0.5)
    wqkv = jnp.concatenate([wq.reshape(d, GLA_HEADS, HEAD_K), wk.reshape(d, GLA_HEADS, HEAD_K),
                            wv.reshape(d, GLA_HEADS, HEAD_V)], axis=2).reshape(d, GLA_HEADS * HEAD_W)
    w_in_p = jnp.concatenate([wqkv, wr, wcb, wcc, wcx, wga, wgb, wlf, wlb, pad], axis=1).astype(BF16)

    def pre(wa, ba, row0):
        wh = wa.reshape(GATE_RANK, GLA_HEADS, HEAD_K).transpose(1, 0, 2)
        full = jnp.zeros((GLA_HEADS, 128, HEAD_K), F32).at[:, row0:row0 + GATE_RANK].set(wh)
        return full.astype(BF16), ba.reshape(GLA_HEADS, 1, HEAD_K)

    wpre_f, bpre_f = pre(w_alpha_f, b_alpha_f, 0)
    wpre_b, bpre_b = pre(w_alpha_b, b_alpha_b, GATE_RANK)
    return dict(
        w_in=w_in_p, wpre_f=wpre_f, bpre_f=bpre_f, wpre_b=wpre_b, bpre_b=bpre_b,
        conv_w=conv_w, conv_b=conv_b.reshape(1, d),
        w_out_a=(gla_norm_w.reshape(d, 1) * w_out_a).astype(BF16),
        w_out_b=w_out_b.astype(BF16), w_o=w_o.astype(BF16),
        norm1_w=norm1_w, norm2_w=norm2_w.reshape(1, d), w_router_t=w_router.T.astype(BF16),
        norm_f_w=norm_f_w.reshape(1, d))


def kernel(x_prompt, x_sample, c_prompt, c_sample, w_ada, b_ada, norm1_w, w_in, w_alpha_f, b_alpha_f, w_alpha_b, b_alpha_b, gla_norm_w, conv_w, conv_b, w_out_a, w_out_b, w_o, norm2_w, w_router, w_gate, w_up, w_down, norm_f_w):
    d = D_MODEL
    bp, bs = c_prompt.shape[0], c_sample.shape[0]
    nrows = -(-(bp + bs) // 8) * 8
    c_all = jnp.concatenate([c_prompt, c_sample, jnp.zeros((nrows - bp - bs, d), F32)], axis=0)
    mod3 = _adaln(c_all, w_ada[0], b_ada[0]).reshape(nrows, 6, d)
    w = _prepare_weights(w_in[0], w_alpha_f[0], b_alpha_f[0], w_alpha_b[0], b_alpha_b[0], gla_norm_w[0],
                         conv_w[0], conv_b[0], w_out_a[0], w_out_b[0], w_o[0], norm1_w[0], norm2_w[0],
                         w_router[0], norm_f_w)
    y_prompt, experts = _trunk(x_prompt, mod3, 0, w, (w_gate[0], w_up[0], w_down[0]))
    y_sample, _ = _trunk(x_sample, mod3, bp, w, experts)
    return (y_prompt, y_sample)
```
